```python
import jax, jax.numpy as jnp
from jax import lax
import numpy as np

D_MODEL = 1024
BATCH = 8
SEQ = 2048
DEPTH = 1

RWKV_HEADS = 8
RWKV_HEAD_DIM = 64
RWKV_WIDTH = RWKV_HEADS * RWKV_HEAD_DIM
RET_HEADS = 4
RET_HEAD_DIM = 128
RET_WIDTH = RET_HEADS * RET_HEAD_DIM
MIX_WIDTH = RWKV_WIDTH + RET_WIDTH
DECAY_LORA = 64
AAA_LORA = 64
GATE_LORA = 128
IN_COLS = 3 * RWKV_WIDTH + 4 * RET_WIDTH
IN_SPLITS = (RWKV_WIDTH, 2 * RWKV_WIDTH, 3 * RWKV_WIDTH,
             3 * RWKV_WIDTH + RET_WIDTH, 3 * RWKV_WIDTH + 2 * RET_WIDTH,
             3 * RWKV_WIDTH + 3 * RET_WIDTH)
RET_CHUNK = 128
ROPE_BASE = 10000.0
D_FF = 2816
CONV_WIDTH = 3
NORM_EPS = 1e-6
RWKV_GN_EPS = 64e-5
RET_GN_EPS = 1e-5

kernel_name = "hymba_rwkv7_retnet_convglu"


def rms_norm(x, g):
    xf = x.astype(jnp.float32)
    y = xf * lax.rsqrt(jnp.mean(xf * xf, axis=-1, keepdims=True) + NORM_EPS)
    return (y * g.astype(jnp.float32)).astype(x.dtype)


def token_shift(x):
    return jnp.pad(x, ((0, 0), (1, 0), (0, 0)))[:, :-1, :]


def head_norm(y, eps):
    mu = jnp.mean(y, axis=-1, keepdims=True)
    var = jnp.mean(jnp.square(y - mu), axis=-1, keepdims=True)
    yn = (y - mu) * lax.rsqrt(var + eps)
    return yn.reshape(y.shape[0], y.shape[1], -1)


def rotary(x, positions):
    half = x.shape[-1] // 2
    inv_freq = ROPE_BASE ** (-jnp.arange(half, dtype=jnp.float32) / half)
    ang = positions.astype(jnp.float32)[:, None] * inv_freq[None, :]
    cos = jnp.cos(ang)[None, :, None, :]
    sin = jnp.sin(ang)[None, :, None, :]
    x1, x2 = x[..., :half], x[..., half:]
    return jnp.concatenate([x1 * cos - x2 * sin, x1 * sin + x2 * cos], axis=-1)


def rwkv7_scan(r, w, k, v, a, b):
    Bsz, T, H, D = r.shape
    seq_first = lambda z: jnp.swapaxes(z, 0, 1)

    def step(S, inp):
        r_t, w_t, k_t, v_t, a_t, b_t = inp
        sa = jnp.einsum('bhij,bhj->bhi', S, a_t)
        S = (S * w_t[:, :, None, :] + sa[..., None] * b_t[:, :, None, :]
             + v_t[..., None] * k_t[:, :, None, :])
        y = jnp.einsum('bhij,bhj->bhi', S, r_t)
        return S, y

    S0 = jnp.zeros((Bsz, H, D, D), jnp.float32)
    _, ys = lax.scan(step, S0, tuple(seq_first(z) for z in (r, w, k, v, a, b)))
    return seq_first(ys)


def rwkv7_group(xn, p_r, p_k, p_v, mu_r, mu_k, mu_v, mu_w, mu_a, mu_g,
                w0, w1, w2, a0, a1, a2, g1, g2, k_k, k_a, r_k, lnx_w, lnx_b):
    Bsz, T, _ = xn.shape
    f32 = jnp.float32
    heads = lambda z: z.reshape(Bsz, T, RWKV_HEADS, RWKV_HEAD_DIM)
    dx = token_shift(xn) - xn
    xw = xn + dx * mu_w
    xa = xn + dx * mu_a
    xg = xn + dx * mu_g
    r = (p_r + (token_shift(p_r) - p_r) * mu_r).astype(f32)
    k = (p_k + (token_shift(p_k) - p_k) * mu_k).astype(f32)
    v = (p_v + (token_shift(p_v) - p_v) * mu_v).astype(f32)
    w_log = -jax.nn.softplus(-(w0 + jnp.tanh(xw @ w1) @ w2).astype(f32)) - 0.5
    decay = jnp.exp(-jnp.exp(w_log))
    a = jax.nn.sigmoid((a0 + (xa @ a1) @ a2).astype(f32))
    g = (jax.nn.sigmoid(xg @ g1) @ g2).astype(f32)
    kk = heads(k * k_k.astype(f32))
    kk = kk / jnp.maximum(jnp.linalg.norm(kk, axis=-1, keepdims=True), 1e-12)
    k = k * (1.0 + (a - 1.0) * k_a.astype(f32))
    rh, kh, vh, ah = heads(r), heads(k), heads(v), heads(a)
    y = rwkv7_scan(rh, heads(decay), kh, vh, -kk, kk * ah)
    y = head_norm(y, RWKV_GN_EPS) * lnx_w.astype(f32) + lnx_b.astype(f32)
    bonus = jnp.sum(rh * kh * r_k.astype(f32), axis=-1, keepdims=True) * vh
    y = (y + bonus.reshape(Bsz, T, RWKV_WIDTH)) * g
    return y


def retention_group(q_p, k_p, v_p, g_p, gn_w):
    Bsz, T, _ = q_p.shape
    f32 = jnp.float32
    H, D, C = RET_HEADS, RET_HEAD_DIM, RET_CHUNK
    n = T // C
    pos = jnp.arange(T)
    q = rotary(q_p.astype(f32).reshape(Bsz, T, H, D), pos)
    k = rotary(k_p.astype(f32).reshape(Bsz, T, H, D), pos) * (D ** -0.5)
    v = v_p.astype(f32).reshape(Bsz, T, H, D)
    log_gamma = jnp.log(1.0 - 2.0 ** (-5.0 - jnp.arange(H, dtype=f32)))
    qc = q.reshape(Bsz, n, C, H, D)
    kc = k.reshape(Bsz, n, C, H, D)
    vc = v.reshape(Bsz, n, C, H, D)
    idx = jnp.arange(C, dtype=f32)
    diff = idx[:, None] - idx[None, :]
    dmask = jnp.where(diff[None] >= 0,
                      jnp.exp(jnp.maximum(diff, 0.0)[None] * log_gamma[:, None, None]), 0.0)
    scores = jnp.einsum('bnihd,bnjhd->bnhij', qc, kc) * dmask
    intra = jnp.einsum('bnhij,bnjhe->bnihe', scores, vc)
    zeta = jnp.exp((C - 1.0 - idx)[None, :] * log_gamma[:, None])
    kv = jnp.einsum('bnjhd,hj,bnjhe->bnhde', kc, zeta, vc)
    gamma_c = jnp.exp(C * log_gamma)[None, :, None, None]

    def step(R, kv_i):
        return R * gamma_c + kv_i, R

    R0 = jnp.zeros((Bsz, H, D, D), f32)
    _, R_prev = lax.scan(step, R0, jnp.swapaxes(kv, 0, 1))
    R_prev = jnp.swapaxes(R_prev, 0, 1)
    xi = jnp.exp((idx + 1.0)[None, :] * log_gamma[:, None])
    inter = jnp.einsum('bnihd,bnhde,hi->bnihe', qc, R_prev, xi)
    y = (intra + inter).reshape(Bsz, T, H, D)
    y = head_norm(y, RET_GN_EPS) * gn_w.astype(f32)
    return jax.nn.silu(g_p.astype(f32)) * y


def hybrid_mixer(xn, w_in, mu_r, mu_k, mu_v, mu_w, mu_a, mu_g, w0, w1, w2,
                 a0, a1, a2, g1, g2, k_k, k_a, r_k, lnx_w, lnx_b, ret_gn_w, w_out):
    proj = xn @ w_in
    p_r, p_k, p_v, q_ret, k_ret, v_ret, g_ret = jnp.split(proj, IN_SPLITS, axis=-1)
    y_rwkv = rwkv7_group(xn, p_r, p_k, p_v, mu_r, mu_k, mu_v, mu_w, mu_a, mu_g,
                         w0, w1, w2, a0, a1, a2, g1, g2, k_k, k_a, r_k, lnx_w, lnx_b)
    y_ret = retention_group(q_ret, k_ret, v_ret, g_ret, ret_gn_w)
    y = jnp.concatenate([y_rwkv, y_ret], axis=-1).astype(xn.dtype)
    return y @ w_out


def conv_glu(xn, w_gate, w_up, conv_w, conv_b, w_down):
    gate = xn @ w_gate
    up = xn @ w_up
    gate = lax.conv_general_dilated(
        gate, conv_w, window_strides=(1,), padding=[(CONV_WIDTH - 1, 0)],
        dimension_numbers=('NWC', 'WIO', 'NWC'), feature_group_count=D_FF) + conv_b
    return (jax.nn.silu(gate) * up) @ w_down


def setup_inputs(seed: int = 0) -> dict:
    key = jax.random.key(seed)
    ks = iter(jax.random.split(key, 40))
    L, D = DEPTH, D_MODEL
    nrm = lambda shape, s: jax.random.normal(next(ks), shape, jnp.float32) * s
    uni = lambda shape, lo, hi: jax.random.uniform(next(ks), shape, jnp.float32, lo, hi)
    gain = lambda shape: 1.0 + nrm(shape, 0.02)
    return {
        "x": nrm((BATCH, SEQ, D), 1.0),
        "norm_mix_g": gain((L, D)),
        "w_in": nrm((L, D, IN_COLS), D ** -0.5),
        "rwkv_mu_r": uni((L, RWKV_WIDTH), 0.0, 1.0),
        "rwkv_mu_k": uni((L, RWKV_WIDTH), 0.0, 1.0),
        "rwkv_mu_v": uni((L, RWKV_WIDTH), 0.0, 1.0),
        "rwkv_mu_w": uni((L, D), 0.0, 1.0),
        "rwkv_mu_a": uni((L, D), 0.0, 1.0),
        "rwkv_mu_g": uni((L, D), 0.0, 1.0),
        "rwkv_w0": uni((L, RWKV_WIDTH), -6.0, -1.0),
        "rwkv_w1": nrm((L, D, DECAY_LORA), D ** -0.5),
        "rwkv_w2": nrm((L, DECAY_LORA, RWKV_WIDTH), 0.5 * DECAY_LORA ** -0.5),
        "rwkv_a0": nrm((L, RWKV_WIDTH), 0.1),
        "rwkv_a1": nrm((L, D, AAA_LORA), D ** -0.5),
        "rwkv_a2": nrm((L, AAA_LORA, RWKV_WIDTH), 0.5 * AAA_LORA ** -0.5),
        "rwkv_g1": nrm((L, D, GATE_LORA), D ** -0.5),
        "rwkv_g2": nrm((L, GATE_LORA, RWKV_WIDTH), GATE_LORA ** -0.5),
        "rwkv_k_k": 0.85 + nrm((L, RWKV_WIDTH), 0.05),
        "rwkv_k_a": 1.0 + nrm((L, RWKV_WIDTH), 0.05),
        "rwkv_r_k": nrm((L, RWKV_HEADS, RWKV_HEAD_DIM), 0.1),
        "rwkv_lnx_w": gain((L, RWKV_WIDTH)),
        "rwkv_lnx_b": nrm((L, RWKV_WIDTH), 0.02),
        "ret_gn_w": gain((L, RET_WIDTH)),
        "w_out": nrm((L, MIX_WIDTH, D), MIX_WIDTH ** -0.5),
        "norm_ffn_g": gain((L, D)),
        "ffn_w_gate": nrm((L, D, D_FF), D ** -0.5),
        "ffn_w_up": nrm((L, D, D_FF), D ** -0.5),
        "ffn_conv_w": nrm((L, CONV_WIDTH, 1, D_FF), CONV_WIDTH ** -0.5),
        "ffn_conv_b": nrm((L, D_FF), 0.02),
        "ffn_w_down": nrm((L, D_FF, D), D_FF ** -0.5),
        "norm_final_g": gain((D,)),
    }


def reference(x, norm_mix_g, w_in, rwkv_mu_r, rwkv_mu_k, rwkv_mu_v, rwkv_mu_w,
              rwkv_mu_a, rwkv_mu_g, rwkv_w0, rwkv_w1, rwkv_w2, rwkv_a0, rwkv_a1,
              rwkv_a2, rwkv_g1, rwkv_g2, rwkv_k_k, rwkv_k_a, rwkv_r_k, rwkv_lnx_w,
              rwkv_lnx_b, ret_gn_w, w_out, norm_ffn_g, ffn_w_gate, ffn_w_up,
              ffn_conv_w, ffn_conv_b, ffn_w_down, norm_final_g):
    for l in range(DEPTH):
        h = rms_norm(x, norm_mix_g[l])
        x = x + hybrid_mixer(h, w_in[l], rwkv_mu_r[l], rwkv_mu_k[l], rwkv_mu_v[l],
                             rwkv_mu_w[l], rwkv_mu_a[l], rwkv_mu_g[l], rwkv_w0[l],
                             rwkv_w1[l], rwkv_w2[l], rwkv_a0[l], rwkv_a1[l], rwkv_a2[l],
                             rwkv_g1[l], rwkv_g2[l], rwkv_k_k[l], rwkv_k_a[l], rwkv_r_k[l],
                             rwkv_lnx_w[l], rwkv_lnx_b[l], ret_gn_w[l], w_out[l])
        h = rms_norm(x, norm_ffn_g[l])
        x = x + conv_glu(h, ffn_w_gate[l], ffn_w_up[l], ffn_conv_w[l], ffn_conv_b[l],
                         ffn_w_down[l])
    return rms_norm(x, norm_final_g)
```

```python
import math

import jax
import jax.numpy as jnp
from jax import lax
from jax.experimental import pallas as pl
from jax.experimental.pallas import tpu as pltpu

F32 = jnp.float32
BF16 = jnp.bfloat16

D_MODEL = 1024
RWKV_HEADS = 8
RWKV_HEAD_DIM = 64
RWKV_WIDTH = 512
RET_HEADS = 4
RET_HEAD_DIM = 128
RET_WIDTH = 512
DECAY_LORA = 64
AAA_LORA = 64
GATE_LORA = 128
LORA_WIDTH = DECAY_LORA + AAA_LORA + GATE_LORA
RET_CHUNK = 128
ROPE_BASE = 10000.0
D_FF = 2816
NORM_EPS = 1e-6
RWKV_GN_EPS = 64e-5
RET_GN_EPS = 1e-5

V7X_VMEM_BYTES = 64 * 1024 * 1024
SUBLANES = 8

RWKV_CHUNK = 64
RWKV_GROUP_HEADS = 4
RWKV_TILE = 256
PROJ_TILE = 512
FFN_TILE = 256
FFN_COL_SPLIT = 2


def _vmem_limit(nbytes):
    return int(min(nbytes, V7X_VMEM_BYTES - 4 * 1024 * 1024))


def _dot(a, b):
    return jnp.dot(a, b, preferred_element_type=F32)


def _dot_nt(a, b):
    return lax.dot_general(a, b, (((1,), (1,)), ((), ())), preferred_element_type=F32)


def _dot_tn(a, b):
    return lax.dot_general(a, b, (((0,), (0,)), ((), ())), preferred_element_type=F32)


def _split_dot(mat_bf16, x):
    hi = x.astype(BF16)
    lo = (x - hi.astype(F32)).astype(BF16)
    return _dot(mat_bf16, hi) + _dot(mat_bf16, lo)


def _split_dot_right(x, mat_bf16):
    hi = x.astype(BF16)
    lo = (x - hi.astype(F32)).astype(BF16)
    return _dot(hi, mat_bf16) + _dot(lo, mat_bf16)


def _shift_rows(x, prev_row):
    rolled = pltpu.roll(x, 1, 0)
    row = lax.broadcasted_iota(jnp.int32, x.shape, 0)
    return jnp.where(row == 0, prev_row, rolled)


def _rms_norm(x, g):
    ms = jnp.mean(x * x, axis=-1, keepdims=True)
    return x * lax.rsqrt(ms + NORM_EPS) * g


def _fold_lora_kernel(w_ref, mu_ref, o_ref):
    w = w_ref[...]
    mu = mu_ref[...]
    o_ref[:, 0:LORA_WIDTH] = (w * (1.0 - mu)).astype(BF16)
    o_ref[:, LORA_WIDTH:2 * LORA_WIDTH] = (w * mu).astype(BF16)


def _fold_lora(w_cat, mu_cat):
    return pl.pallas_call(
        _fold_lora_kernel,
        out_shape=jax.ShapeDtypeStruct((D_MODEL, 2 * LORA_WIDTH), BF16),
        name="fold_lora",
    )(w_cat, mu_cat)


N_RKV = 3 * RWKV_WIDTH
N_RET = 4 * RET_WIDTH
N_LORA = 2 * LORA_WIDTH
N_PROJ = N_RKV + N_RET + N_LORA


def _norm_proj_kernel(x_ref, g_ref, w_ref, rkv_ref, ret_ref, lora_ref):
    xb = _rms_norm(x_ref[...], g_ref[...]).astype(BF16)
    rkv_ref[...] = _dot(xb, w_ref[:, 0:N_RKV])
    ret_ref[...] = _dot(xb, w_ref[:, N_RKV:N_RKV + N_RET])
    lora_ref[...] = _dot(xb, w_ref[:, N_RKV + N_RET:N_PROJ])


def _norm_proj(x2, g, w_ext):
    n = x2.shape[0]
    tm = PROJ_TILE
    const = lambda i: (0, 0)
    row = lambda i: (i, 0)
    vmem = 2 * tm * D_MODEL * 4 + D_MODEL * N_PROJ * 2 + 2 * tm * N_PROJ * 4 + 3 * tm * N_PROJ * 4
    return pl.pallas_call(
        _norm_proj_kernel,
        grid=(n // tm,),
        in_specs=[
            pl.BlockSpec((tm, D_MODEL), row),
            pl.BlockSpec((1, D_MODEL), const),
            pl.BlockSpec((D_MODEL, N_PROJ), const, pipeline_mode=pl.Buffered(1)),
        ],
        out_specs=[
            pl.BlockSpec((tm, N_RKV), row),
            pl.BlockSpec((tm, N_RET), row),
            pl.BlockSpec((tm, N_LORA), row),
        ],
        out_shape=[
            jax.ShapeDtypeStruct((n, N_RKV), F32),
            jax.ShapeDtypeStruct((n, N_RET), F32),
            jax.ShapeDtypeStruct((n, N_LORA), F32),
        ],
        compiler_params=pltpu.CompilerParams(
            dimension_semantics=("arbitrary",), vmem_limit_bytes=_vmem_limit(vmem)),
        name="norm_proj",
    )(x2, g, w_ext)


(_V_MU_R, _V_MU_K, _V_MU_V, _V_W0, _V_A0, _V_KK, _V_KA, _V_RK, _V_LNW, _V_LNB) = range(10)
_N_VEC_ROWS = 16


def _rwkv_kernel(prkv_ref, prkv_prev_ref, plora_ref, plora_prev_ref, vec_ref, w2_ref, seg_ref,
                 tril_ref, cones_ref, out_ref,
                 h_ref, at_ref, rt_ref, bt_ref, kt_ref, bw_ref, kw_ref, v_ref, wc_ref, y_ref):
    C, G, TT = RWKV_CHUNK, RWKV_GROUP_HEADS, RWKV_TILE
    HD = RWKV_HEAD_DIM
    W = G * HD
    GC = G * C
    n_groups = RWKV_WIDTH // W
    t_idx = pl.program_id(1)

    @pl.when(t_idx == 0)
    def _():
        h_ref[...] = jnp.zeros_like(h_ref)

    vec = lambda i: vec_ref[i:i + 1, :]
    first = t_idx == 0

    prkv = prkv_ref[...]
    srkv = _shift_rows(prkv, jnp.where(first, 0.0, prkv_prev_ref[SUBLANES - 1:SUBLANES, :]))
    lerp = lambda j, mu: (prkv[:, j * RWKV_WIDTH:(j + 1) * RWKV_WIDTH]
                          + (srkv[:, j * RWKV_WIDTH:(j + 1) * RWKV_WIDTH]
                             - prkv[:, j * RWKV_WIDTH:(j + 1) * RWKV_WIDTH]) * mu)
    r = lerp(0, vec(_V_MU_R))
    k = lerp(1, vec(_V_MU_K))
    v = lerp(2, vec(_V_MU_V))

    plora = plora_ref[...]
    slora = _shift_rows(plora, jnp.where(first, 0.0, plora_prev_ref[SUBLANES - 1:SUBLANES, :]))
    low = plora[:, 0:LORA_WIDTH] + slora[:, LORA_WIDTH:2 * LORA_WIDTH]
    lane = lax.broadcasted_iota(jnp.int32, low.shape, 1)
    act = jnp.where(lane < DECAY_LORA, jnp.tanh(low),
                    jnp.where(lane < DECAY_LORA + AAA_LORA, low, jax.nn.sigmoid(low)))
    second = _dot(act.astype(BF16), w2_ref[...])
    z = -(second[:, 0:RWKV_WIDTH] + vec(_V_W0))
    softplus = jnp.maximum(z, 0.0) + jnp.log(1.0 + jnp.exp(-jnp.abs(z)))
    ld = -jnp.exp(-softplus - 0.5)
    a = jax.nn.sigmoid(second[:, RWKV_WIDTH:2 * RWKV_WIDTH] + vec(_V_A0))
    gate = second[:, 2 * RWKV_WIDTH:3 * RWKV_WIDTH]

    seg = seg_ref[...]
    kk = k * vec(_V_KK)
    kk = kk / jnp.maximum(jnp.sqrt(_split_dot_right(kk * kk, seg)), 1e-12)
    k2 = k * (1.0 + (a - 1.0) * vec(_V_KA))
    b = kk * a
    bonus = _split_dot_right(r * k2 * vec(_V_RK), seg) * v

    cum = _split_dot(tril_ref[...], ld)
    cum_end = _split_dot(cones_ref[...], ld)
    e_neg = jnp.exp(-cum)
    e_end = jnp.exp(cum_end - cum)
    at_ref[...] = -kk * jnp.exp(cum - ld)
    rt_ref[...] = r * jnp.exp(cum)
    bt_ref[...] = b * e_neg
    kt_ref[...] = k2 * e_neg
    bw_ref[...] = b * e_end
    kw_ref[...] = k2 * e_end
    v_ref[...] = v
    wc_ref[...] = jnp.exp(cum_end)

    c_bits = C.bit_length() - 1
    hd_bits = HD.bit_length() - 1
    assert C == 1 << c_bits and HD == 1 << hd_bits

    def masks():
        ri = lax.broadcasted_iota(jnp.int32, (GC, GC), 0)
        ci = lax.broadcasted_iota(jnp.int32, (GC, GC), 1)
        same = (ri >> c_bits) == (ci >> c_bits)
        strict = same & ((ri & (C - 1)) > (ci & (C - 1)))
        incl = same & ((ri & (C - 1)) >= (ci & (C - 1)))
        eye_gc = (ri == ci).astype(F32)
        sr = lax.broadcasted_iota(jnp.int32, (GC, W), 0)
        sl = lax.broadcasted_iota(jnp.int32, (GC, W), 1)
        stack_mask = (sr >> c_bits) == (sl >> hd_bits)
        wi = lax.broadcasted_iota(jnp.int32, (W, W), 0)
        wj = lax.broadcasted_iota(jnp.int32, (W, W), 1)
        return strict, incl, eye_gc, stack_mask, wi == wj

    def stack_rows(x, stack_mask):
        return jnp.where(stack_mask, jnp.concatenate([x] * G, axis=0), 0.0)

    def unstack(x):
        out = x[0:C]
        for h in range(1, G):
            out = out + x[h * C:(h + 1) * C]
        return out

    def chunk_body(c, carry):
        rows = pl.ds(pl.multiple_of(c * C, C), C)
        strict, incl, eye_gc, stack_mask, eye_w = masks()
        stack = lambda x: stack_rows(x, stack_mask)
        for g in range(n_groups):
            lanes = slice(g * W, (g + 1) * W)
            a_s = stack(at_ref[rows, lanes]).astype(BF16)
            r_s32 = stack(rt_ref[rows, lanes])
            r_s = r_s32.astype(BF16)
            b_s = stack(bt_ref[rows, lanes]).astype(BF16)
            k_s = stack(kt_ref[rows, lanes]).astype(BF16)
            bw_s = stack(bw_ref[rows, lanes]).astype(BF16)
            kw_s = stack(kw_ref[rows, lanes]).astype(BF16)
            v_s = stack(v_ref[rows, lanes]).astype(BF16)
            w_end = wc_ref[rows, lanes][0:1, :]

            n_mat = jnp.where(strict, _dot_nt(a_s, b_s), 0.0)
            a_ak = jnp.where(strict, _dot_nt(a_s, k_s), 0.0).astype(BF16)
            a_rb = jnp.where(incl, _dot_nt(r_s, b_s), 0.0).astype(BF16)
            a_rk = jnp.where(incl, _dot_nt(r_s, k_s), 0.0).astype(BF16)

            t_mat = eye_gc + n_mat
            p = n_mat
            for _ in range(5):
                pb = p.astype(BF16)
                p = _dot(pb, pb)
                t_mat = t_mat + _dot(t_mat.astype(BF16), p.astype(BF16))
            t_b = t_mat.astype(BF16)

            x1 = _dot(t_b, a_s).astype(BF16)
            x2 = _dot(t_b, _dot(a_ak, v_s).astype(BF16)).astype(BF16)
            q_s = r_s32 + _dot(a_rb, x1)
            y0_s = _dot(a_rb, x2) + _dot(a_rk, v_s)
            m_mat = _dot_tn(bw_s, x1) + jnp.where(eye_w, w_end, 0.0)
            g_mat = _dot_tn(bw_s, x2) + _dot_tn(kw_s, v_s)

            h_prev = h_ref[g]
            h_b = h_prev.astype(BF16)
            y_ref[rows, lanes] = _dot(unstack(q_s).astype(BF16), h_b) + unstack(y0_s)
            h_ref[g] = _dot(m_mat.astype(BF16), h_b) + g_mat
        return carry

    lax.fori_loop(0, TT // C, chunk_body, 0)

    y = y_ref[...]
    inv_hd = 1.0 / HD
    mu = _split_dot_right(y, seg) * inv_hd
    d = y - mu
    var = _split_dot_right(d * d, seg) * inv_hd
    yn = d * lax.rsqrt(var + RWKV_GN_EPS) * vec(_V_LNW) + vec(_V_LNB)
    out_ref[...] = ((yn + bonus) * gate).astype(BF16)


def _rwkv(p_rkv, p_lora, vecs, w2cat, seg, tril, cones, bsz, seq):
    tt = RWKV_TILE
    nt = seq // tt
    w = RWKV_GROUP_HEADS * RWKV_HEAD_DIM
    n_groups = RWKV_WIDTH // w
    row = lambda b, t: (b * nt + t, 0)
    prev = lambda b, t: (jnp.maximum((b * nt + t) * (tt // SUBLANES) - 1, 0), 0)
    const = lambda b, t: (0, 0)
    tile = pltpu.VMEM((tt, RWKV_WIDTH), F32)
    return pl.pallas_call(
        _rwkv_kernel,
        grid=(bsz, nt),
        in_specs=[
            pl.BlockSpec((tt, N_RKV), row),
            pl.BlockSpec((SUBLANES, N_RKV), prev),
            pl.BlockSpec((tt, N_LORA), row),
            pl.BlockSpec((SUBLANES, N_LORA), prev),
            pl.BlockSpec((_N_VEC_ROWS, RWKV_WIDTH), const),
            pl.BlockSpec((LORA_WIDTH, 3 * RWKV_WIDTH), const),
            pl.BlockSpec((RWKV_WIDTH, RWKV_WIDTH), const),
            pl.BlockSpec((tt, tt), const),
            pl.BlockSpec((tt, tt), const),
        ],
        out_specs=pl.BlockSpec((tt, RWKV_WIDTH), row),
        out_shape=jax.ShapeDtypeStruct((bsz * seq, RWKV_WIDTH), BF16),
        scratch_shapes=[pltpu.VMEM((n_groups, w, w), F32)] + [tile] * 9,
        compiler_params=pltpu.CompilerParams(
            dimension_semantics=("arbitrary", "arbitrary"), vmem_limit_bytes=_vmem_limit(48 * 1024 * 1024)),
        name="rwkv",
    )(p_rkv, p_rkv, p_lora, p_lora, vecs, w2cat, seg, tril, cones)


def _retention_kernel(p_ref, freq_ref, gnw_ref, out_ref, state_ref):
    C, HD = RET_CHUNK, RET_HEAD_DIM
    half = HD // 2
    c_idx = pl.program_id(1)

    @pl.when(c_idx == 0)
    def _():
        state_ref[...] = jnp.zeros_like(state_ref)

    pos = (lax.broadcasted_iota(jnp.int32, (C, HD), 0) + c_idx * C).astype(F32)
    ang = pos * freq_ref[...]
    lane = lax.broadcasted_iota(jnp.int32, (C, HD), 1)
    cos = jnp.cos(ang)
    sin = jnp.where(lane < half, -jnp.sin(ang), jnp.sin(ang))

    def rot(xh):
        return xh * cos + pltpu.roll(xh, half, 1) * sin

    ii = lax.broadcasted_iota(jnp.int32, (C, C), 0)
    jj = lax.broadcasted_iota(jnp.int32, (C, C), 1)
    diff = (ii - jj).astype(F32)
    causal = ii >= jj
    row_i = lax.broadcasted_iota(jnp.int32, (C, 1), 0).astype(F32)

    for h in range(RET_HEADS):
        lg = _LOG_GAMMA[h]
        sl = lambda j: slice(j * RET_WIDTH + h * HD, j * RET_WIDTH + (h + 1) * HD)
        q = rot(p_ref[:, sl(0)])
        k = rot(p_ref[:, sl(1)]) * (HD ** -0.5)
        v = p_ref[:, sl(2)]
        gt = p_ref[:, sl(3)]
        dmask = jnp.where(causal, jnp.exp(jnp.maximum(diff, 0.0) * lg), 0.0)
        qb, kb, vb = q.astype(BF16), k.astype(BF16), v.astype(BF16)
        scores = _dot_nt(qb, kb) * dmask
        intra = _dot(scores.astype(BF16), vb)
        zeta = jnp.exp((C - 1.0 - row_i) * lg)
        xi = jnp.exp((row_i + 1.0) * lg)
        kv = _dot_tn(kb, (v * zeta).astype(BF16))
        r_prev = state_ref[h]
        inter = _dot((q * xi).astype(BF16), r_prev.astype(BF16))
        state_ref[h] = r_prev * _GAMMA_C[h] + kv
        y = intra + inter
        mu = jnp.mean(y, axis=-1, keepdims=True)
        d = y - mu
        var = jnp.mean(d * d, axis=-1, keepdims=True)
        yn = d * lax.rsqrt(var + RET_GN_EPS) * gnw_ref[:, h * HD:(h + 1) * HD]
        out_ref[:, h * HD:(h + 1) * HD] = (gt * jax.nn.sigmoid(gt) * yn).astype(BF16)


_LOG_GAMMA = [math.log(1.0 - 2.0 ** (-5.0 - h)) for h in range(RET_HEADS)]
_GAMMA_C = [math.exp(RET_CHUNK * lg) for lg in _LOG_GAMMA]


def _retention(p_ret, freq, gn_w, bsz, seq):
    c = RET_CHUNK
    nc = seq // c
    row = lambda b, i: (b * nc + i, 0)
    const = lambda b, i: (0, 0)
    return pl.pallas_call(
        _retention_kernel,
        grid=(bsz, nc),
        in_specs=[
            pl.BlockSpec((c, N_RET), row),
            pl.BlockSpec((1, RET_HEAD_DIM), const),
            pl.BlockSpec((1, RET_WIDTH), const),
        ],
        out_specs=pl.BlockSpec((c, RET_WIDTH), row),
        out_shape=jax.ShapeDtypeStruct((bsz * seq, RET_WIDTH), BF16),
        scratch_shapes=[pltpu.VMEM((RET_HEADS, RET_HEAD_DIM, RET_HEAD_DIM), F32)],
        compiler_params=pltpu.CompilerParams(dimension_semantics=("arbitrary", "arbitrary")),
        name="retention",
    )(p_ret, freq, gn_w)


def _ffn_kernel(yr_ref, yt_ref, x_ref, wo_ref, gf_ref, wg_ref, wu_ref, cw_ref, cb_ref, wd_ref, gl_ref,
                out_ref, carry_ref):
    tm = FFN_TILE
    t_idx = pl.program_id(1)

    @pl.when(t_idx == 0)
    def _():
        carry_ref[...] = jnp.zeros_like(carry_ref)

    mix = _dot(yr_ref[...], wo_ref[0:RWKV_WIDTH, :]) + _dot(yt_ref[...], wo_ref[RWKV_WIDTH:, :])
    x1 = x_ref[...] + mix
    hb = _rms_norm(x1, gf_ref[...]).astype(BF16)

    cols = D_FF // FFN_COL_SPLIT
    acc = x1
    for j in range(FFN_COL_SPLIT):
        cs = slice(j * cols, (j + 1) * cols)
        gate = _dot(hb, wg_ref[:, cs])
        up = _dot(hb, wu_ref[:, cs])
        prev = carry_ref[:, cs]
        row = lax.broadcasted_iota(jnp.int32, gate.shape, 0)
        g1 = jnp.where(row == 0, prev[SUBLANES - 1:SUBLANES], pltpu.roll(gate, 1, 0))
        g2 = jnp.where(row == 0, prev[SUBLANES - 2:SUBLANES - 1],
                       jnp.where(row == 1, prev[SUBLANES - 1:SUBLANES], pltpu.roll(gate, 2, 0)))
        carry_ref[:, cs] = gate[tm - SUBLANES:tm]
        conv = cw_ref[0:1, cs] * g2 + cw_ref[1:2, cs] * g1 + cw_ref[2:3, cs] * gate + cb_ref[:, cs]
        hidden = (conv * jax.nn.sigmoid(conv) * up).astype(BF16)
        acc = acc + _dot(hidden, wd_ref[cs, :])
    out_ref[...] = _rms_norm(acc, gl_ref[...])


def _ffn(y_rwkv, y_ret, x2, w_out, g_ffn, w_gate, w_up, conv_w, conv_b, w_down, g_final, bsz, seq):
    tm = FFN_TILE
    nt = seq // tm
    row = lambda b, t: (b * nt + t, 0)
    const = lambda b, t: (0, 0)
    single = dict(pipeline_mode=pl.Buffered(1))
    weights = (D_MODEL * D_MODEL + 3 * D_MODEL * D_FF) * 2
    vmem = weights + 4 * tm * D_MODEL * 4 + 8 * tm * D_FF * 4 + 8 * 1024 * 1024
    return pl.pallas_call(
        _ffn_kernel,
        grid=(bsz, nt),
        in_specs=[
            pl.BlockSpec((tm, RWKV_WIDTH), row),
            pl.BlockSpec((tm, RET_WIDTH), row),
            pl.BlockSpec((tm, D_MODEL), row),
            pl.BlockSpec((D_MODEL, D_MODEL), const, **single),
            pl.BlockSpec((1, D_MODEL), const),
            pl.BlockSpec((D_MODEL, D_FF), const, **single),
            pl.BlockSpec((D_MODEL, D_FF), const, **single),
            pl.BlockSpec((SUBLANES, D_FF), const),
            pl.BlockSpec((1, D_FF), const),
            pl.BlockSpec((D_FF, D_MODEL), const, **single),
            pl.BlockSpec((1, D_MODEL), const),
        ],
        out_specs=pl.BlockSpec((tm, D_MODEL), row),
        out_shape=jax.ShapeDtypeStruct((bsz * seq, D_MODEL), F32),
        scratch_shapes=[pltpu.VMEM((SUBLANES, D_FF), F32)],
        compiler_params=pltpu.CompilerParams(
            dimension_semantics=("arbitrary", "arbitrary"), vmem_limit_bytes=_vmem_limit(vmem)),
        name="ffn",
    )(y_rwkv, y_ret, x2, w_out, g_ffn, w_gate, w_up, conv_w, conv_b, w_down, g_final)


def _block_ones(n, block):
    i = jnp.arange(n)
    return ((i[:, None] // block) == (i[None, :] // block)).astype(BF16)


def kernel(x, norm_mix_g, w_in, rwkv_mu_r, rwkv_mu_k, rwkv_mu_v, rwkv_mu_w, rwkv_mu_a, rwkv_mu_g, rwkv_w0, rwkv_w1, rwkv_w2, rwkv_a0, rwkv_a1, rwkv_a2, rwkv_g1, rwkv_g2, rwkv_k_k, rwkv_k_a, rwkv_r_k, rwkv_lnx_w, rwkv_lnx_b, ret_gn_w, w_out, norm_ffn_g, ffn_w_gate, ffn_w_up, ffn_conv_w, ffn_conv_b, ffn_w_down, norm_final_g):
    bsz, seq, d = x.shape
    assert d == D_MODEL and seq % PROJ_TILE == 0 and seq % RWKV_TILE == 0 and seq % FFN_TILE == 0
    assert norm_mix_g.shape[0] == 1, "one layer"
    x2 = x.reshape(bsz * seq, d)
    row = lambda p: p.reshape(1, -1)

    w_cat = jnp.concatenate([rwkv_w1[0], rwkv_a1[0], rwkv_g1[0]], axis=1)
    mu_cat = jnp.concatenate([
        jnp.broadcast_to(rwkv_mu_w[0][:, None], (d, DECAY_LORA)),
        jnp.broadcast_to(rwkv_mu_a[0][:, None], (d, AAA_LORA)),
        jnp.broadcast_to(rwkv_mu_g[0][:, None], (d, GATE_LORA))], axis=1)
    w_ext = jnp.concatenate([w_in[0].astype(BF16), _fold_lora(w_cat, mu_cat)], axis=1)
    p_rkv, p_ret, p_lora = _norm_proj(x2, row(norm_mix_g[0]), w_ext)

    vec_rows = [rwkv_mu_r[0], rwkv_mu_k[0], rwkv_mu_v[0], rwkv_w0[0], rwkv_a0[0], rwkv_k_k[0], rwkv_k_a[0],
                rwkv_r_k[0].reshape(-1), rwkv_lnx_w[0], rwkv_lnx_b[0]]
    vecs = jnp.zeros((_N_VEC_ROWS, RWKV_WIDTH), F32).at[:len(vec_rows)].set(jnp.stack(vec_rows))
    w2cat = jnp.zeros((LORA_WIDTH, 3 * RWKV_WIDTH), BF16)
    w2cat = w2cat.at[0:DECAY_LORA, 0:RWKV_WIDTH].set(rwkv_w2[0].astype(BF16))
    w2cat = w2cat.at[DECAY_LORA:DECAY_LORA + AAA_LORA, RWKV_WIDTH:2 * RWKV_WIDTH].set(rwkv_a2[0].astype(BF16))
    w2cat = w2cat.at[DECAY_LORA + AAA_LORA:, 2 * RWKV_WIDTH:].set(rwkv_g2[0].astype(BF16))
    seg = _block_ones(RWKV_WIDTH, RWKV_HEAD_DIM)
    ti = jnp.arange(RWKV_TILE)
    cones = _block_ones(RWKV_TILE, RWKV_CHUNK)
    tril = (cones.astype(F32) * (ti[:, None] >= ti[None, :])).astype(BF16)
    y_rwkv = _rwkv(p_rkv, p_lora, vecs, w2cat, seg, tril, cones, bsz, seq)

    half = RET_HEAD_DIM // 2
    inv_freq = ROPE_BASE ** (-jnp.arange(half, dtype=F32) / half)
    freq = jnp.concatenate([inv_freq, inv_freq]).reshape(1, RET_HEAD_DIM)
    y_ret = _retention(p_ret, freq, row(ret_gn_w[0]), bsz, seq)

    conv_w = jnp.zeros((SUBLANES, D_FF), F32).at[0:3].set(ffn_conv_w[0][:, 0, :])
    out = _ffn(y_rwkv, y_ret, x2, w_out[0].astype(BF16), row(norm_ffn_g[0]), ffn_w_gate[0].astype(BF16),
               ffn_w_up[0].astype(BF16), conv_w, row(ffn_conv_b[0]), ffn_w_down[0].astype(BF16),
               row(norm_final_g), bsz, seq)
    return out.reshape(bsz, seq, d)
```

```python
import math

import jax
import jax.numpy as jnp
from jax import lax
from jax.experimental import pallas as pl
from jax.experimental.pallas import tpu as pltpu

F32 = jnp.float32
BF16 = jnp.bfloat16

D_MODEL = 1024
RWKV_HEADS = 8
RWKV_HEAD_DIM = 64
RWKV_WIDTH = 512
RET_HEADS = 4
RET_HEAD_DIM = 128
RET_WIDTH = 512
DECAY_LORA = 64
AAA_LORA = 64
GATE_LORA = 128
LORA_WIDTH = DECAY_LORA + AAA_LORA + GATE_LORA
RET_CHUNK = 128
ROPE_BASE = 10000.0
D_FF = 2816
NORM_EPS = 1e-6
RWKV_GN_EPS = 64e-5
RET_GN_EPS = 1e-5

V7X_VMEM_BYTES = 64 * 1024 * 1024
SUBLANES = 8

RWKV_CHUNK = 64
RWKV_GROUP_HEADS = 2
RWKV_TILE = 256
PROJ_TILE = 512
FFN_TILE = 256
FFN_COL_SPLIT = 2


def _vmem_limit(nbytes):
    return int(min(nbytes, V7X_VMEM_BYTES - 4 * 1024 * 1024))


def _dot(a, b):
    return jnp.dot(a, b, preferred_element_type=F32)


def _dot_nt(a, b):
    return lax.dot_general(a, b, (((1,), (1,)), ((), ())), preferred_element_type=F32)


def _dot_tn(a, b):
    return lax.dot_general(a, b, (((0,), (0,)), ((), ())), preferred_element_type=F32)


def _split_dot(mat_bf16, x):
    hi = x.astype(BF16)
    lo = (x - hi.astype(F32)).astype(BF16)
    return _dot(mat_bf16, hi) + _dot(mat_bf16, lo)


def _split_dot_right(x, mat_bf16):
    hi = x.astype(BF16)
    lo = (x - hi.astype(F32)).astype(BF16)
    return _dot(hi, mat_bf16) + _dot(lo, mat_bf16)


def _shift_rows(x, prev_row):
    rolled = pltpu.roll(x, 1, 0)
    row = lax.broadcasted_iota(jnp.int32, x.shape, 0)
    return jnp.where(row == 0, prev_row, rolled)


def _rms_norm(x, g):
    ms = jnp.mean(x * x, axis=-1, keepdims=True)
    return x * lax.rsqrt(ms + NORM_EPS) * g


def _fold_lora_kernel(w_ref, mu_ref, o_ref):
    w = w_ref[...]
    mu = mu_ref[...]
    o_ref[:, 0:LORA_WIDTH] = (w * (1.0 - mu)).astype(BF16)
    o_ref[:, LORA_WIDTH:2 * LORA_WIDTH] = (w * mu).astype(BF16)


def _fold_lora(w_cat, mu_cat):
    return pl.pallas_call(
        _fold_lora_kernel,
        out_shape=jax.ShapeDtypeStruct((D_MODEL, 2 * LORA_WIDTH), BF16),
        name="fold_lora",
    )(w_cat, mu_cat)


N_RKV = 3 * RWKV_WIDTH
N_RET = 4 * RET_WIDTH
N_LORA = 2 * LORA_WIDTH
N_PROJ = N_RKV + N_RET + N_LORA


def _norm_proj_kernel(x_ref, g_ref, w_ref, rkv_ref, ret_ref, lora_ref):
    xb = _rms_norm(x_ref[...], g_ref[...]).astype(BF16)
    rkv_ref[...] = _dot(xb, w_ref[:, 0:N_RKV])
    ret_ref[...] = _dot(xb, w_ref[:, N_RKV:N_RKV + N_RET])
    lora_ref[...] = _dot(xb, w_ref[:, N_RKV + N_RET:N_PROJ])


def _norm_proj(x2, g, w_ext):
    n = x2.shape[0]
    tm = PROJ_TILE
    const = lambda i: (0, 0)
    row = lambda i: (i, 0)
    vmem = 2 * tm * D_MODEL * 4 + D_MODEL * N_PROJ * 2 + 2 * tm * N_PROJ * 4 + 3 * tm * N_PROJ * 4
    return pl.pallas_call(
        _norm_proj_kernel,
        grid=(n // tm,),
        in_specs=[
            pl.BlockSpec((tm, D_MODEL), row),
            pl.BlockSpec((1, D_MODEL), const),
            pl.BlockSpec((D_MODEL, N_PROJ), const, pipeline_mode=pl.Buffered(1)),
        ],
        out_specs=[
            pl.BlockSpec((tm, N_RKV), row),
            pl.BlockSpec((tm, N_RET), row),
            pl.BlockSpec((tm, N_LORA), row),
        ],
        out_shape=[
            jax.ShapeDtypeStruct((n, N_RKV), F32),
            jax.ShapeDtypeStruct((n, N_RET), F32),
            jax.ShapeDtypeStruct((n, N_LORA), F32),
        ],
        compiler_params=pltpu.CompilerParams(
            dimension_semantics=("arbitrary",), vmem_limit_bytes=_vmem_limit(vmem)),
        name="norm_proj",
    )(x2, g, w_ext)


(_V_MU_R, _V_MU_K, _V_MU_V, _V_W0, _V_A0, _V_KK, _V_KA, _V_RK, _V_LNW, _V_LNB) = range(10)
_N_VEC_ROWS = 16


def _rwkv_kernel(prkv_ref, prkv_prev_ref, plora_ref, plora_prev_ref, vec_ref, w2_ref, seg_ref,
                 tril_ref, cones_ref, out_ref, h_ref):
    C, G, TT = RWKV_CHUNK, RWKV_GROUP_HEADS, RWKV_TILE
    HD = RWKV_HEAD_DIM
    W = G * HD
    GC = G * C
    n_groups = RWKV_WIDTH // W
    t_idx = pl.program_id(1)

    @pl.when(t_idx == 0)
    def _():
        h_ref[...] = jnp.zeros_like(h_ref)

    vec = lambda i: vec_ref[i:i + 1, :]
    first = t_idx == 0

    prkv = prkv_ref[...]
    srkv = _shift_rows(prkv, jnp.where(first, 0.0, prkv_prev_ref[SUBLANES - 1:SUBLANES, :]))
    lerp = lambda j, mu: (prkv[:, j * RWKV_WIDTH:(j + 1) * RWKV_WIDTH]
                          + (srkv[:, j * RWKV_WIDTH:(j + 1) * RWKV_WIDTH]
                             - prkv[:, j * RWKV_WIDTH:(j + 1) * RWKV_WIDTH]) * mu)
    r = lerp(0, vec(_V_MU_R))
    k = lerp(1, vec(_V_MU_K))
    v = lerp(2, vec(_V_MU_V))

    plora = plora_ref[...]
    slora = _shift_rows(plora, jnp.where(first, 0.0, plora_prev_ref[SUBLANES - 1:SUBLANES, :]))
    low = plora[:, 0:LORA_WIDTH] + slora[:, LORA_WIDTH:2 * LORA_WIDTH]
    lane = lax.broadcasted_iota(jnp.int32, low.shape, 1)
    act = jnp.where(lane < DECAY_LORA, jnp.tanh(low),
                    jnp.where(lane < DECAY_LORA + AAA_LORA, low, jax.nn.sigmoid(low)))
    second = _dot(act.astype(BF16), w2_ref[...])
    z = -(second[:, 0:RWKV_WIDTH] + vec(_V_W0))
    softplus = jnp.maximum(z, 0.0) + jnp.log(1.0 + jnp.exp(-jnp.abs(z)))
    ld = -jnp.exp(-softplus - 0.5)
    a = jax.nn.sigmoid(second[:, RWKV_WIDTH:2 * RWKV_WIDTH] + vec(_V_A0))
    gate = second[:, 2 * RWKV_WIDTH:3 * RWKV_WIDTH]

    seg = seg_ref[...]
    kk = k * vec(_V_KK)
    kk = kk / jnp.maximum(jnp.sqrt(_split_dot_right(kk * kk, seg)), 1e-12)
    k2 = k * (1.0 + (a - 1.0) * vec(_V_KA))
    b = kk * a
    bonus = _split_dot_right(r * k2 * vec(_V_RK), seg) * v

    cum = _split_dot(tril_ref[...], ld)
    cum_end = _split_dot(cones_ref[...], ld)
    e_neg = jnp.exp(-cum)
    e_end = jnp.exp(cum_end - cum)
    a_t = -kk * jnp.exp(cum - ld)
    r_t = r * jnp.exp(cum)
    b_t = b * e_neg
    k_t = k2 * e_neg
    b_w = b * e_end
    k_w = k2 * e_end
    w_c = jnp.exp(cum_end)

    c_bits = C.bit_length() - 1
    hd_bits = HD.bit_length() - 1
    assert C == 1 << c_bits and HD == 1 << hd_bits
    ri = lax.broadcasted_iota(jnp.int32, (GC, GC), 0)
    ci = lax.broadcasted_iota(jnp.int32, (GC, GC), 1)
    same = (ri >> c_bits) == (ci >> c_bits)
    strict = same & ((ri & (C - 1)) > (ci & (C - 1)))
    incl = same & ((ri & (C - 1)) >= (ci & (C - 1)))
    eye_gc = (ri == ci).astype(F32)
    sr = lax.broadcasted_iota(jnp.int32, (GC, W), 0)
    sl = lax.broadcasted_iota(jnp.int32, (GC, W), 1)
    stack_mask = (sr >> c_bits) == (sl >> hd_bits)
    wi = lax.broadcasted_iota(jnp.int32, (W, W), 0)
    wj = lax.broadcasted_iota(jnp.int32, (W, W), 1)
    eye_w = wi == wj

    def stack(x):
        return jnp.where(stack_mask, jnp.concatenate([x] * G, axis=0), 0.0)

    def unstack(x):
        out = x[0:C]
        for h in range(1, G):
            out = out + x[h * C:(h + 1) * C]
        return out

    rows_of = lambda z, n: [z[i * GC:(i + 1) * GC] for i in range(n)]
    chains = [(c, g) for c in range(TT // C) for g in range(n_groups)]
    blk = lambda z, c, g: z[c * C:(c + 1) * C, g * W:(g + 1) * W]
    st = []
    for c, g in chains:
        r_s32 = stack(blk(r_t, c, g))
        st.append(dict(
            a=stack(blk(a_t, c, g)).astype(BF16), r32=r_s32, r=r_s32.astype(BF16),
            b=stack(blk(b_t, c, g)).astype(BF16), k=stack(blk(k_t, c, g)).astype(BF16),
            bw_t=stack(blk(b_w, c, g)).T.astype(BF16), kw_t=stack(blk(k_w, c, g)).T.astype(BF16),
            v=stack(blk(v, c, g)).astype(BF16), w_end=blk(w_c, c, g)[0:1, :]))
    for s in st:
        prod = _dot_nt(jnp.concatenate([s["a"], s["r"]], axis=0), jnp.concatenate([s["b"], s["k"]], axis=0))
        s["n"] = jnp.where(strict, prod[0:GC, 0:GC], 0.0)
        s["a_ak"] = jnp.where(strict, prod[0:GC, GC:], 0.0).astype(BF16)
        s["a_rb"] = jnp.where(incl, prod[GC:, 0:GC], 0.0).astype(BF16)
        s["a_rk"] = jnp.where(incl, prod[GC:, GC:], 0.0).astype(BF16)
    for s in st:
        s["av"], s["rkv"], s["kwv"] = rows_of(
            _dot(jnp.concatenate([s["a_ak"], s["a_rk"], s["kw_t"]], axis=0), s["v"]), 3)
    for s in st:
        nb = s["n"].astype(BF16)
        s["t"] = eye_gc + s["n"]
        s["p"] = _dot(nb, nb).astype(BF16)
    for i in range(1, 6):
        for s in st:
            if i < 5:
                t_p, p_p = rows_of(_dot(jnp.concatenate([s["t"].astype(BF16), s["p"]], axis=0), s["p"]), 2)
                s["t"] = s["t"] + t_p
                s["p"] = p_p.astype(BF16)
            else:
                s["t"] = (s["t"] + _dot(s["t"].astype(BF16), s["p"])).astype(BF16)
    for s in st:
        s["x1"] = _dot(s["t"], s["a"]).astype(BF16)
        s["x2"] = _dot(s["t"], s["av"].astype(BF16)).astype(BF16)
    for s in st:
        lhs = jnp.concatenate([s["a_rb"], s["bw_t"]], axis=0)
        o1 = _dot(lhs, s["x1"])
        o2 = _dot(lhs, s["x2"])
        s["q"] = unstack(s["r32"] + o1[0:GC]).astype(BF16)
        s["m"] = (o1[GC:] + jnp.where(eye_w, s["w_end"], 0.0)).astype(BF16)
        s["y0"] = unstack(o2[0:GC] + s["rkv"])
        s["g"] = o2[GC:] + s["kwv"]
    h_cur = [h_ref[g] for g in range(n_groups)]
    y_rows = []
    for c in range(TT // C):
        y_lanes = []
        for g in range(n_groups):
            s = st[c * n_groups + g]
            res = _dot(jnp.concatenate([s["q"], s["m"]], axis=0), h_cur[g].astype(BF16))
            y_lanes.append(res[0:C] + s["y0"])
            h_cur[g] = res[C:] + s["g"]
        y_rows.append(jnp.concatenate(y_lanes, axis=1))
    for g in range(n_groups):
        h_ref[g] = h_cur[g]

    y = jnp.concatenate(y_rows, axis=0)
    inv_hd = 1.0 / HD
    mu = _split_dot_right(y, seg) * inv_hd
    d = y - mu
    var = _split_dot_right(d * d, seg) * inv_hd
    yn = d * lax.rsqrt(var + RWKV_GN_EPS) * vec(_V_LNW) + vec(_V_LNB)
    out_ref[...] = ((yn + bonus) * gate).astype(BF16)


def _rwkv(p_rkv, p_lora, vecs, w2cat, seg, tril, cones, bsz, seq):
    tt = RWKV_TILE
    nt = seq // tt
    w = RWKV_GROUP_HEADS * RWKV_HEAD_DIM
    n_groups = RWKV_WIDTH // w
    row = lambda b, t: (b * nt + t, 0)
    prev = lambda b, t: (jnp.maximum((b * nt + t) * (tt // SUBLANES) - 1, 0), 0)
    const = lambda b, t: (0, 0)
    return pl.pallas_call(
        _rwkv_kernel,
        grid=(bsz, nt),
        in_specs=[
            pl.BlockSpec((tt, N_RKV), row),
            pl.BlockSpec((SUBLANES, N_RKV), prev),
            pl.BlockSpec((tt, N_LORA), row),
            pl.BlockSpec((SUBLANES, N_LORA), prev),
            pl.BlockSpec((_N_VEC_ROWS, RWKV_WIDTH), const),
            pl.BlockSpec((LORA_WIDTH, 3 * RWKV_WIDTH), const),
            pl.BlockSpec((RWKV_WIDTH, RWKV_WIDTH), const),
            pl.BlockSpec((tt, tt), const),
            pl.BlockSpec((tt, tt), const),
        ],
        out_specs=pl.BlockSpec((tt, RWKV_WIDTH), row),
        out_shape=jax.ShapeDtypeStruct((bsz * seq, RWKV_WIDTH), BF16),
        scratch_shapes=[pltpu.VMEM((n_groups, w, w), F32)],
        compiler_params=pltpu.CompilerParams(
            dimension_semantics=("arbitrary", "arbitrary"), vmem_limit_bytes=_vmem_limit(48 * 1024 * 1024)),
        name="rwkv",
    )(p_rkv, p_rkv, p_lora, p_lora, vecs, w2cat, seg, tril, cones)


def _retention_kernel(p_ref, freq_ref, gnw_ref, out_ref, state_ref):
    C, HD = RET_CHUNK, RET_HEAD_DIM
    half = HD // 2
    c_idx = pl.program_id(1)

    @pl.when(c_idx == 0)
    def _():
        state_ref[...] = jnp.zeros_like(state_ref)

    pos = (lax.broadcasted_iota(jnp.int32, (C, HD), 0) + c_idx * C).astype(F32)
    ang = pos * freq_ref[...]
    lane = lax.broadcasted_iota(jnp.int32, (C, HD), 1)
    cos = jnp.cos(ang)
    sin = jnp.where(lane < half, -jnp.sin(ang), jnp.sin(ang))

    def rot(xh):
        return xh * cos + pltpu.roll(xh, half, 1) * sin

    ii = lax.broadcasted_iota(jnp.int32, (C, C), 0)
    jj = lax.broadcasted_iota(jnp.int32, (C, C), 1)
    diff = (ii - jj).astype(F32)
    causal = ii >= jj
    row_i = lax.broadcasted_iota(jnp.int32, (C, 1), 0).astype(F32)

    for h in range(RET_HEADS):
        lg = _LOG_GAMMA[h]
        sl = lambda j: slice(j * RET_WIDTH + h * HD, j * RET_WIDTH + (h + 1) * HD)
        q = rot(p_ref[:, sl(0)])
        k = rot(p_ref[:, sl(1)]) * (HD ** -0.5)
        v = p_ref[:, sl(2)]
        gt = p_ref[:, sl(3)]
        dmask = jnp.where(causal, jnp.exp(jnp.maximum(diff, 0.0) * lg), 0.0)
        qb, kb, vb = q.astype(BF16), k.astype(BF16), v.astype(BF16)
        scores = _dot_nt(qb, kb) * dmask
        intra = _dot(scores.astype(BF16), vb)
        zeta = jnp.exp((C - 1.0 - row_i) * lg)
        xi = jnp.exp((row_i + 1.0) * lg)
        kv = _dot_tn(kb, (v * zeta).astype(BF16))
        r_prev = state_ref[h]
        inter = _dot((q * xi).astype(BF16), r_prev.astype(BF16))
        state_ref[h] = r_prev * _GAMMA_C[h] + kv
        y = intra + inter
        mu = jnp.mean(y, axis=-1, keepdims=True)
        d = y - mu
        var = jnp.mean(d * d, axis=-1, keepdims=True)
        yn = d * lax.rsqrt(var + RET_GN_EPS) * gnw_ref[:, h * HD:(h + 1) * HD]
        out_ref[:, h * HD:(h + 1) * HD] = (gt * jax.nn.sigmoid(gt) * yn).astype(BF16)


_LOG_GAMMA = [math.log(1.0 - 2.0 ** (-5.0 - h)) for h in range(RET_HEADS)]
_GAMMA_C = [math.exp(RET_CHUNK * lg) for lg in _LOG_GAMMA]


def _retention(p_ret, freq, gn_w, bsz, seq):
    c = RET_CHUNK
    nc = seq // c
    row = lambda b, i: (b * nc + i, 0)
    const = lambda b, i: (0, 0)
    return pl.pallas_call(
        _retention_kernel,
        grid=(bsz, nc),
        in_specs=[
            pl.BlockSpec((c, N_RET), row),
            pl.BlockSpec((1, RET_HEAD_DIM), const),
            pl.BlockSpec((1, RET_WIDTH), const),
        ],
        out_specs=pl.BlockSpec((c, RET_WIDTH), row),
        out_shape=jax.ShapeDtypeStruct((bsz * seq, RET_WIDTH), BF16),
        scratch_shapes=[pltpu.VMEM((RET_HEADS, RET_HEAD_DIM, RET_HEAD_DIM), F32)],
        compiler_params=pltpu.CompilerParams(dimension_semantics=("arbitrary", "arbitrary")),
        name="retention",
    )(p_ret, freq, gn_w)


def _ffn_kernel(yr_ref, yt_ref, x_ref, wo_ref, gf_ref, wg_ref, wu_ref, cw_ref, cb_ref, wd_ref, gl_ref,
                out_ref, carry_ref):
    tm = FFN_TILE
    t_idx = pl.program_id(1)

    @pl.when(t_idx == 0)
    def _():
        carry_ref[...] = jnp.zeros_like(carry_ref)

    mix = _dot(yr_ref[...], wo_ref[0:RWKV_WIDTH, :]) + _dot(yt_ref[...], wo_ref[RWKV_WIDTH:, :])
    x1 = x_ref[...] + mix
    hb = _rms_norm(x1, gf_ref[...]).astype(BF16)

    cols = D_FF // FFN_COL_SPLIT
    acc = x1
    for j in range(FFN_COL_SPLIT):
        cs = slice(j * cols, (j + 1) * cols)
        gate = _dot(hb, wg_ref[:, cs])
        up = _dot(hb, wu_ref[:, cs])
        prev = carry_ref[:, cs]
        row = lax.broadcasted_iota(jnp.int32, gate.shape, 0)
        g1 = jnp.where(row == 0, prev[SUBLANES - 1:SUBLANES], pltpu.roll(gate, 1, 0))
        g2 = jnp.where(row == 0, prev[SUBLANES - 2:SUBLANES - 1],
                       jnp.where(row == 1, prev[SUBLANES - 1:SUBLANES], pltpu.roll(gate, 2, 0)))
        carry_ref[:, cs] = gate[tm - SUBLANES:tm]
        conv = cw_ref[0:1, cs] * g2 + cw_ref[1:2, cs] * g1 + cw_ref[2:3, cs] * gate + cb_ref[:, cs]
        hidden = (conv * jax.nn.sigmoid(conv) * up).astype(BF16)
        acc = acc + _dot(hidden, wd_ref[cs, :])
    out_ref[...] = _rms_norm(acc, gl_ref[...])


def _ffn(y_rwkv, y_ret, x2, w_out, g_ffn, w_gate, w_up, conv_w, conv_b, w_down, g_final, bsz, seq):
    tm = FFN_TILE
    nt = seq // tm
    row = lambda b, t: (b * nt + t, 0)
    const = lambda b, t: (0, 0)
    single = dict(pipeline_mode=pl.Buffered(1))
    weights = (D_MODEL * D_MODEL + 3 * D_MODEL * D_FF) * 2
    vmem = weights + 4 * tm * D_MODEL * 4 + 8 * tm * D_FF * 4 + 8 * 1024 * 1024
    return pl.pallas_call(
        _ffn_kernel,
        grid=(bsz, nt),
        in_specs=[
            pl.BlockSpec((tm, RWKV_WIDTH), row),
            pl.BlockSpec((tm, RET_WIDTH), row),
            pl.BlockSpec((tm, D_MODEL), row),
            pl.BlockSpec((D_MODEL, D_MODEL), const, **single),
            pl.BlockSpec((1, D_MODEL), const),
            pl.BlockSpec((D_MODEL, D_FF), const, **single),
            pl.BlockSpec((D_MODEL, D_FF), const, **single),
            pl.BlockSpec((SUBLANES, D_FF), const),
            pl.BlockSpec((1, D_FF), const),
            pl.BlockSpec((D_FF, D_MODEL), const, **single),
            pl.BlockSpec((1, D_MODEL), const),
        ],
        out_specs=pl.BlockSpec((tm, D_MODEL), row),
        out_shape=jax.ShapeDtypeStruct((bsz * seq, D_MODEL), F32),
        scratch_shapes=[pltpu.VMEM((SUBLANES, D_FF), F32)],
        compiler_params=pltpu.CompilerParams(
            dimension_semantics=("arbitrary", "arbitrary"), vmem_limit_bytes=_vmem_limit(vmem)),
        name="ffn",
    )(y_rwkv, y_ret, x2, w_out, g_ffn, w_gate, w_up, conv_w, conv_b, w_down, g_final)


def _block_ones(n, block):
    i = jnp.arange(n)
    return ((i[:, None] // block) == (i[None, :] // block)).astype(BF16)


def kernel(x, norm_mix_g, w_in, rwkv_mu_r, rwkv_mu_k, rwkv_mu_v, rwkv_mu_w, rwkv_mu_a, rwkv_mu_g, rwkv_w0, rwkv_w1, rwkv_w2, rwkv_a0, rwkv_a1, rwkv_a2, rwkv_g1, rwkv_g2, rwkv_k_k, rwkv_k_a, rwkv_r_k, rwkv_lnx_w, rwkv_lnx_b, ret_gn_w, w_out, norm_ffn_g, ffn_w_gate, ffn_w_up, ffn_conv_w, ffn_conv_b, ffn_w_down, norm_final_g):
    bsz, seq, d = x.shape
    assert d == D_MODEL and seq % PROJ_TILE == 0 and seq % RWKV_TILE == 0 and seq % FFN_TILE == 0
    assert norm_mix_g.shape[0] == 1, "one layer"
    x2 = x.reshape(bsz * seq, d)
    row = lambda p: p.reshape(1, -1)

    w_cat = jnp.concatenate([rwkv_w1[0], rwkv_a1[0], rwkv_g1[0]], axis=1)
    mu_cat = jnp.concatenate([
        jnp.broadcast_to(rwkv_mu_w[0][:, None], (d, DECAY_LORA)),
        jnp.broadcast_to(rwkv_mu_a[0][:, None], (d, AAA_LORA)),
        jnp.broadcast_to(rwkv_mu_g[0][:, None], (d, GATE_LORA))], axis=1)
    w_ext = jnp.concatenate([w_in[0].astype(BF16), _fold_lora(w_cat, mu_cat)], axis=1)
    p_rkv, p_ret, p_lora = _norm_proj(x2, row(norm_mix_g[0]), w_ext)

    vec_rows = [rwkv_mu_r[0], rwkv_mu_k[0], rwkv_mu_v[0], rwkv_w0[0], rwkv_a0[0], rwkv_k_k[0], rwkv_k_a[0],
                rwkv_r_k[0].reshape(-1), rwkv_lnx_w[0], rwkv_lnx_b[0]]
    vecs = jnp.zeros((_N_VEC_ROWS, RWKV_WIDTH), F32).at[:len(vec_rows)].set(jnp.stack(vec_rows))
    w2cat = jnp.zeros((LORA_WIDTH, 3 * RWKV_WIDTH), BF16)
    w2cat = w2cat.at[0:DECAY_LORA, 0:RWKV_WIDTH].set(rwkv_w2[0].astype(BF16))
    w2cat = w2cat.at[DECAY_LORA:DECAY_LORA + AAA_LORA, RWKV_WIDTH:2 * RWKV_WIDTH].set(rwkv_a2[0].astype(BF16))
    w2cat = w2cat.at[DECAY_LORA + AAA_LORA:, 2 * RWKV_WIDTH:].set(rwkv_g2[0].astype(BF16))
    seg = _block_ones(RWKV_WIDTH, RWKV_HEAD_DIM)
    ti = jnp.arange(RWKV_TILE)
    cones = _block_ones(RWKV_TILE, RWKV_CHUNK)
    tril = (cones.astype(F32) * (ti[:, None] >= ti[None, :])).astype(BF16)
    y_rwkv = _rwkv(p_rkv, p_lora, vecs, w2cat, seg, tril, cones, bsz, seq)

    half = RET_HEAD_DIM // 2
    inv_freq = ROPE_BASE ** (-jnp.arange(half, dtype=F32) / half)
    freq = jnp.concatenate([inv_freq, inv_freq]).reshape(1, RET_HEAD_DIM)
    y_ret = _retention(p_ret, freq, row(ret_gn_w[0]), bsz, seq)

    conv_w = jnp.zeros((SUBLANES, D_FF), F32).at[0:3].set(ffn_conv_w[0][:, 0, :])
    out = _ffn(y_rwkv, y_ret, x2, w_out[0].astype(BF16), row(norm_ffn_g[0]), ffn_w_gate[0].astype(BF16),
               ffn_w_up[0].astype(BF16), conv_w, row(ffn_conv_b[0]), ffn_w_down[0].astype(BF16),
               row(norm_final_g), bsz, seq)
    return out.reshape(bsz, seq, d)
```

```python
import math

import jax
import jax.numpy as jnp
from jax import lax
from jax.experimental import pallas as pl
from jax.experimental.pallas import tpu as pltpu

F32 = jnp.float32
BF16 = jnp.bfloat16

D_MODEL = 1024
RWKV_HEADS = 8
RWKV_HEAD_DIM = 64
RWKV_WIDTH = 512
RET_HEADS = 4
RET_HEAD_DIM = 128
RET_WIDTH = 512
DECAY_LORA = 64
AAA_LORA = 64
GATE_LORA = 128
LORA_WIDTH = DECAY_LORA + AAA_LORA + GATE_LORA
RET_CHUNK = 128
ROPE_BASE = 10000.0
D_FF = 2816
NORM_EPS = 1e-6
RWKV_GN_EPS = 64e-5
RET_GN_EPS = 1e-5

V7X_VMEM_BYTES = 64 * 1024 * 1024
SUBLANES = 8
LANES = 128

RWKV_CHUNK = 64
RWKV_GROUP_HEADS = 2
RWKV_TILE = 256
RET_TILE = 256
PROJ_TILE = 512
FFN_TILE = 256
FFN_COL_SPLIT = 2


def _vmem_limit(nbytes):
    return int(min(nbytes, V7X_VMEM_BYTES - 4 * 1024 * 1024))


def _dot(a, b):
    return jnp.dot(a, b, preferred_element_type=F32)


def _dot_nt(a, b):
    return lax.dot_general(a, b, (((1,), (1,)), ((), ())), preferred_element_type=F32)


def _dot_tn(a, b):
    return lax.dot_general(a, b, (((0,), (0,)), ((), ())), preferred_element_type=F32)


def _split_dot(mat_bf16, x):
    hi = x.astype(BF16)
    lo = (x - hi.astype(F32)).astype(BF16)
    return _dot(mat_bf16, hi) + _dot(mat_bf16, lo)


def _split_dot_right(x, mat_bf16):
    hi = x.astype(BF16)
    lo = (x - hi.astype(F32)).astype(BF16)
    return _dot(hi, mat_bf16) + _dot(lo, mat_bf16)


def _shift_rows(x, prev_row):
    rolled = pltpu.roll(x, 1, 0)
    row = lax.broadcasted_iota(jnp.int32, x.shape, 0)
    return jnp.where(row == 0, prev_row, rolled)


def _rms_norm(x, g):
    ms = jnp.mean(x * x, axis=-1, keepdims=True)
    return x * lax.rsqrt(ms + NORM_EPS) * g


def _fold_lora_kernel(w_ref, mu_ref, o_ref):
    w = w_ref[...]
    mu = mu_ref[...]
    o_ref[:, 0:LORA_WIDTH] = (w * (1.0 - mu)).astype(BF16)
    o_ref[:, LORA_WIDTH:2 * LORA_WIDTH] = (w * mu).astype(BF16)


def _fold_lora(w_cat, mu_cat):
    return pl.pallas_call(
        _fold_lora_kernel,
        out_shape=jax.ShapeDtypeStruct((D_MODEL, 2 * LORA_WIDTH), BF16),
        name="fold_lora",
    )(w_cat, mu_cat)


N_RKV = 3 * RWKV_WIDTH
N_RET = 4 * RET_WIDTH
N_LORA = 2 * LORA_WIDTH
N_PROJ = N_RKV + N_RET + N_LORA


def _norm_proj_kernel(x_ref, g_ref, w_ref, rkv_ref, ret_ref, lora_ref):
    xb = _rms_norm(x_ref[...], g_ref[...]).astype(BF16)
    rkv_ref[...] = _dot(xb, w_ref[:, 0:N_RKV])
    ret_ref[...] = _dot(xb, w_ref[:, N_RKV:N_RKV + N_RET])
    lora_ref[...] = _dot(xb, w_ref[:, N_RKV + N_RET:N_PROJ])


def _norm_proj(x2, g, w_ext):
    n = x2.shape[0]
    tm = PROJ_TILE
    const = lambda i: (0, 0)
    row = lambda i: (i, 0)
    vmem = 2 * tm * D_MODEL * 4 + D_MODEL * N_PROJ * 2 + 2 * tm * N_PROJ * 4 + 3 * tm * N_PROJ * 4
    return pl.pallas_call(
        _norm_proj_kernel,
        grid=(n // tm,),
        in_specs=[
            pl.BlockSpec((tm, D_MODEL), row),
            pl.BlockSpec((1, D_MODEL), const),
            pl.BlockSpec((D_MODEL, N_PROJ), const, pipeline_mode=pl.Buffered(1)),
        ],
        out_specs=[
            pl.BlockSpec((tm, N_RKV), row),
            pl.BlockSpec((tm, N_RET), row),
            pl.BlockSpec((tm, N_LORA), row),
        ],
        out_shape=[
            jax.ShapeDtypeStruct((n, N_RKV), F32),
            jax.ShapeDtypeStruct((n, N_RET), F32),
            jax.ShapeDtypeStruct((n, N_LORA), F32),
        ],
        compiler_params=pltpu.CompilerParams(
            dimension_semantics=("arbitrary",), vmem_limit_bytes=_vmem_limit(vmem)),
        name="norm_proj",
    )(x2, g, w_ext)


(_V_MU_R, _V_MU_K, _V_MU_V, _V_W0, _V_A0, _V_KK, _V_KA, _V_RK, _V_LNW, _V_LNB) = range(10)
_N_VEC_ROWS = 16


def _rwkv_kernel(prkv_ref, prkv_prev_ref, plora_ref, plora_prev_ref, vec_ref, w2_ref, seg_ref,
                 tril_ref, out_ref, h_ref):
    C, G, TT = RWKV_CHUNK, RWKV_GROUP_HEADS, RWKV_TILE
    HD = RWKV_HEAD_DIM
    W = G * HD
    GC = G * C
    n_groups = RWKV_WIDTH // W
    t_idx = pl.program_id(1)

    @pl.when(t_idx == 0)
    def _():
        h_ref[...] = jnp.zeros_like(h_ref)

    vec = lambda i: vec_ref[i:i + 1, :]
    first = t_idx == 0

    prkv = prkv_ref[...]
    srkv = _shift_rows(prkv, jnp.where(first, 0.0, prkv_prev_ref[SUBLANES - 1:SUBLANES, :]))
    lerp = lambda j, mu: (prkv[:, j * RWKV_WIDTH:(j + 1) * RWKV_WIDTH]
                          + (srkv[:, j * RWKV_WIDTH:(j + 1) * RWKV_WIDTH]
                             - prkv[:, j * RWKV_WIDTH:(j + 1) * RWKV_WIDTH]) * mu)
    r = lerp(0, vec(_V_MU_R))
    k = lerp(1, vec(_V_MU_K))
    v = lerp(2, vec(_V_MU_V))

    plora = plora_ref[...]
    slora = _shift_rows(plora, jnp.where(first, 0.0, plora_prev_ref[SUBLANES - 1:SUBLANES, :]))
    low = plora[:, 0:LORA_WIDTH] + slora[:, LORA_WIDTH:2 * LORA_WIDTH]
    lane = lax.broadcasted_iota(jnp.int32, low.shape, 1)
    act = jnp.where(lane < DECAY_LORA, jnp.tanh(low),
                    jnp.where(lane < DECAY_LORA + AAA_LORA, low, jax.nn.sigmoid(low)))
    second = _dot(act.astype(BF16), w2_ref[...])
    z = -(second[:, 0:RWKV_WIDTH] + vec(_V_W0))
    softplus = jnp.maximum(z, 0.0) + jnp.log(1.0 + jnp.exp(-jnp.abs(z)))
    ld = -jnp.exp(-softplus - 0.5)
    a = jax.nn.sigmoid(second[:, RWKV_WIDTH:2 * RWKV_WIDTH] + vec(_V_A0))
    gate = second[:, 2 * RWKV_WIDTH:3 * RWKV_WIDTH]

    seg = seg_ref[...]

    def head_sum(x):
        n_blk = x.shape[1] // LANES
        rows = jnp.concatenate([x[:, j * LANES:(j + 1) * LANES] for j in range(n_blk)], axis=0)
        s = _split_dot_right(rows, seg)
        return jnp.concatenate([s[j * TT:(j + 1) * TT] for j in range(n_blk)], axis=1)

    kk = k * vec(_V_KK)
    kk = kk / jnp.maximum(jnp.sqrt(head_sum(kk * kk)), 1e-12)
    k2 = k * (1.0 + (a - 1.0) * vec(_V_KA))
    b = kk * a
    bonus = head_sum(r * k2 * vec(_V_RK)) * v

    cums = _split_dot(tril_ref[...], ld)
    cum = cums[0:TT]
    cum_end = cums[TT:]
    e_neg = jnp.exp(-cum)
    e_end = jnp.exp(cum_end - cum)
    a_t = -kk * jnp.exp(cum - ld)
    r_t = r * jnp.exp(cum)
    b_t = b * e_neg
    k_t = k2 * e_neg
    b_w = b * e_end
    k_w = k2 * e_end
    w_c = jnp.exp(cum_end)

    c_bits = C.bit_length() - 1
    hd_bits = HD.bit_length() - 1
    assert C == 1 << c_bits and HD == 1 << hd_bits
    sr = lax.broadcasted_iota(jnp.int32, (GC, W), 0)
    sl = lax.broadcasted_iota(jnp.int32, (GC, W), 1)
    stack_mask = (sr >> c_bits) == (sl >> hd_bits)
    assert W == GC
    ti = lax.broadcasted_iota(jnp.int32, (C, GC), 0)
    si = lax.broadcasted_iota(jnp.int32, (C, GC), 1) & (C - 1)
    strict = ti > si
    incl = ti >= si
    eye_c = (ti == si).astype(F32)
    wi = lax.broadcasted_iota(jnp.int32, (W, W), 0)
    wj = lax.broadcasted_iota(jnp.int32, (W, W), 1)
    eye_w = wi == wj

    def stack(x):
        return jnp.where(stack_mask, jnp.concatenate([x] * G, axis=0), 0.0)

    chains = [(c, g) for c in range(TT // C) for g in range(n_groups)]
    blk = lambda z, c, g: z[c * C:(c + 1) * C, g * W:(g + 1) * W]
    st = []
    for c, g in chains:
        st.append(dict(
            ar=jnp.concatenate([blk(a_t, c, g), blk(r_t, c, g)], axis=0).astype(BF16), r32=blk(r_t, c, g),
            a_s=stack(blk(a_t, c, g)).astype(BF16),
            bk_s=jnp.concatenate([stack(blk(b_t, c, g)), stack(blk(k_t, c, g))], axis=0).astype(BF16),
            bw_t=stack(blk(b_w, c, g)).T.astype(BF16), kw_t=stack(blk(k_w, c, g)).T.astype(BF16),
            v_s=stack(blk(v, c, g)).astype(BF16), w_end=blk(w_c, c, g)[0:1, :]))
    for s in st:
        prod = _dot_nt(s["ar"], s["bk_s"])
        s["n"] = jnp.where(strict, prod[0:C, 0:GC], 0.0)
        s["a_ak"] = jnp.where(strict, prod[0:C, GC:], 0.0).astype(BF16)
        s["a_rb"] = jnp.where(incl, prod[C:, 0:GC], 0.0).astype(BF16)
        s["a_rk"] = jnp.where(incl, prod[C:, GC:], 0.0).astype(BF16)
    for s in st:
        res = _dot(jnp.concatenate([s["a_ak"], s["a_rk"], s["kw_t"]], axis=0), s["v_s"])
        s["av_s"] = stack(res[0:C]).astype(BF16)
        s["rkv"] = res[C:2 * C]
        s["kwv"] = res[2 * C:]
    for s in st:
        nb = s["n"].astype(BF16)
        s["t"] = eye_c + s["n"]
        s["p"] = _dot(nb, stack(s["n"]).astype(BF16))
    for i in range(1, 6):
        for s in st:
            p_s = stack(s["p"]).astype(BF16)
            if i < 5:
                res = _dot(jnp.concatenate([s["t"], s["p"]], axis=0).astype(BF16), p_s)
                s["t"] = s["t"] + res[0:C]
                s["p"] = res[C:]
            else:
                s["t"] = (s["t"] + _dot(s["t"].astype(BF16), p_s)).astype(BF16)
    for s in st:
        s["x1_s"] = stack(_dot(s["t"], s["a_s"])).astype(BF16)
        s["x2_s"] = stack(_dot(s["t"], s["av_s"])).astype(BF16)
    for s in st:
        lhs = jnp.concatenate([s["a_rb"], s["bw_t"]], axis=0)
        o1 = _dot(lhs, s["x1_s"])
        o2 = _dot(lhs, s["x2_s"])
        s["q"] = (s["r32"] + o1[0:C]).astype(BF16)
        s["m"] = (o1[C:] + jnp.where(eye_w, s["w_end"], 0.0)).astype(BF16)
        s["y0"] = o2[0:C] + s["rkv"]
        s["g"] = o2[C:] + s["kwv"]
    h_cur = [h_ref[g] for g in range(n_groups)]
    y_rows = []
    for c in range(TT // C):
        y_lanes = []
        for g in range(n_groups):
            s = st[c * n_groups + g]
            res = _dot(jnp.concatenate([s["q"], s["m"]], axis=0), h_cur[g].astype(BF16))
            y_lanes.append(res[0:C] + s["y0"])
            h_cur[g] = res[C:] + s["g"]
        y_rows.append(jnp.concatenate(y_lanes, axis=1))
    for g in range(n_groups):
        h_ref[g] = h_cur[g]

    y = jnp.concatenate(y_rows, axis=0)
    inv_hd = 1.0 / HD
    mu = head_sum(y) * inv_hd
    d = y - mu
    var = head_sum(d * d) * inv_hd
    yn = d * lax.rsqrt(var + RWKV_GN_EPS) * vec(_V_LNW) + vec(_V_LNB)
    out_ref[...] = ((yn + bonus) * gate).astype(BF16)


def _rwkv(p_rkv, p_lora, vecs, w2cat, seg, tril, bsz, seq):
    tt = RWKV_TILE
    nt = seq // tt
    w = RWKV_GROUP_HEADS * RWKV_HEAD_DIM
    n_groups = RWKV_WIDTH // w
    row = lambda b, t: (b * nt + t, 0)
    prev = lambda b, t: (jnp.maximum((b * nt + t) * (tt // SUBLANES) - 1, 0), 0)
    const = lambda b, t: (0, 0)
    return pl.pallas_call(
        _rwkv_kernel,
        grid=(bsz, nt),
        in_specs=[
            pl.BlockSpec((tt, N_RKV), row),
            pl.BlockSpec((SUBLANES, N_RKV), prev),
            pl.BlockSpec((tt, N_LORA), row),
            pl.BlockSpec((SUBLANES, N_LORA), prev),
            pl.BlockSpec((_N_VEC_ROWS, RWKV_WIDTH), const),
            pl.BlockSpec((LORA_WIDTH, 3 * RWKV_WIDTH), const),
            pl.BlockSpec((LANES, LANES), const),
            pl.BlockSpec((2 * tt, tt), const),
        ],
        out_specs=pl.BlockSpec((tt, RWKV_WIDTH), row),
        out_shape=jax.ShapeDtypeStruct((bsz * seq, RWKV_WIDTH), BF16),
        scratch_shapes=[pltpu.VMEM((n_groups, w, w), F32)],
        compiler_params=pltpu.CompilerParams(
            dimension_semantics=("arbitrary", "arbitrary"), vmem_limit_bytes=_vmem_limit(48 * 1024 * 1024)),
        name="rwkv",
    )(p_rkv, p_rkv, p_lora, p_lora, vecs, w2cat, seg, tril)


def _ret_tables_kernel(freq_ref, cos_ref, sin_ref, dmask_ref):
    C, HD = RET_CHUNK, RET_HEAD_DIM
    n = cos_ref.shape[0]
    ang = lax.broadcasted_iota(jnp.int32, (n, HD), 0).astype(F32) * freq_ref[...]
    lane = lax.broadcasted_iota(jnp.int32, (n, HD), 1)
    sin = jnp.sin(ang)
    cos_ref[...] = jnp.cos(ang)
    sin_ref[...] = jnp.where(lane < HD // 2, -sin, sin)
    ii = lax.broadcasted_iota(jnp.int32, (C, C), 0)
    jj = lax.broadcasted_iota(jnp.int32, (C, C), 1)
    diff = jnp.maximum((ii - jj).astype(F32), 0.0)
    for h in range(RET_HEADS):
        dmask_ref[h] = jnp.where(ii >= jj, jnp.exp(diff * _LOG_GAMMA[h]), 0.0)


def _ret_tables(freq, seq):
    return pl.pallas_call(
        _ret_tables_kernel,
        out_shape=[jax.ShapeDtypeStruct((seq, RET_HEAD_DIM), F32),
                   jax.ShapeDtypeStruct((seq, RET_HEAD_DIM), F32),
                   jax.ShapeDtypeStruct((RET_HEADS, RET_CHUNK, RET_CHUNK), F32)],
        name="ret_tables",
    )(freq)


def _retention_kernel(p_ref, cos_ref, sin_ref, dmask_ref, gnw_ref, out_ref, state_ref):
    C, HD = RET_CHUNK, RET_HEAD_DIM

    @pl.when(pl.program_id(1) == 0)
    def _():
        state_ref[...] = jnp.zeros_like(state_ref)

    row_i = lax.broadcasted_iota(jnp.int32, (C, 1), 0).astype(F32)
    for h in range(RET_HEADS):
        lg = _LOG_GAMMA[h]
        zeta = jnp.exp((C - 1.0 - row_i) * lg)
        xi = jnp.exp((row_i + 1.0) * lg)
        state = state_ref[h]
        for c in range(RET_TILE // C):
            rows = slice(c * C, (c + 1) * C)
            cos, sin = cos_ref[rows, :], sin_ref[rows, :]
            rot = lambda xh: xh * cos + pltpu.roll(xh, HD // 2, 1) * sin
            sl = lambda j: slice(j * RET_WIDTH + h * HD, j * RET_WIDTH + (h + 1) * HD)
            q = rot(p_ref[rows, sl(0)])
            k = rot(p_ref[rows, sl(1)]) * (HD ** -0.5)
            v = p_ref[rows, sl(2)]
            gt = p_ref[rows, sl(3)]
            qb, kb, vb = q.astype(BF16), k.astype(BF16), v.astype(BF16)
            scores = _dot_nt(qb, kb) * dmask_ref[h]
            intra = _dot(scores.astype(BF16), vb)
            kv = _dot_tn(kb, (v * zeta).astype(BF16))
            inter = _dot((q * xi).astype(BF16), state.astype(BF16))
            state = state * _GAMMA_C[h] + kv
            y = intra + inter
            mu = jnp.mean(y, axis=-1, keepdims=True)
            d = y - mu
            var = jnp.mean(d * d, axis=-1, keepdims=True)
            yn = d * lax.rsqrt(var + RET_GN_EPS) * gnw_ref[:, h * HD:(h + 1) * HD]
            out_ref[rows, h * HD:(h + 1) * HD] = (gt * jax.nn.sigmoid(gt) * yn).astype(BF16)
        state_ref[h] = state


_LOG_GAMMA = [math.log(1.0 - 2.0 ** (-5.0 - h)) for h in range(RET_HEADS)]
_GAMMA_C = [math.exp(RET_CHUNK * lg) for lg in _LOG_GAMMA]


def _retention(p_ret, freq, gn_w, bsz, seq):
    tt = RET_TILE
    nt = seq // tt
    cos, sin, dmask = _ret_tables(freq, seq)
    row = lambda b, i: (b * nt + i, 0)
    pos = lambda b, i: (i, 0)
    return pl.pallas_call(
        _retention_kernel,
        grid=(bsz, nt),
        in_specs=[
            pl.BlockSpec((tt, N_RET), row),
            pl.BlockSpec((tt, RET_HEAD_DIM), pos),
            pl.BlockSpec((tt, RET_HEAD_DIM), pos),
            pl.BlockSpec((RET_HEADS, RET_CHUNK, RET_CHUNK), lambda b, i: (0, 0, 0)),
            pl.BlockSpec((1, RET_WIDTH), lambda b, i: (0, 0)),
        ],
        out_specs=pl.BlockSpec((tt, RET_WIDTH), row),
        out_shape=jax.ShapeDtypeStruct((bsz * seq, RET_WIDTH), BF16),
        scratch_shapes=[pltpu.VMEM((RET_HEADS, RET_HEAD_DIM, RET_HEAD_DIM), F32)],
        compiler_params=pltpu.CompilerParams(dimension_semantics=("arbitrary", "arbitrary")),
        name="retention",
    )(p_ret, cos, sin, dmask, gn_w)


def _ffn_kernel(yr_ref, yt_ref, x_ref, wo_ref, gf_ref, wg_ref, wu_ref, cw_ref, cb_ref, wd_ref, gl_ref,
                out_ref, carry_ref):
    tm = FFN_TILE
    t_idx = pl.program_id(1)

    @pl.when(t_idx == 0)
    def _():
        carry_ref[...] = jnp.zeros_like(carry_ref)

    mix = _dot(yr_ref[...], wo_ref[0:RWKV_WIDTH, :]) + _dot(yt_ref[...], wo_ref[RWKV_WIDTH:, :])
    x1 = x_ref[...] + mix
    hb = _rms_norm(x1, gf_ref[...]).astype(BF16)

    cols = D_FF // FFN_COL_SPLIT
    acc = x1
    for j in range(FFN_COL_SPLIT):
        cs = slice(j * cols, (j + 1) * cols)
        gate = _dot(hb, wg_ref[:, cs])
        up = _dot(hb, wu_ref[:, cs])
        prev = carry_ref[:, cs]
        row = lax.broadcasted_iota(jnp.int32, gate.shape, 0)
        g1 = jnp.where(row == 0, prev[SUBLANES - 1:SUBLANES], pltpu.roll(gate, 1, 0))
        g2 = jnp.where(row == 0, prev[SUBLANES - 2:SUBLANES - 1],
                       jnp.where(row == 1, prev[SUBLANES - 1:SUBLANES], pltpu.roll(gate, 2, 0)))
        carry_ref[:, cs] = gate[tm - SUBLANES:tm]
        conv = cw_ref[0:1, cs] * g2 + cw_ref[1:2, cs] * g1 + cw_ref[2:3, cs] * gate + cb_ref[:, cs]
        hidden = (conv * jax.nn.sigmoid(conv) * up).astype(BF16)
        acc = acc + _dot(hidden, wd_ref[cs, :])
    out_ref[...] = _rms_norm(acc, gl_ref[...])


def _ffn(y_rwkv, y_ret, x2, w_out, g_ffn, w_gate, w_up, conv_w, conv_b, w_down, g_final, bsz, seq):
    tm = FFN_TILE
    nt = seq // tm
    row = lambda b, t: (b * nt + t, 0)
    const = lambda b, t: (0, 0)
    single = dict(pipeline_mode=pl.Buffered(1))
    weights = (D_MODEL * D_MODEL + 3 * D_MODEL * D_FF) * 2
    vmem = weights + 4 * tm * D_MODEL * 4 + 8 * tm * D_FF * 4 + 8 * 1024 * 1024
    return pl.pallas_call(
        _ffn_kernel,
        grid=(bsz, nt),
        in_specs=[
            pl.BlockSpec((tm, RWKV_WIDTH), row),
            pl.BlockSpec((tm, RET_WIDTH), row),
            pl.BlockSpec((tm, D_MODEL), row),
            pl.BlockSpec((D_MODEL, D_MODEL), const, **single),
            pl.BlockSpec((1, D_MODEL), const),
            pl.BlockSpec((D_MODEL, D_FF), const, **single),
            pl.BlockSpec((D_MODEL, D_FF), const, **single),
            pl.BlockSpec((SUBLANES, D_FF), const),
            pl.BlockSpec((1, D_FF), const),
            pl.BlockSpec((D_FF, D_MODEL), const, **single),
            pl.BlockSpec((1, D_MODEL), const),
        ],
        out_specs=pl.BlockSpec((tm, D_MODEL), row),
        out_shape=jax.ShapeDtypeStruct((bsz * seq, D_MODEL), F32),
        scratch_shapes=[pltpu.VMEM((SUBLANES, D_FF), F32)],
        compiler_params=pltpu.CompilerParams(
            dimension_semantics=("arbitrary", "arbitrary"), vmem_limit_bytes=_vmem_limit(vmem)),
        name="ffn",
    )(y_rwkv, y_ret, x2, w_out, g_ffn, w_gate, w_up, conv_w, conv_b, w_down, g_final)


def _block_ones(n, block):
    i = jnp.arange(n)
    return ((i[:, None] // block) == (i[None, :] // block)).astype(BF16)


def kernel(x, norm_mix_g, w_in, rwkv_mu_r, rwkv_mu_k, rwkv_mu_v, rwkv_mu_w, rwkv_mu_a, rwkv_mu_g, rwkv_w0, rwkv_w1, rwkv_w2, rwkv_a0, rwkv_a1, rwkv_a2, rwkv_g1, rwkv_g2, rwkv_k_k, rwkv_k_a, rwkv_r_k, rwkv_lnx_w, rwkv_lnx_b, ret_gn_w, w_out, norm_ffn_g, ffn_w_gate, ffn_w_up, ffn_conv_w, ffn_conv_b, ffn_w_down, norm_final_g):
    bsz, seq, d = x.shape
    assert d == D_MODEL and all(seq % t == 0 for t in (PROJ_TILE, RWKV_TILE, RET_TILE, FFN_TILE))
    assert norm_mix_g.shape[0] == 1, "one layer"
    x2 = x.reshape(bsz * seq, d)
    row = lambda p: p.reshape(1, -1)

    w_cat = jnp.concatenate([rwkv_w1[0], rwkv_a1[0], rwkv_g1[0]], axis=1)
    mu_cat = jnp.concatenate([
        jnp.broadcast_to(rwkv_mu_w[0][:, None], (d, DECAY_LORA)),
        jnp.broadcast_to(rwkv_mu_a[0][:, None], (d, AAA_LORA)),
        jnp.broadcast_to(rwkv_mu_g[0][:, None], (d, GATE_LORA))], axis=1)
    w_ext = jnp.concatenate([w_in[0].astype(BF16), _fold_lora(w_cat, mu_cat)], axis=1)
    p_rkv, p_ret, p_lora = _norm_proj(x2, row(norm_mix_g[0]), w_ext)

    vec_rows = [rwkv_mu_r[0], rwkv_mu_k[0], rwkv_mu_v[0], rwkv_w0[0], rwkv_a0[0], rwkv_k_k[0], rwkv_k_a[0],
                rwkv_r_k[0].reshape(-1), rwkv_lnx_w[0], rwkv_lnx_b[0]]
    vecs = jnp.zeros((_N_VEC_ROWS, RWKV_WIDTH), F32).at[:len(vec_rows)].set(jnp.stack(vec_rows))
    w2cat = jnp.zeros((LORA_WIDTH, 3 * RWKV_WIDTH), BF16)
    w2cat = w2cat.at[0:DECAY_LORA, 0:RWKV_WIDTH].set(rwkv_w2[0].astype(BF16))
    w2cat = w2cat.at[DECAY_LORA:DECAY_LORA + AAA_LORA, RWKV_WIDTH:2 * RWKV_WIDTH].set(rwkv_a2[0].astype(BF16))
    w2cat = w2cat.at[DECAY_LORA + AAA_LORA:, 2 * RWKV_WIDTH:].set(rwkv_g2[0].astype(BF16))
    seg = _block_ones(LANES, RWKV_HEAD_DIM)
    ti = jnp.arange(RWKV_TILE)
    cones = _block_ones(RWKV_TILE, RWKV_CHUNK)
    tril = (cones.astype(F32) * (ti[:, None] >= ti[None, :])).astype(BF16)
    y_rwkv = _rwkv(p_rkv, p_lora, vecs, w2cat, seg, jnp.concatenate([tril, cones], axis=0), bsz, seq)

    half = RET_HEAD_DIM // 2
    inv_freq = ROPE_BASE ** (-jnp.arange(half, dtype=F32) / half)
    freq = jnp.concatenate([inv_freq, inv_freq]).reshape(1, RET_HEAD_DIM)
    y_ret = _retention(p_ret, freq, row(ret_gn_w[0]), bsz, seq)

    conv_w = jnp.zeros((SUBLANES, D_FF), F32).at[0:3].set(ffn_conv_w[0][:, 0, :])
    out = _ffn(y_rwkv, y_ret, x2, w_out[0].astype(BF16), row(norm_ffn_g[0]), ffn_w_gate[0].astype(BF16),
               ffn_w_up[0].astype(BF16), conv_w, row(ffn_conv_b[0]), ffn_w_down[0].astype(BF16),
               row(norm_final_g), bsz, seq)
    return out.reshape(bsz, seq, d)
```

```python
import math

import jax
import jax.numpy as jnp
from jax import lax
from jax.experimental import pallas as pl
from jax.experimental.pallas import tpu as pltpu

F32 = jnp.float32
BF16 = jnp.bfloat16

D_MODEL = 1024
RWKV_HEADS = 8
RWKV_HEAD_DIM = 64
RWKV_WIDTH = 512
RET_HEADS = 4
RET_HEAD_DIM = 128
RET_WIDTH = 512
DECAY_LORA = 64
AAA_LORA = 64
GATE_LORA = 128
LORA_WIDTH = DECAY_LORA + AAA_LORA + GATE_LORA
RET_CHUNK = 128
ROPE_BASE = 10000.0
D_FF = 2816
NORM_EPS = 1e-6
RWKV_GN_EPS = 64e-5
RET_GN_EPS = 1e-5

V7X_VMEM_BYTES = 64 * 1024 * 1024
SUBLANES = 8
LANES = 128

RWKV_CHUNK = 64
RWKV_GROUP_HEADS = 2
RWKV_TILE = 512
RWKV_STAGE = 256
RET_TILE = 256
PROJ_TILE = 512
FFN_TILE = 256
FFN_COL_SPLIT = 2


def _vmem_limit(nbytes):
    return int(min(nbytes, V7X_VMEM_BYTES - 4 * 1024 * 1024))


def _dot(a, b):
    return jnp.dot(a, b, preferred_element_type=F32)


def _dot_nt(a, b):
    return lax.dot_general(a, b, (((1,), (1,)), ((), ())), preferred_element_type=F32)


def _dot_tn(a, b):
    return lax.dot_general(a, b, (((0,), (0,)), ((), ())), preferred_element_type=F32)


def _split_dot(mat_bf16, x):
    hi = x.astype(BF16)
    lo = (x - hi.astype(F32)).astype(BF16)
    return _dot(mat_bf16, hi) + _dot(mat_bf16, lo)


def _split_dot_right(x, mat_bf16):
    hi = x.astype(BF16)
    lo = (x - hi.astype(F32)).astype(BF16)
    return _dot(hi, mat_bf16) + _dot(lo, mat_bf16)


def _shift_rows(x, prev_row):
    rolled = pltpu.roll(x, 1, 0)
    row = lax.broadcasted_iota(jnp.int32, x.shape, 0)
    return jnp.where(row == 0, prev_row, rolled)


def _rms_norm(x, g):
    ms = jnp.mean(x * x, axis=-1, keepdims=True)
    return x * lax.rsqrt(ms + NORM_EPS) * g


def _fold_lora_kernel(w_ref, mu_ref, o_ref):
    w = w_ref[...]
    mu = mu_ref[...]
    o_ref[:, 0:LORA_WIDTH] = (w * (1.0 - mu)).astype(BF16)
    o_ref[:, LORA_WIDTH:2 * LORA_WIDTH] = (w * mu).astype(BF16)


def _fold_lora(w_cat, mu_cat):
    return pl.pallas_call(
        _fold_lora_kernel,
        out_shape=jax.ShapeDtypeStruct((D_MODEL, 2 * LORA_WIDTH), BF16),
        name="fold_lora",
    )(w_cat, mu_cat)


N_RKV = 3 * RWKV_WIDTH
N_RET = 4 * RET_WIDTH
N_LORA = 2 * LORA_WIDTH
N_PROJ = N_RKV + N_RET + N_LORA


def _norm_proj_kernel(x_ref, g_ref, w_ref, rkv_ref, ret_ref, lora_ref):
    xb = _rms_norm(x_ref[...], g_ref[...]).astype(BF16)
    rkv_ref[...] = _dot(xb, w_ref[:, 0:N_RKV])
    ret_ref[...] = _dot(xb, w_ref[:, N_RKV:N_RKV + N_RET])
    lora_ref[...] = _dot(xb, w_ref[:, N_RKV + N_RET:N_PROJ])


def _norm_proj(x2, g, w_ext):
    n = x2.shape[0]
    tm = PROJ_TILE
    const = lambda i: (0, 0)
    row = lambda i: (i, 0)
    vmem = 2 * tm * D_MODEL * 4 + D_MODEL * N_PROJ * 2 + 2 * tm * N_PROJ * 4 + 3 * tm * N_PROJ * 4
    return pl.pallas_call(
        _norm_proj_kernel,
        grid=(n // tm,),
        in_specs=[
            pl.BlockSpec((tm, D_MODEL), row),
            pl.BlockSpec((1, D_MODEL), const),
            pl.BlockSpec((D_MODEL, N_PROJ), const, pipeline_mode=pl.Buffered(1)),
        ],
        out_specs=[
            pl.BlockSpec((tm, N_RKV), row),
            pl.BlockSpec((tm, N_RET), row),
            pl.BlockSpec((tm, N_LORA), row),
        ],
        out_shape=[
            jax.ShapeDtypeStruct((n, N_RKV), F32),
            jax.ShapeDtypeStruct((n, N_RET), F32),
            jax.ShapeDtypeStruct((n, N_LORA), F32),
        ],
        compiler_params=pltpu.CompilerParams(
            dimension_semantics=("arbitrary",), vmem_limit_bytes=_vmem_limit(vmem)),
        name="norm_proj",
    )(x2, g, w_ext)


(_V_MU_R, _V_MU_K, _V_MU_V, _V_W0, _V_A0, _V_KK, _V_KA, _V_RK, _V_LNW, _V_LNB) = range(10)
_N_VEC_ROWS = 16


def _rwkv_kernel(prkv_ref, prkv_prev_ref, plora_ref, plora_prev_ref, vec_ref, w2_ref, seg_ref,
                 tril_ref, out_ref, h_ref):
    C, G, TT, STG = RWKV_CHUNK, RWKV_GROUP_HEADS, RWKV_TILE, RWKV_STAGE
    HD = RWKV_HEAD_DIM
    W = G * HD
    GC = G * C
    n_groups = RWKV_WIDTH // W
    t_idx = pl.program_id(1)

    @pl.when(t_idx == 0)
    def _():
        h_ref[...] = jnp.zeros_like(h_ref)

    vec = lambda i: vec_ref[i:i + 1, :]
    first = t_idx == 0

    prkv = prkv_ref[...]
    srkv = _shift_rows(prkv, jnp.where(first, 0.0, prkv_prev_ref[SUBLANES - 1:SUBLANES, :]))
    lerp = lambda j, mu: (prkv[:, j * RWKV_WIDTH:(j + 1) * RWKV_WIDTH]
                          + (srkv[:, j * RWKV_WIDTH:(j + 1) * RWKV_WIDTH]
                             - prkv[:, j * RWKV_WIDTH:(j + 1) * RWKV_WIDTH]) * mu)
    r = lerp(0, vec(_V_MU_R))
    k = lerp(1, vec(_V_MU_K))
    v = lerp(2, vec(_V_MU_V))

    plora = plora_ref[...]
    slora = _shift_rows(plora, jnp.where(first, 0.0, plora_prev_ref[SUBLANES - 1:SUBLANES, :]))
    low = plora[:, 0:LORA_WIDTH] + slora[:, LORA_WIDTH:2 * LORA_WIDTH]
    seg = seg_ref[...]

    def head_sum(x):
        n_blk = x.shape[1] // LANES
        n_rows = x.shape[0]
        rows = jnp.concatenate([x[:, j * LANES:(j + 1) * LANES] for j in range(n_blk)], axis=0)
        s = _split_dot_right(rows, seg)
        return jnp.concatenate([s[j * n_rows:(j + 1) * n_rows] for j in range(n_blk)], axis=1)

    def prep(rows):
        lo_s, r_s, k_s, v_s = low[rows], r[rows], k[rows], v[rows]
        lane = lax.broadcasted_iota(jnp.int32, lo_s.shape, 1)
        act = jnp.where(lane < DECAY_LORA, jnp.tanh(lo_s),
                        jnp.where(lane < DECAY_LORA + AAA_LORA, lo_s, jax.nn.sigmoid(lo_s)))
        second = _dot(act.astype(BF16), w2_ref[...])
        ld = -math.exp(-0.5) * jax.nn.sigmoid(second[:, 0:RWKV_WIDTH] + vec(_V_W0))
        a = jax.nn.sigmoid(second[:, RWKV_WIDTH:2 * RWKV_WIDTH] + vec(_V_A0))
        kk = k_s * vec(_V_KK)
        kk = kk * lax.rsqrt(jnp.maximum(head_sum(kk * kk), 1e-24))
        k2 = k_s * (1.0 + (a - 1.0) * vec(_V_KA))
        b = kk * a
        cums = _split_dot(tril_ref[...], ld)
        n_rows = lo_s.shape[0]
        cum = cums[0:n_rows]
        cum_end = cums[n_rows:]
        e_neg = jnp.exp(-cum)
        e_end = jnp.exp(cum_end - cum)
        return dict(a_t=-kk * jnp.exp(cum - ld), r_t=r_s * jnp.exp(cum), b_t=b * e_neg, k_t=k2 * e_neg,
                    b_w=b * e_end, k_w=k2 * e_end, w_c=jnp.exp(cum_end), v=v_s,
                    bonus=head_sum(r_s * k2 * vec(_V_RK)) * v_s,
                    gate=second[:, 2 * RWKV_WIDTH:3 * RWKV_WIDTH])

    c_bits = C.bit_length() - 1
    hd_bits = HD.bit_length() - 1
    assert C == 1 << c_bits and HD == 1 << hd_bits
    sr = lax.broadcasted_iota(jnp.int32, (GC, W), 0)
    sl = lax.broadcasted_iota(jnp.int32, (GC, W), 1)
    stack_mask = (sr >> c_bits) == (sl >> hd_bits)
    assert W == GC
    ti = lax.broadcasted_iota(jnp.int32, (C, GC), 0)
    si = lax.broadcasted_iota(jnp.int32, (C, GC), 1) & (C - 1)
    strict = ti > si
    incl = ti >= si
    eye_c = (ti == si).astype(F32)
    wi = lax.broadcasted_iota(jnp.int32, (W, W), 0)
    wj = lax.broadcasted_iota(jnp.int32, (W, W), 1)
    eye_w = wi == wj

    def stack(x):
        return jnp.where(stack_mask, jnp.concatenate([x] * G, axis=0), 0.0)

    blk = lambda z, c, g: z[c * C:(c + 1) * C, g * W:(g + 1) * W]

    def chains_of(p):
        st = []
        for c in range(STG // C):
            for g in range(n_groups):
                a_c, r_c = blk(p["a_t"], c, g), blk(p["r_t"], c, g)
                st.append(dict(
                    ar=jnp.concatenate([a_c, r_c], axis=0).astype(BF16), r32=r_c,
                    a_s=stack(a_c).astype(BF16),
                    bk_s=jnp.concatenate([stack(blk(p["b_t"], c, g)), stack(blk(p["k_t"], c, g))],
                                         axis=0).astype(BF16),
                    bw_t=stack(blk(p["b_w"], c, g)).T.astype(BF16),
                    kw_t=stack(blk(p["k_w"], c, g)).T.astype(BF16),
                    v_s=stack(blk(p["v"], c, g)).astype(BF16), w_end=blk(p["w_c"], c, g)[0:1, :]))
        for s in st:
            prod = _dot_nt(s["ar"], s["bk_s"])
            s["n"] = jnp.where(strict, prod[0:C, 0:GC], 0.0)
            s["a_ak"] = jnp.where(strict, prod[0:C, GC:], 0.0).astype(BF16)
            s["a_rb"] = jnp.where(incl, prod[C:, 0:GC], 0.0).astype(BF16)
            s["a_rk"] = jnp.where(incl, prod[C:, GC:], 0.0).astype(BF16)
        for s in st:
            res = _dot(jnp.concatenate([s["a_ak"], s["a_rk"], s["kw_t"]], axis=0), s["v_s"])
            s["av_s"] = stack(res[0:C]).astype(BF16)
            s["rkv"] = res[C:2 * C]
            s["kwv"] = res[2 * C:]
        for s in st:
            nb = s["n"].astype(BF16)
            s["t"] = eye_c + s["n"]
            s["p"] = _dot(nb, stack(s["n"]).astype(BF16))
        for i in range(1, 6):
            for s in st:
                p_s = stack(s["p"]).astype(BF16)
                if i < 5:
                    res = _dot(jnp.concatenate([s["t"], s["p"]], axis=0).astype(BF16), p_s)
                    s["t"] = s["t"] + res[0:C]
                    s["p"] = res[C:]
                else:
                    s["t"] = (s["t"] + _dot(s["t"].astype(BF16), p_s)).astype(BF16)
        for s in st:
            s["x1_s"] = stack(_dot(s["t"], s["a_s"])).astype(BF16)
            s["x2_s"] = stack(_dot(s["t"], s["av_s"])).astype(BF16)
        for s in st:
            lhs = jnp.concatenate([s["a_rb"], s["bw_t"]], axis=0)
            o1 = _dot(lhs, s["x1_s"])
            o2 = _dot(lhs, s["x2_s"])
            s["q"] = (s["r32"] + o1[0:C]).astype(BF16)
            s["m"] = (o1[C:] + jnp.where(eye_w, s["w_end"], 0.0)).astype(BF16)
            s["y0"] = o2[0:C] + s["rkv"]
            s["g"] = o2[C:] + s["kwv"]
        return st

    def state_steps(st, h_cur):
        y_rows = []
        for c in range(STG // C):
            y_lanes = []
            for g in range(n_groups):
                s = st[c * n_groups + g]
                res = _dot(jnp.concatenate([s["q"], s["m"]], axis=0), h_cur[g].astype(BF16))
                y_lanes.append(res[0:C] + s["y0"])
                h_cur[g] = res[C:] + s["g"]
            y_rows.append(jnp.concatenate(y_lanes, axis=1))
        return jnp.concatenate(y_rows, axis=0)

    def finish(rows, y, p):
        inv_hd = 1.0 / HD
        mu = head_sum(y) * inv_hd
        d = y - mu
        var = head_sum(d * d) * inv_hd
        yn = d * lax.rsqrt(var + RWKV_GN_EPS) * vec(_V_LNW) + vec(_V_LNB)
        out_ref[rows, :] = ((yn + p["bonus"]) * p["gate"]).astype(BF16)

    h_cur = [h_ref[g] for g in range(n_groups)]
    stage_rows = [slice(i * STG, (i + 1) * STG) for i in range(TT // STG)]
    preps = [prep(stage_rows[0])]
    chains = [chains_of(preps[0])]
    for i, rows in enumerate(stage_rows):
        if i + 1 < len(stage_rows):
            preps.append(prep(stage_rows[i + 1]))
        y = state_steps(chains[i], h_cur)
        if i + 1 < len(stage_rows):
            chains.append(chains_of(preps[i + 1]))
        finish(rows, y, preps[i])
    for g in range(n_groups):
        h_ref[g] = h_cur[g]


def _rwkv(p_rkv, p_lora, vecs, w2cat, seg, tril, bsz, seq):
    tt = RWKV_TILE
    nt = seq // tt
    w = RWKV_GROUP_HEADS * RWKV_HEAD_DIM
    n_groups = RWKV_WIDTH // w
    row = lambda b, t: (b * nt + t, 0)
    prev = lambda b, t: (jnp.maximum((b * nt + t) * (tt // SUBLANES) - 1, 0), 0)
    const = lambda b, t: (0, 0)
    return pl.pallas_call(
        _rwkv_kernel,
        grid=(bsz, nt),
        in_specs=[
            pl.BlockSpec((tt, N_RKV), row),
            pl.BlockSpec((SUBLANES, N_RKV), prev),
            pl.BlockSpec((tt, N_LORA), row),
            pl.BlockSpec((SUBLANES, N_LORA), prev),
            pl.BlockSpec((_N_VEC_ROWS, RWKV_WIDTH), const),
            pl.BlockSpec((LORA_WIDTH, 3 * RWKV_WIDTH), const),
            pl.BlockSpec((LANES, LANES), const),
            pl.BlockSpec((2 * RWKV_STAGE, RWKV_STAGE), const),
        ],
        out_specs=pl.BlockSpec((tt, RWKV_WIDTH), row),
        out_shape=jax.ShapeDtypeStruct((bsz * seq, RWKV_WIDTH), BF16),
        scratch_shapes=[pltpu.VMEM((n_groups, w, w), F32)],
        compiler_params=pltpu.CompilerParams(
            dimension_semantics=("arbitrary", "arbitrary"), vmem_limit_bytes=_vmem_limit(48 * 1024 * 1024)),
        name="rwkv",
    )(p_rkv, p_rkv, p_lora, p_lora, vecs, w2cat, seg, tril)


def _ret_tables_kernel(freq_ref, cos_ref, sin_ref, dmask_ref):
    C, HD = RET_CHUNK, RET_HEAD_DIM
    n = cos_ref.shape[0]
    ang = lax.broadcasted_iota(jnp.int32, (n, HD), 0).astype(F32) * freq_ref[...]
    lane = lax.broadcasted_iota(jnp.int32, (n, HD), 1)
    sin = jnp.sin(ang)
    cos_ref[...] = jnp.cos(ang)
    sin_ref[...] = jnp.where(lane < HD // 2, -sin, sin)
    ii = lax.broadcasted_iota(jnp.int32, (C, C), 0)
    jj = lax.broadcasted_iota(jnp.int32, (C, C), 1)
    diff = jnp.maximum((ii - jj).astype(F32), 0.0)
    for h in range(RET_HEADS):
        dmask_ref[h] = jnp.where(ii >= jj, jnp.exp(diff * _LOG_GAMMA[h]), 0.0)


def _ret_tables(freq, seq):
    return pl.pallas_call(
        _ret_tables_kernel,
        out_shape=[jax.ShapeDtypeStruct((seq, RET_HEAD_DIM), F32),
                   jax.ShapeDtypeStruct((seq, RET_HEAD_DIM), F32),
                   jax.ShapeDtypeStruct((RET_HEADS, RET_CHUNK, RET_CHUNK), F32)],
        name="ret_tables",
    )(freq)


def _retention_kernel(p_ref, cos_ref, sin_ref, dmask_ref, gnw_ref, out_ref, state_ref):
    C, HD = RET_CHUNK, RET_HEAD_DIM

    @pl.when(pl.program_id(1) == 0)
    def _():
        state_ref[...] = jnp.zeros_like(state_ref)

    row_i = lax.broadcasted_iota(jnp.int32, (C, 1), 0).astype(F32)
    for h in range(RET_HEADS):
        lg = _LOG_GAMMA[h]
        zeta = jnp.exp((C - 1.0 - row_i) * lg)
        xi = jnp.exp((row_i + 1.0) * lg)
        state = state_ref[h]
        for c in range(RET_TILE // C):
            rows = slice(c * C, (c + 1) * C)
            cos, sin = cos_ref[rows, :], sin_ref[rows, :]
            rot = lambda xh: xh * cos + pltpu.roll(xh, HD // 2, 1) * sin
            sl = lambda j: slice(j * RET_WIDTH + h * HD, j * RET_WIDTH + (h + 1) * HD)
            q = rot(p_ref[rows, sl(0)])
            k = rot(p_ref[rows, sl(1)]) * (HD ** -0.5)
            v = p_ref[rows, sl(2)]
            gt = p_ref[rows, sl(3)]
            qb, kb, vb = q.astype(BF16), k.astype(BF16), v.astype(BF16)
            scores = _dot_nt(qb, kb) * dmask_ref[h]
            intra = _dot(scores.astype(BF16), vb)
            kv = _dot_tn(kb, (v * zeta).astype(BF16))
            inter = _dot((q * xi).astype(BF16), state.astype(BF16))
            state = state * _GAMMA_C[h] + kv
            y = intra + inter
            mu = jnp.mean(y, axis=-1, keepdims=True)
            d = y - mu
            var = jnp.mean(d * d, axis=-1, keepdims=True)
            yn = d * lax.rsqrt(var + RET_GN_EPS) * gnw_ref[:, h * HD:(h + 1) * HD]
            out_ref[rows, h * HD:(h + 1) * HD] = (gt * jax.nn.sigmoid(gt) * yn).astype(BF16)
        state_ref[h] = state


_LOG_GAMMA = [math.log(1.0 - 2.0 ** (-5.0 - h)) for h in range(RET_HEADS)]
_GAMMA_C = [math.exp(RET_CHUNK * lg) for lg in _LOG_GAMMA]


def _retention(p_ret, freq, gn_w, bsz, seq):
    tt = RET_TILE
    nt = seq // tt
    cos, sin, dmask = _ret_tables(freq, seq)
    row = lambda b, i: (b * nt + i, 0)
    pos = lambda b, i: (i, 0)
    return pl.pallas_call(
        _retention_kernel,
        grid=(bsz, nt),
        in_specs=[
            pl.BlockSpec((tt, N_RET), row),
            pl.BlockSpec((tt, RET_HEAD_DIM), pos),
            pl.BlockSpec((tt, RET_HEAD_DIM), pos),
            pl.BlockSpec((RET_HEADS, RET_CHUNK, RET_CHUNK), lambda b, i: (0, 0, 0)),
            pl.BlockSpec((1, RET_WIDTH), lambda b, i: (0, 0)),
        ],
        out_specs=pl.BlockSpec((tt, RET_WIDTH), row),
        out_shape=jax.ShapeDtypeStruct((bsz * seq, RET_WIDTH), BF16),
        scratch_shapes=[pltpu.VMEM((RET_HEADS, RET_HEAD_DIM, RET_HEAD_DIM), F32)],
        compiler_params=pltpu.CompilerParams(dimension_semantics=("arbitrary", "arbitrary")),
        name="retention",
    )(p_ret, cos, sin, dmask, gn_w)


def _ffn_kernel(yr_ref, yt_ref, x_ref, wo_ref, gf_ref, wg_ref, wu_ref, cw_ref, cb_ref, wd_ref, gl_ref,
                out_ref, carry_ref):
    tm = FFN_TILE
    t_idx = pl.program_id(1)

    @pl.when(t_idx == 0)
    def _():
        carry_ref[...] = jnp.zeros_like(carry_ref)

    mix = _dot(yr_ref[...], wo_ref[0:RWKV_WIDTH, :]) + _dot(yt_ref[...], wo_ref[RWKV_WIDTH:, :])
    x1 = x_ref[...] + mix
    hb = _rms_norm(x1, gf_ref[...]).astype(BF16)

    cols = D_FF // FFN_COL_SPLIT
    acc = x1
    for j in range(FFN_COL_SPLIT):
        cs = slice(j * cols, (j + 1) * cols)
        gate = _dot(hb, wg_ref[:, cs])
        up = _dot(hb, wu_ref[:, cs])
        prev = carry_ref[:, cs]
        row = lax.broadcasted_iota(jnp.int32, gate.shape, 0)
        g1 = jnp.where(row == 0, prev[SUBLANES - 1:SUBLANES], pltpu.roll(gate, 1, 0))
        g2 = jnp.where(row == 0, prev[SUBLANES - 2:SUBLANES - 1],
                       jnp.where(row == 1, prev[SUBLANES - 1:SUBLANES], pltpu.roll(gate, 2, 0)))
        carry_ref[:, cs] = gate[tm - SUBLANES:tm]
        conv = cw_ref[0:1, cs] * g2 + cw_ref[1:2, cs] * g1 + cw_ref[2:3, cs] * gate + cb_ref[:, cs]
        hidden = (conv * jax.nn.sigmoid(conv) * up).astype(BF16)
        acc = acc + _dot(hidden, wd_ref[cs, :])
    out_ref[...] = _rms_norm(acc, gl_ref[...])


def _ffn(y_rwkv, y_ret, x2, w_out, g_ffn, w_gate, w_up, conv_w, conv_b, w_down, g_final, bsz, seq):
    tm = FFN_TILE
    nt = seq // tm
    row = lambda b, t: (b * nt + t, 0)
    const = lambda b, t: (0, 0)
    single = dict(pipeline_mode=pl.Buffered(1))
    weights = (D_MODEL * D_MODEL + 3 * D_MODEL * D_FF) * 2
    vmem = weights + 4 * tm * D_MODEL * 4 + 8 * tm * D_FF * 4 + 8 * 1024 * 1024
    return pl.pallas_call(
        _ffn_kernel,
        grid=(bsz, nt),
        in_specs=[
            pl.BlockSpec((tm, RWKV_WIDTH), row),
            pl.BlockSpec((tm, RET_WIDTH), row),
            pl.BlockSpec((tm, D_MODEL), row),
            pl.BlockSpec((D_MODEL, D_MODEL), const, **single),
            pl.BlockSpec((1, D_MODEL), const),
            pl.BlockSpec((D_MODEL, D_FF), const, **single),
            pl.BlockSpec((D_MODEL, D_FF), const, **single),
            pl.BlockSpec((SUBLANES, D_FF), const),
            pl.BlockSpec((1, D_FF), const),
            pl.BlockSpec((D_FF, D_MODEL), const, **single),
            pl.BlockSpec((1, D_MODEL), const),
        ],
        out_specs=pl.BlockSpec((tm, D_MODEL), row),
        out_shape=jax.ShapeDtypeStruct((bsz * seq, D_MODEL), F32),
        scratch_shapes=[pltpu.VMEM((SUBLANES, D_FF), F32)],
        compiler_params=pltpu.CompilerParams(
            dimension_semantics=("arbitrary", "arbitrary"), vmem_limit_bytes=_vmem_limit(vmem)),
        name="ffn",
    )(y_rwkv, y_ret, x2, w_out, g_ffn, w_gate, w_up, conv_w, conv_b, w_down, g_final)


def _block_ones(n, block):
    i = jnp.arange(n)
    return ((i[:, None] // block) == (i[None, :] // block)).astype(BF16)


def kernel(x, norm_mix_g, w_in, rwkv_mu_r, rwkv_mu_k, rwkv_mu_v, rwkv_mu_w, rwkv_mu_a, rwkv_mu_g, rwkv_w0, rwkv_w1, rwkv_w2, rwkv_a0, rwkv_a1, rwkv_a2, rwkv_g1, rwkv_g2, rwkv_k_k, rwkv_k_a, rwkv_r_k, rwkv_lnx_w, rwkv_lnx_b, ret_gn_w, w_out, norm_ffn_g, ffn_w_gate, ffn_w_up, ffn_conv_w, ffn_conv_b, ffn_w_down, norm_final_g):
    bsz, seq, d = x.shape
    assert d == D_MODEL and all(seq % t == 0 for t in (PROJ_TILE, RWKV_TILE, RET_TILE, FFN_TILE))
    assert norm_mix_g.shape[0] == 1, "one layer"
    x2 = x.reshape(bsz * seq, d)
    row = lambda p: p.reshape(1, -1)

    w_cat = jnp.concatenate([rwkv_w1[0], rwkv_a1[0], rwkv_g1[0]], axis=1)
    mu_cat = jnp.concatenate([
        jnp.broadcast_to(rwkv_mu_w[0][:, None], (d, DECAY_LORA)),
        jnp.broadcast_to(rwkv_mu_a[0][:, None], (d, AAA_LORA)),
        jnp.broadcast_to(rwkv_mu_g[0][:, None], (d, GATE_LORA))], axis=1)
    w_ext = jnp.concatenate([w_in[0].astype(BF16), _fold_lora(w_cat, mu_cat)], axis=1)
    p_rkv, p_ret, p_lora = _norm_proj(x2, row(norm_mix_g[0]), w_ext)

    vec_rows = [rwkv_mu_r[0], rwkv_mu_k[0], rwkv_mu_v[0], rwkv_w0[0], rwkv_a0[0], rwkv_k_k[0], rwkv_k_a[0],
                rwkv_r_k[0].reshape(-1), rwkv_lnx_w[0], rwkv_lnx_b[0]]
    vecs = jnp.zeros((_N_VEC_ROWS, RWKV_WIDTH), F32).at[:len(vec_rows)].set(jnp.stack(vec_rows))
    w2cat = jnp.zeros((LORA_WIDTH, 3 * RWKV_WIDTH), BF16)
    w2cat = w2cat.at[0:DECAY_LORA, 0:RWKV_WIDTH].set(rwkv_w2[0].astype(BF16))
    w2cat = w2cat.at[DECAY_LORA:DECAY_LORA + AAA_LORA, RWKV_WIDTH:2 * RWKV_WIDTH].set(rwkv_a2[0].astype(BF16))
    w2cat = w2cat.at[DECAY_LORA + AAA_LORA:, 2 * RWKV_WIDTH:].set(rwkv_g2[0].astype(BF16))
    seg = _block_ones(LANES, RWKV_HEAD_DIM)
    ti = jnp.arange(RWKV_STAGE)
    cones = _block_ones(RWKV_STAGE, RWKV_CHUNK)
    tril = (cones.astype(F32) * (ti[:, None] >= ti[None, :])).astype(BF16)
    y_rwkv = _rwkv(p_rkv, p_lora, vecs, w2cat, seg, jnp.concatenate([tril, cones], axis=0), bsz, seq)

    half = RET_HEAD_DIM // 2
    inv_freq = ROPE_BASE ** (-jnp.arange(half, dtype=F32) / half)
    freq = jnp.concatenate([inv_freq, inv_freq]).reshape(1, RET_HEAD_DIM)
    y_ret = _retention(p_ret, freq, row(ret_gn_w[0]), bsz, seq)

    conv_w = jnp.zeros((SUBLANES, D_FF), F32).at[0:3].set(ffn_conv_w[0][:, 0, :])
    out = _ffn(y_rwkv, y_ret, x2, w_out[0].astype(BF16), row(norm_ffn_g[0]), ffn_w_gate[0].astype(BF16),
               ffn_w_up[0].astype(BF16), conv_w, row(ffn_conv_b[0]), ffn_w_down[0].astype(BF16),
               row(norm_final_g), bsz, seq)
    return out.reshape(bsz, seq, d)
```

```python
import math

import jax
import jax.numpy as jnp
from jax import lax
from jax.experimental import pallas as pl
from jax.experimental.pallas import tpu as pltpu

F32 = jnp.float32
BF16 = jnp.bfloat16

D_MODEL = 1024
RWKV_HEADS = 8
RWKV_HEAD_DIM = 64
RWKV_WIDTH = 512
RET_HEADS = 4
RET_HEAD_DIM = 128
RET_WIDTH = 512
DECAY_LORA = 64
AAA_LORA = 64
GATE_LORA = 128
LORA_WIDTH = DECAY_LORA + AAA_LORA + GATE_LORA
RET_CHUNK = 128
ROPE_BASE = 10000.0
D_FF = 2816
NORM_EPS = 1e-6
RWKV_GN_EPS = 64e-5
RET_GN_EPS = 1e-5

V7X_VMEM_BYTES = 64 * 1024 * 1024
SUBLANES = 8
LANES = 128

RWKV_CHUNK = 64
RWKV_GROUP_HEADS = 2
RWKV_TILE = 512
RWKV_STAGE = 256
PROJ_TILE = 512
FFN_TILE = 512
FFN_STAGES = 2
FFN_COL_SPLIT = 2


def _vmem_limit(nbytes):
    return int(min(nbytes, V7X_VMEM_BYTES - 4 * 1024 * 1024))


def _dot(a, b):
    return jnp.dot(a, b, preferred_element_type=F32)


def _dot_nt(a, b):
    return lax.dot_general(a, b, (((1,), (1,)), ((), ())), preferred_element_type=F32)


def _dot_tn(a, b):
    return lax.dot_general(a, b, (((0,), (0,)), ((), ())), preferred_element_type=F32)


def _split_dot(mat_bf16, x):
    hi = x.astype(BF16)
    lo = (x - hi.astype(F32)).astype(BF16)
    return _dot(mat_bf16, hi) + _dot(mat_bf16, lo)


def _split_dot_right(x, mat_bf16):
    hi = x.astype(BF16)
    lo = (x - hi.astype(F32)).astype(BF16)
    return _dot(hi, mat_bf16) + _dot(lo, mat_bf16)


def _shift_rows(x, prev_row):
    rolled = pltpu.roll(x, 1, 0)
    row = lax.broadcasted_iota(jnp.int32, x.shape, 0)
    return jnp.where(row == 0, prev_row, rolled)


def _rms_norm(x, g):
    ms = jnp.mean(x * x, axis=-1, keepdims=True)
    return x * lax.rsqrt(ms + NORM_EPS) * g


def _fold_lora_kernel(w_ref, mu_ref, o_ref):
    w = w_ref[...]
    mu = mu_ref[...]
    o_ref[:, 0:LORA_WIDTH] = (w * (1.0 - mu)).astype(BF16)
    o_ref[:, LORA_WIDTH:2 * LORA_WIDTH] = (w * mu).astype(BF16)


def _fold_lora(w_cat, mu_cat):
    return pl.pallas_call(
        _fold_lora_kernel,
        out_shape=jax.ShapeDtypeStruct((D_MODEL, 2 * LORA_WIDTH), BF16),
        name="fold_lora",
    )(w_cat, mu_cat)


N_RKV = 3 * RWKV_WIDTH
N_RET = 4 * RET_WIDTH
N_LORA = 2 * LORA_WIDTH
N_PROJ = N_RKV + N_RET + N_LORA


def _norm_proj_kernel(x_ref, g_ref, w_ref, rkv_ref, ret_ref, lora_ref):
    xb = _rms_norm(x_ref[...], g_ref[...]).astype(BF16)
    rkv_ref[...] = _dot(xb, w_ref[:, 0:N_RKV])
    ret_ref[...] = _dot(xb, w_ref[:, N_RKV:N_RKV + N_RET])
    lora_ref[...] = _dot(xb, w_ref[:, N_RKV + N_RET:N_PROJ])


def _norm_proj(x2, g, w_ext):
    n = x2.shape[0]
    tm = PROJ_TILE
    const = lambda i: (0, 0)
    row = lambda i: (i, 0)
    vmem = 2 * tm * D_MODEL * 4 + D_MODEL * N_PROJ * 2 + 2 * tm * N_PROJ * 4 + 3 * tm * N_PROJ * 4
    return pl.pallas_call(
        _norm_proj_kernel,
        grid=(n // tm,),
        in_specs=[
            pl.BlockSpec((tm, D_MODEL), row),
            pl.BlockSpec((1, D_MODEL), const),
            pl.BlockSpec((D_MODEL, N_PROJ), const, pipeline_mode=pl.Buffered(1)),
        ],
        out_specs=[
            pl.BlockSpec((tm, N_RKV), row),
            pl.BlockSpec((tm, N_RET), row),
            pl.BlockSpec((tm, N_LORA), row),
        ],
        out_shape=[
            jax.ShapeDtypeStruct((n, N_RKV), F32),
            jax.ShapeDtypeStruct((n, N_RET), F32),
            jax.ShapeDtypeStruct((n, N_LORA), F32),
        ],
        compiler_params=pltpu.CompilerParams(
            dimension_semantics=("arbitrary",), vmem_limit_bytes=_vmem_limit(vmem)),
        name="norm_proj",
    )(x2, g, w_ext)


(_V_MU_R, _V_MU_K, _V_MU_V, _V_W0, _V_A0, _V_KK, _V_KA, _V_RK, _V_LNW, _V_LNB) = range(10)
_N_VEC_ROWS = 16


def _rwkv_kernel(prkv_ref, prkv_prev_ref, plora_ref, plora_prev_ref, vec_ref, w2_ref, seg_ref,
                 tril_ref, pret_ref, cos_ref, sin_ref, dmask_ref, gnw_ref, out_ref, yret_ref,
                 h_ref, rstate_ref):
    C, G, TT, STG = RWKV_CHUNK, RWKV_GROUP_HEADS, RWKV_TILE, RWKV_STAGE
    HD = RWKV_HEAD_DIM
    W = G * HD
    GC = G * C
    n_groups = RWKV_WIDTH // W
    t_idx = pl.program_id(1)

    @pl.when(t_idx == 0)
    def _():
        h_ref[...] = jnp.zeros_like(h_ref)
        rstate_ref[...] = jnp.zeros_like(rstate_ref)

    vec = lambda i: vec_ref[i:i + 1, :]
    first = t_idx == 0

    ret_state = [rstate_ref[h] for h in range(RET_HEADS)]

    def ret_unit(c, h):
        RC, RD = RET_CHUNK, RET_HEAD_DIM
        rows = slice(c * RC, (c + 1) * RC)
        row_i = lax.broadcasted_iota(jnp.int32, (RC, 1), 0).astype(F32)
        lg = _LOG_GAMMA[h]
        cos, sin = cos_ref[rows, :], sin_ref[rows, :]
        rot = lambda xh: xh * cos + pltpu.roll(xh, RD // 2, 1) * sin
        sl = lambda j: slice(j * RET_WIDTH + h * RD, j * RET_WIDTH + (h + 1) * RD)
        q = rot(pret_ref[rows, sl(0)])
        kr = rot(pret_ref[rows, sl(1)]) * (RD ** -0.5)
        vr = pret_ref[rows, sl(2)]
        gt = pret_ref[rows, sl(3)]
        qb, kb, vb = q.astype(BF16), kr.astype(BF16), vr.astype(BF16)
        scores = (_dot_nt(qb, kb) * dmask_ref[h]).astype(BF16)
        kv = _dot_tn(kb, (vr * jnp.exp((RC - 1.0 - row_i) * lg)).astype(BF16))
        inter = _dot((q * jnp.exp((row_i + 1.0) * lg)).astype(BF16), ret_state[h].astype(BF16))
        ret_state[h] = ret_state[h] * _GAMMA_C[h] + kv

        def second_half():
            y = _dot(scores, vb) + inter
            mu = jnp.mean(y, axis=-1, keepdims=True)
            d = y - mu
            var = jnp.mean(d * d, axis=-1, keepdims=True)
            yn = d * lax.rsqrt(var + RET_GN_EPS) * gnw_ref[:, h * RD:(h + 1) * RD]
            yret_ref[rows, h * RD:(h + 1) * RD] = (gt * jax.nn.sigmoid(gt) * yn).astype(BF16)
        return second_half

    ret_units = [(c, h) for c in range(TT // RET_CHUNK) for h in range(RET_HEADS)]
    ret_pending = []

    def ret_fill(n=1):
        for _ in range(n):
            if ret_pending:
                ret_pending.pop(0)()
            if ret_units:
                ret_pending.append(ret_unit(*ret_units.pop(0)))

    prkv = prkv_ref[...]
    srkv = _shift_rows(prkv, jnp.where(first, 0.0, prkv_prev_ref[SUBLANES - 1:SUBLANES, :]))
    lerp = lambda j, mu: (prkv[:, j * RWKV_WIDTH:(j + 1) * RWKV_WIDTH]
                          + (srkv[:, j * RWKV_WIDTH:(j + 1) * RWKV_WIDTH]
                             - prkv[:, j * RWKV_WIDTH:(j + 1) * RWKV_WIDTH]) * mu)
    r = lerp(0, vec(_V_MU_R))
    k = lerp(1, vec(_V_MU_K))
    v = lerp(2, vec(_V_MU_V))

    plora = plora_ref[...]
    slora = _shift_rows(plora, jnp.where(first, 0.0, plora_prev_ref[SUBLANES - 1:SUBLANES, :]))
    low = plora[:, 0:LORA_WIDTH] + slora[:, LORA_WIDTH:2 * LORA_WIDTH]
    seg = seg_ref[...]

    def head_sum(x):
        n_blk = x.shape[1] // LANES
        n_rows = x.shape[0]
        rows = jnp.concatenate([x[:, j * LANES:(j + 1) * LANES] for j in range(n_blk)], axis=0)
        s = _split_dot_right(rows, seg)
        return jnp.concatenate([s[j * n_rows:(j + 1) * n_rows] for j in range(n_blk)], axis=1)

    def prep(rows):
        lo_s, r_s, k_s, v_s = low[rows], r[rows], k[rows], v[rows]
        lane = lax.broadcasted_iota(jnp.int32, lo_s.shape, 1)
        act = jnp.where(lane < DECAY_LORA, jnp.tanh(lo_s),
                        jnp.where(lane < DECAY_LORA + AAA_LORA, lo_s, jax.nn.sigmoid(lo_s)))
        second = _dot(act.astype(BF16), w2_ref[...])
        ld = -math.exp(-0.5) * jax.nn.sigmoid(second[:, 0:RWKV_WIDTH] + vec(_V_W0))
        a = jax.nn.sigmoid(second[:, RWKV_WIDTH:2 * RWKV_WIDTH] + vec(_V_A0))
        kk = k_s * vec(_V_KK)
        kk = kk * lax.rsqrt(jnp.maximum(head_sum(kk * kk), 1e-24))
        k2 = k_s * (1.0 + (a - 1.0) * vec(_V_KA))
        b = kk * a
        cums = _split_dot(tril_ref[...], ld)
        n_rows = lo_s.shape[0]
        cum = cums[0:n_rows]
        cum_end = cums[n_rows:]
        e_neg = jnp.exp(-cum)
        e_end = jnp.exp(cum_end - cum)
        return dict(a_t=-kk * jnp.exp(cum - ld), r_t=r_s * jnp.exp(cum), b_t=b * e_neg, k_t=k2 * e_neg,
                    b_w=b * e_end, k_w=k2 * e_end, w_c=jnp.exp(cum_end), v=v_s,
                    bonus=head_sum(r_s * k2 * vec(_V_RK)) * v_s,
                    gate=second[:, 2 * RWKV_WIDTH:3 * RWKV_WIDTH])

    c_bits = C.bit_length() - 1
    hd_bits = HD.bit_length() - 1
    assert C == 1 << c_bits and HD == 1 << hd_bits
    sr = lax.broadcasted_iota(jnp.int32, (GC, W), 0)
    sl = lax.broadcasted_iota(jnp.int32, (GC, W), 1)
    stack_mask = (sr >> c_bits) == (sl >> hd_bits)
    assert W == GC
    ti = lax.broadcasted_iota(jnp.int32, (C, GC), 0)
    si = lax.broadcasted_iota(jnp.int32, (C, GC), 1) & (C - 1)
    strict = ti > si
    incl = ti >= si
    eye_c = (ti == si).astype(F32)
    wi = lax.broadcasted_iota(jnp.int32, (W, W), 0)
    wj = lax.broadcasted_iota(jnp.int32, (W, W), 1)
    eye_w = wi == wj

    def stack(x):
        return jnp.where(stack_mask, jnp.concatenate([x] * G, axis=0), 0.0)

    blk = lambda z, c, g: z[c * C:(c + 1) * C, g * W:(g + 1) * W]

    def chains_of(p):
        st = []
        for c in range(STG // C):
            for g in range(n_groups):
                a_c, r_c = blk(p["a_t"], c, g), blk(p["r_t"], c, g)
                st.append(dict(
                    ar=jnp.concatenate([a_c, r_c], axis=0).astype(BF16), r32=r_c,
                    a_s=stack(a_c).astype(BF16),
                    bk_s=jnp.concatenate([stack(blk(p["b_t"], c, g)), stack(blk(p["k_t"], c, g))],
                                         axis=0).astype(BF16),
                    bw_t=stack(blk(p["b_w"], c, g)).T.astype(BF16),
                    kw_t=stack(blk(p["k_w"], c, g)).T.astype(BF16),
                    v_s=stack(blk(p["v"], c, g)).astype(BF16), w_end=blk(p["w_c"], c, g)[0:1, :]))
        for s in st:
            prod = _dot_nt(s["ar"], s["bk_s"])
            s["n"] = jnp.where(strict, prod[0:C, 0:GC], 0.0)
            s["a_ak"] = jnp.where(strict, prod[0:C, GC:], 0.0).astype(BF16)
            s["a_rb"] = jnp.where(incl, prod[C:, 0:GC], 0.0).astype(BF16)
            s["a_rk"] = jnp.where(incl, prod[C:, GC:], 0.0).astype(BF16)
        ret_fill()
        for s in st:
            res = _dot(jnp.concatenate([s["a_ak"], s["a_rk"], s["kw_t"]], axis=0), s["v_s"])
            s["av_s"] = stack(res[0:C]).astype(BF16)
            s["rkv"] = res[C:2 * C]
            s["kwv"] = res[2 * C:]
        ret_fill()
        for s in st:
            nb = s["n"].astype(BF16)
            s["t"] = eye_c + s["n"]
            s["p"] = _dot(nb, stack(s["n"]).astype(BF16))
        ret_fill()
        for i in range(1, 6):
            for s in st:
                p_s = stack(s["p"]).astype(BF16)
                if i < 5:
                    res = _dot(jnp.concatenate([s["t"], s["p"]], axis=0).astype(BF16), p_s)
                    s["t"] = s["t"] + res[0:C]
                    s["p"] = res[C:]
                else:
                    s["t"] = (s["t"] + _dot(s["t"].astype(BF16), p_s)).astype(BF16)
            ret_fill()
        for s in st:
            s["x1_s"] = stack(_dot(s["t"], s["a_s"])).astype(BF16)
            s["x2_s"] = stack(_dot(s["t"], s["av_s"])).astype(BF16)
        for s in st:
            lhs = jnp.concatenate([s["a_rb"], s["bw_t"]], axis=0)
            o1 = _dot(lhs, s["x1_s"])
            o2 = _dot(lhs, s["x2_s"])
            s["q"] = (s["r32"] + o1[0:C]).astype(BF16)
            s["m"] = (o1[C:] + jnp.where(eye_w, s["w_end"], 0.0)).astype(BF16)
            s["y0"] = o2[0:C] + s["rkv"]
            s["g"] = o2[C:] + s["kwv"]
        return st

    def state_steps(st, h_cur):
        y_rows = []
        for c in range(STG // C):
            y_lanes = []
            for g in range(n_groups):
                s = st[c * n_groups + g]
                res = _dot(jnp.concatenate([s["q"], s["m"]], axis=0), h_cur[g].astype(BF16))
                y_lanes.append(res[0:C] + s["y0"])
                h_cur[g] = res[C:] + s["g"]
            y_rows.append(jnp.concatenate(y_lanes, axis=1))
        return jnp.concatenate(y_rows, axis=0)

    def finish(rows, y, p):
        inv_hd = 1.0 / HD
        mu = head_sum(y) * inv_hd
        d = y - mu
        var = head_sum(d * d) * inv_hd
        yn = d * lax.rsqrt(var + RWKV_GN_EPS) * vec(_V_LNW) + vec(_V_LNB)
        out_ref[rows, :] = ((yn + p["bonus"]) * p["gate"]).astype(BF16)

    h_cur = [h_ref[g] for g in range(n_groups)]
    stage_rows = [slice(i * STG, (i + 1) * STG) for i in range(TT // STG)]
    preps = [prep(stage_rows[0])]
    chains = [chains_of(preps[0])]
    for i, rows in enumerate(stage_rows):
        if i + 1 < len(stage_rows):
            preps.append(prep(stage_rows[i + 1]))
        y = state_steps(chains[i], h_cur)
        if i + 1 < len(stage_rows):
            chains.append(chains_of(preps[i + 1]))
        finish(rows, y, preps[i])
    ret_fill(len(ret_units) + len(ret_pending))
    for g in range(n_groups):
        h_ref[g] = h_cur[g]
    for h in range(RET_HEADS):
        rstate_ref[h] = ret_state[h]


def _mixer(p_rkv, p_lora, vecs, w2cat, seg, tril, p_ret, freq, gn_w, bsz, seq):
    tt = RWKV_TILE
    nt = seq // tt
    w = RWKV_GROUP_HEADS * RWKV_HEAD_DIM
    n_groups = RWKV_WIDTH // w
    cos, sin, dmask = _ret_tables(freq, seq)
    row = lambda b, t: (b * nt + t, 0)
    prev = lambda b, t: (jnp.maximum((b * nt + t) * (tt // SUBLANES) - 1, 0), 0)
    pos = lambda b, t: (t, 0)
    const = lambda b, t: (0, 0)
    return pl.pallas_call(
        _rwkv_kernel,
        grid=(bsz, nt),
        in_specs=[
            pl.BlockSpec((tt, N_RKV), row),
            pl.BlockSpec((SUBLANES, N_RKV), prev),
            pl.BlockSpec((tt, N_LORA), row),
            pl.BlockSpec((SUBLANES, N_LORA), prev),
            pl.BlockSpec((_N_VEC_ROWS, RWKV_WIDTH), const),
            pl.BlockSpec((LORA_WIDTH, 3 * RWKV_WIDTH), const),
            pl.BlockSpec((LANES, LANES), const),
            pl.BlockSpec((2 * RWKV_STAGE, RWKV_STAGE), const),
            pl.BlockSpec((tt, N_RET), row),
            pl.BlockSpec((tt, RET_HEAD_DIM), pos),
            pl.BlockSpec((tt, RET_HEAD_DIM), pos),
            pl.BlockSpec((RET_HEADS, RET_CHUNK, RET_CHUNK), lambda b, t: (0, 0, 0)),
            pl.BlockSpec((1, RET_WIDTH), const),
        ],
        out_specs=[pl.BlockSpec((tt, RWKV_WIDTH), row), pl.BlockSpec((tt, RET_WIDTH), row)],
        out_shape=[jax.ShapeDtypeStruct((bsz * seq, RWKV_WIDTH), BF16),
                   jax.ShapeDtypeStruct((bsz * seq, RET_WIDTH), BF16)],
        scratch_shapes=[pltpu.VMEM((n_groups, w, w), F32),
                        pltpu.VMEM((RET_HEADS, RET_HEAD_DIM, RET_HEAD_DIM), F32)],
        compiler_params=pltpu.CompilerParams(
            dimension_semantics=("arbitrary", "arbitrary"), vmem_limit_bytes=_vmem_limit(56 * 1024 * 1024)),
        name="mixer",
    )(p_rkv, p_rkv, p_lora, p_lora, vecs, w2cat, seg, tril, p_ret, cos, sin, dmask, gn_w)


def _ret_tables_kernel(freq_ref, cos_ref, sin_ref, dmask_ref):
    C, HD = RET_CHUNK, RET_HEAD_DIM
    n = cos_ref.shape[0]
    ang = lax.broadcasted_iota(jnp.int32, (n, HD), 0).astype(F32) * freq_ref[...]
    lane = lax.broadcasted_iota(jnp.int32, (n, HD), 1)
    sin = jnp.sin(ang)
    cos_ref[...] = jnp.cos(ang)
    sin_ref[...] = jnp.where(lane < HD // 2, -sin, sin)
    ii = lax.broadcasted_iota(jnp.int32, (C, C), 0)
    jj = lax.broadcasted_iota(jnp.int32, (C, C), 1)
    diff = jnp.maximum((ii - jj).astype(F32), 0.0)
    for h in range(RET_HEADS):
        dmask_ref[h] = jnp.where(ii >= jj, jnp.exp(diff * _LOG_GAMMA[h]), 0.0)


def _ret_tables(freq, seq):
    return pl.pallas_call(
        _ret_tables_kernel,
        out_shape=[jax.ShapeDtypeStruct((seq, RET_HEAD_DIM), F32),
                   jax.ShapeDtypeStruct((seq, RET_HEAD_DIM), F32),
                   jax.ShapeDtypeStruct((RET_HEADS, RET_CHUNK, RET_CHUNK), F32)],
        name="ret_tables",
    )(freq)


_LOG_GAMMA = [math.log(1.0 - 2.0 ** (-5.0 - h)) for h in range(RET_HEADS)]
_GAMMA_C = [math.exp(RET_CHUNK * lg) for lg in _LOG_GAMMA]


def _ffn_kernel(yr_ref, yt_ref, x_ref, wo_ref, gf_ref, wg_ref, wu_ref, cw_ref, cb_ref, wd_ref, gl_ref,
                out_ref, carry_ref):
    sm = FFN_TILE // FFN_STAGES
    stages = [slice(i * sm, (i + 1) * sm) for i in range(FFN_STAGES)]

    @pl.when(pl.program_id(1) == 0)
    def _():
        carry_ref[...] = jnp.zeros_like(carry_ref)

    mix = [_dot(yr_ref[s, :], wo_ref[0:RWKV_WIDTH, :]) + _dot(yt_ref[s, :], wo_ref[RWKV_WIDTH:, :])
           for s in stages]
    acc = [x_ref[s, :] + m for s, m in zip(stages, mix)]
    hb = [_rms_norm(x1, gf_ref[...]).astype(BF16) for x1 in acc]

    cols = D_FF // FFN_COL_SPLIT
    for j in range(FFN_COL_SPLIT):
        cs = slice(j * cols, (j + 1) * cols)
        gate_up = [(_dot(h, wg_ref[:, cs]), _dot(h, wu_ref[:, cs])) for h in hb]
        prev = carry_ref[:, cs]
        p2, p1 = prev[SUBLANES - 2:SUBLANES - 1], prev[SUBLANES - 1:SUBLANES]
        hidden = []
        for gate, up in gate_up:
            row = lax.broadcasted_iota(jnp.int32, gate.shape, 0)
            g1 = jnp.where(row == 0, p1, pltpu.roll(gate, 1, 0))
            g2 = jnp.where(row == 0, p2, jnp.where(row == 1, p1, pltpu.roll(gate, 2, 0)))
            p2, p1 = gate[sm - 2:sm - 1], gate[sm - 1:sm]
            conv = cw_ref[0:1, cs] * g2 + cw_ref[1:2, cs] * g1 + cw_ref[2:3, cs] * gate + cb_ref[:, cs]
            hidden.append((conv * jax.nn.sigmoid(conv) * up).astype(BF16))
        carry_ref[:, cs] = gate_up[-1][0][sm - SUBLANES:sm]
        acc = [a + _dot(h, wd_ref[cs, :]) for a, h in zip(acc, hidden)]
    for s, a in zip(stages, acc):
        out_ref[s, :] = _rms_norm(a, gl_ref[...])


def _ffn(y_rwkv, y_ret, x2, w_out, g_ffn, w_gate, w_up, conv_w, conv_b, w_down, g_final, bsz, seq):
    tm = FFN_TILE
    nt = seq // tm
    row = lambda b, t: (b * nt + t, 0)
    const = lambda b, t: (0, 0)
    single = dict(pipeline_mode=pl.Buffered(1))
    weights = (D_MODEL * D_MODEL + 3 * D_MODEL * D_FF) * 2
    vmem = weights + 4 * tm * D_MODEL * 4 + 8 * tm * D_FF * 4 + 8 * 1024 * 1024
    return pl.pallas_call(
        _ffn_kernel,
        grid=(bsz, nt),
        in_specs=[
            pl.BlockSpec((tm, RWKV_WIDTH), row),
            pl.BlockSpec((tm, RET_WIDTH), row),
            pl.BlockSpec((tm, D_MODEL), row),
            pl.BlockSpec((D_MODEL, D_MODEL), const, **single),
            pl.BlockSpec((1, D_MODEL), const),
            pl.BlockSpec((D_MODEL, D_FF), const, **single),
            pl.BlockSpec((D_MODEL, D_FF), const, **single),
            pl.BlockSpec((SUBLANES, D_FF), const),
            pl.BlockSpec((1, D_FF), const),
            pl.BlockSpec((D_FF, D_MODEL), const, **single),
            pl.BlockSpec((1, D_MODEL), const),
        ],
        out_specs=pl.BlockSpec((tm, D_MODEL), row),
        out_shape=jax.ShapeDtypeStruct((bsz * seq, D_MODEL), F32),
        scratch_shapes=[pltpu.VMEM((SUBLANES, D_FF), F32)],
        compiler_params=pltpu.CompilerParams(
            dimension_semantics=("arbitrary", "arbitrary"), vmem_limit_bytes=_vmem_limit(vmem)),
        name="ffn",
    )(y_rwkv, y_ret, x2, w_out, g_ffn, w_gate, w_up, conv_w, conv_b, w_down, g_final)


def _block_ones(n, block):
    i = jnp.arange(n)
    return ((i[:, None] // block) == (i[None, :] // block)).astype(BF16)


def kernel(x, norm_mix_g, w_in, rwkv_mu_r, rwkv_mu_k, rwkv_mu_v, rwkv_mu_w, rwkv_mu_a, rwkv_mu_g, rwkv_w0, rwkv_w1, rwkv_w2, rwkv_a0, rwkv_a1, rwkv_a2, rwkv_g1, rwkv_g2, rwkv_k_k, rwkv_k_a, rwkv_r_k, rwkv_lnx_w, rwkv_lnx_b, ret_gn_w, w_out, norm_ffn_g, ffn_w_gate, ffn_w_up, ffn_conv_w, ffn_conv_b, ffn_w_down, norm_final_g):
    bsz, seq, d = x.shape
    assert d == D_MODEL and all(seq % t == 0 for t in (PROJ_TILE, RWKV_TILE, FFN_TILE))
    assert norm_mix_g.shape[0] == 1, "one layer"
    x2 = x.reshape(bsz * seq, d)
    row = lambda p: p.reshape(1, -1)

    w_cat = jnp.concatenate([rwkv_w1[0], rwkv_a1[0], rwkv_g1[0]], axis=1)
    mu_cat = jnp.concatenate([
        jnp.broadcast_to(rwkv_mu_w[0][:, None], (d, DECAY_LORA)),
        jnp.broadcast_to(rwkv_mu_a[0][:, None], (d, AAA_LORA)),
        jnp.broadcast_to(rwkv_mu_g[0][:, None], (d, GATE_LORA))], axis=1)
    w_ext = jnp.concatenate([w_in[0].astype(BF16), _fold_lora(w_cat, mu_cat)], axis=1)
    p_rkv, p_ret, p_lora = _norm_proj(x2, row(norm_mix_g[0]), w_ext)

    vec_rows = [rwkv_mu_r[0], rwkv_mu_k[0], rwkv_mu_v[0], rwkv_w0[0], rwkv_a0[0], rwkv_k_k[0], rwkv_k_a[0],
                rwkv_r_k[0].reshape(-1), rwkv_lnx_w[0], rwkv_lnx_b[0]]
    vecs = jnp.zeros((_N_VEC_ROWS, RWKV_WIDTH), F32).at[:len(vec_rows)].set(jnp.stack(vec_rows))
    w2cat = jnp.zeros((LORA_WIDTH, 3 * RWKV_WIDTH), BF16)
    w2cat = w2cat.at[0:DECAY_LORA, 0:RWKV_WIDTH].set(rwkv_w2[0].astype(BF16))
    w2cat = w2cat.at[DECAY_LORA:DECAY_LORA + AAA_LORA, RWKV_WIDTH:2 * RWKV_WIDTH].set(rwkv_a2[0].astype(BF16))
    w2cat = w2cat.at[DECAY_LORA + AAA_LORA:, 2 * RWKV_WIDTH:].set(rwkv_g2[0].astype(BF16))
    seg = _block_ones(LANES, RWKV_HEAD_DIM)
    ti = jnp.arange(RWKV_STAGE)
    cones = _block_ones(RWKV_STAGE, RWKV_CHUNK)
    tril = (cones.astype(F32) * (ti[:, None] >= ti[None, :])).astype(BF16)
    half = RET_HEAD_DIM // 2
    inv_freq = ROPE_BASE ** (-jnp.arange(half, dtype=F32) / half)
    freq = jnp.concatenate([inv_freq, inv_freq]).reshape(1, RET_HEAD_DIM)
    y_rwkv, y_ret = _mixer(p_rkv, p_lora, vecs, w2cat, seg, jnp.concatenate([tril, cones], axis=0),
                           p_ret, freq, row(ret_gn_w[0]), bsz, seq)

    conv_w = jnp.zeros((SUBLANES, D_FF), F32).at[0:3].set(ffn_conv_w[0][:, 0, :])
    out = _ffn(y_rwkv, y_ret, x2, w_out[0].astype(BF16), row(norm_ffn_g[0]), ffn_w_gate[0].astype(BF16),
               ffn_w_up[0].astype(BF16), conv_w, row(ffn_conv_b[0]), ffn_w_down[0].astype(BF16),
               row(norm_final_g), bsz, seq)
    return out.reshape(bsz, seq, d)
```

```python
import math

import jax
import jax.numpy as jnp
from jax import lax
from jax.experimental import pallas as pl
from jax.experimental.pallas import tpu as pltpu

F32 = jnp.float32
BF16 = jnp.bfloat16

D_MODEL = 1024
RWKV_HEADS = 8
RWKV_HEAD_DIM = 64
RWKV_WIDTH = 512
RET_HEADS = 4
RET_HEAD_DIM = 128
RET_WIDTH = 512
DECAY_LORA = 64
AAA_LORA = 64
GATE_LORA = 128
LORA_WIDTH = DECAY_LORA + AAA_LORA + GATE_LORA
RET_CHUNK = 128
ROPE_BASE = 10000.0
D_FF = 2816
NORM_EPS = 1e-6
RWKV_GN_EPS = 64e-5
RET_GN_EPS = 1e-5

V7X_VMEM_BYTES = 64 * 1024 * 1024
SUBLANES = 8
LANES = 128

RWKV_CHUNK = 64
RWKV_GROUP_HEADS = 2
MIX_TILE = 512
MIX_STAGE = 256
PROJ_PIECE = 512
FFN_TILE = 512
FFN_STAGES = 2
FFN_COL_SPLIT = 2

N_RKV = 3 * RWKV_WIDTH
N_RET = 4 * RET_WIDTH
N_LORA = 2 * LORA_WIDTH
N_PROJ = N_RKV + N_RET + N_LORA

_LOG_GAMMA = [math.log(1.0 - 2.0 ** (-5.0 - h)) for h in range(RET_HEADS)]
_GAMMA_C = [math.exp(RET_CHUNK * lg) for lg in _LOG_GAMMA]


def _vmem_limit(nbytes):
    return int(min(nbytes, V7X_VMEM_BYTES - 4 * 1024 * 1024))


def _dot(a, b):
    return jnp.dot(a, b, preferred_element_type=F32)


def _dot_nt(a, b):
    return lax.dot_general(a, b, (((1,), (1,)), ((), ())), preferred_element_type=F32)


def _dot_tn(a, b):
    return lax.dot_general(a, b, (((0,), (0,)), ((), ())), preferred_element_type=F32)


def _split_dot(mat_bf16, x):
    hi = x.astype(BF16)
    lo = (x - hi.astype(F32)).astype(BF16)
    return _dot(mat_bf16, hi) + _dot(mat_bf16, lo)


def _split_dot_right(x, mat_bf16):
    hi = x.astype(BF16)
    lo = (x - hi.astype(F32)).astype(BF16)
    return _dot(hi, mat_bf16) + _dot(lo, mat_bf16)


def _shift_rows(x, prev_row):
    rolled = pltpu.roll(x, 1, 0)
    row = lax.broadcasted_iota(jnp.int32, x.shape, 0)
    return jnp.where(row == 0, prev_row, rolled)


def _rms_norm(x, g):
    ms = jnp.mean(x * x, axis=-1, keepdims=True)
    return x * lax.rsqrt(ms + NORM_EPS) * g


def _fold_lora_kernel(w_ref, mu_ref, o_ref):
    w = w_ref[...]
    mu = mu_ref[...]
    o_ref[:, 0:LORA_WIDTH] = (w * (1.0 - mu)).astype(BF16)
    o_ref[:, LORA_WIDTH:2 * LORA_WIDTH] = (w * mu).astype(BF16)


def _fold_lora(w_cat, mu_cat):
    return pl.pallas_call(
        _fold_lora_kernel,
        out_shape=jax.ShapeDtypeStruct((D_MODEL, 2 * LORA_WIDTH), BF16),
        name="fold_lora",
    )(w_cat, mu_cat)


def _ret_tables_kernel(freq_ref, cos_ref, sin_ref, dmask_ref):
    C, HD = RET_CHUNK, RET_HEAD_DIM
    n = cos_ref.shape[0]
    ang = lax.broadcasted_iota(jnp.int32, (n, HD), 0).astype(F32) * freq_ref[...]
    lane = lax.broadcasted_iota(jnp.int32, (n, HD), 1)
    sin = jnp.sin(ang)
    cos_ref[...] = jnp.cos(ang)
    sin_ref[...] = jnp.where(lane < HD // 2, -sin, sin)
    ii = lax.broadcasted_iota(jnp.int32, (C, C), 0)
    jj = lax.broadcasted_iota(jnp.int32, (C, C), 1)
    diff = jnp.maximum((ii - jj).astype(F32), 0.0)
    for h in range(RET_HEADS):
        dmask_ref[h] = jnp.where(ii >= jj, jnp.exp(diff * _LOG_GAMMA[h]), 0.0)


def _ret_tables(freq, seq):
    return pl.pallas_call(
        _ret_tables_kernel,
        out_shape=[jax.ShapeDtypeStruct((seq, RET_HEAD_DIM), F32),
                   jax.ShapeDtypeStruct((seq, RET_HEAD_DIM), F32),
                   jax.ShapeDtypeStruct((RET_HEADS, RET_CHUNK, RET_CHUNK), F32)],
        name="ret_tables",
    )(freq)


(_V_MU_R, _V_MU_K, _V_MU_V, _V_W0, _V_A0, _V_KK, _V_KA, _V_RK, _V_LNW, _V_LNB) = range(10)
_N_VEC_ROWS = 16

_PREP_NAMES = ("a_t", "r_t", "b_t", "k_t", "b_w", "k_w", "w_c", "v", "bonus", "gate")

_PROJ_PIECES = ([("rkv", c) for c in range(0, N_RKV, PROJ_PIECE)]
                + [("ret", c) for c in range(N_RKV, N_RKV + N_RET, PROJ_PIECE)]
                + [("lora", c) for c in range(N_RKV + N_RET, N_PROJ, PROJ_PIECE)])


def _mixer_kernel(x_ref, xnext_ref, gmix_ref, wext_ref, vec_ref, w2_ref, seg_ref, tril_ref, cos_ref, sin_ref,
                  dmask_ref, gnw_ref, out_ref, yret_ref, h_ref, rstate_ref, prev_ref, prep0_ref, pret0_ref):
    C, G, TT, STG = RWKV_CHUNK, RWKV_GROUP_HEADS, MIX_TILE, MIX_STAGE
    HD = RWKV_HEAD_DIM
    W = G * HD
    GC = G * C
    n_groups = RWKV_WIDTH // W
    n_stages = TT // STG
    stage_rows = [slice(i * STG, (i + 1) * STG) for i in range(n_stages)]

    @pl.when(pl.program_id(1) == 0)
    def _():
        h_ref[...] = jnp.zeros_like(h_ref)
        rstate_ref[...] = jnp.zeros_like(rstate_ref)

    vec = lambda i: vec_ref[i:i + 1, :]

    seg = seg_ref[...]

    def head_sum(x):
        n_blk = x.shape[1] // LANES
        n_rows = x.shape[0]
        rows = jnp.concatenate([x[:, j * LANES:(j + 1) * LANES] for j in range(n_blk)], axis=0)
        s = _split_dot_right(rows, seg)
        return jnp.concatenate([s[j * n_rows:(j + 1) * n_rows] for j in range(n_blk)], axis=1)

    def lookahead(x_rows_ref, prev_rows, store):
        pp = dict(rkv=[], ret=[], lora=[])

        def piece(i):
            def run():
                if "xb" not in pp:
                    pp["xb"] = _rms_norm(x_rows_ref[...], gmix_ref[...]).astype(BF16)
                name, c0 = _PROJ_PIECES[i]
                res = _dot(pp["xb"], wext_ref[:, c0:c0 + PROJ_PIECE])
                if store and name == "ret":
                    pret0_ref[:, c0 - N_RKV:c0 - N_RKV + PROJ_PIECE] = res
                else:
                    pp[name].append(res)
            return run

        def put(name, val):
            if store:
                prep0_ref[_PREP_NAMES.index(name)] = val
            else:
                pp[name] = val

        def prep_a():
            p_rkv = jnp.concatenate(pp["rkv"], axis=1)
            p_lora = pp["lora"][0]
            pp["last"] = (p_rkv[STG - SUBLANES:], p_lora[STG - SUBLANES:])
            prev_rkv, prev_lora = prev_rows()
            s_rkv = _shift_rows(p_rkv, prev_rkv)
            lerp = lambda j, mu: (p_rkv[:, j * RWKV_WIDTH:(j + 1) * RWKV_WIDTH]
                                  + (s_rkv[:, j * RWKV_WIDTH:(j + 1) * RWKV_WIDTH]
                                     - p_rkv[:, j * RWKV_WIDTH:(j + 1) * RWKV_WIDTH]) * mu)
            if store:
                prev_ref[:, 0:N_RKV], prev_ref[:, N_RKV:N_RKV + N_LORA] = pp["last"]
            pp["r"] = lerp(0, vec(_V_MU_R))
            pp["k"] = lerp(1, vec(_V_MU_K))
            pp["v"] = lerp(2, vec(_V_MU_V))
            put("v", pp["v"])
            low = p_lora[:, 0:LORA_WIDTH] + _shift_rows(p_lora, prev_lora)[:, LORA_WIDTH:2 * LORA_WIDTH]
            lane = lax.broadcasted_iota(jnp.int32, low.shape, 1)
            act = jnp.where(lane < DECAY_LORA, jnp.tanh(low),
                            jnp.where(lane < DECAY_LORA + AAA_LORA, low, jax.nn.sigmoid(low)))
            pp["second"] = _dot(act.astype(BF16), w2_ref[...])

        def prep_b():
            second = pp["second"]
            pp["ld"] = -math.exp(-0.5) * jax.nn.sigmoid(second[:, 0:RWKV_WIDTH] + vec(_V_W0))
            pp["a"] = jax.nn.sigmoid(second[:, RWKV_WIDTH:2 * RWKV_WIDTH] + vec(_V_A0))
            put("gate", second[:, 2 * RWKV_WIDTH:3 * RWKV_WIDTH])
            pp["kk"] = pp["k"] * vec(_V_KK)
            pp["kk_ss"] = head_sum(pp["kk"] * pp["kk"])
            pp["cum"] = _split_dot(tril_ref[...], pp["ld"])

        def prep_c():
            pp["kk"] = pp["kk"] * lax.rsqrt(jnp.maximum(pp["kk_ss"], 1e-24))
            pp["k2"] = pp["k"] * (1.0 + (pp["a"] - 1.0) * vec(_V_KA))
            put("bonus", head_sum(pp["r"] * pp["k2"] * vec(_V_RK)) * pp["v"])

        def prep_d():
            cum, ld, kk, k2 = pp["cum"], pp["ld"], pp["kk"], pp["k2"]
            b = kk * pp["a"]
            cum_end = jnp.concatenate(
                [jnp.broadcast_to(cum[(c + 1) * C - 1:(c + 1) * C, :], (C, RWKV_WIDTH))
                 for c in range(STG // C)], axis=0)
            e_neg = jnp.exp(-cum)
            e_end = jnp.exp(cum_end - cum)
            put("a_t", -kk * jnp.exp(cum - ld))
            put("r_t", pp["r"] * jnp.exp(cum))
            put("b_t", b * e_neg)
            put("k_t", k2 * e_neg)
            put("b_w", b * e_end)
            put("k_w", k2 * e_end)
            put("w_c", jnp.exp(cum_end))

        order = [i for i, (n, _) in enumerate(_PROJ_PIECES) if n != "ret"]
        order += [i for i, (n, _) in enumerate(_PROJ_PIECES) if n == "ret"]
        pieces = [piece(i) for i in order]
        n_first = len(pieces) - N_RET // PROJ_PIECE
        slots = [pieces[0:2], pieces[2:n_first], [prep_a, pieces[n_first]], [prep_b, pieces[n_first + 1]],
                 [prep_c, pieces[n_first + 2]], [prep_d, pieces[n_first + 3]]]
        return pp, slots

    def run_all(slots):
        for slot in slots:
            for thunk in slot:
                thunk()

    zero_prev = lambda: (jnp.zeros((1, N_RKV), F32), jnp.zeros((1, N_LORA), F32))

    @pl.when((pl.program_id(0) == 0) & (pl.program_id(1) == 0))
    def _():
        run_all(lookahead(x_ref.at[stage_rows[0], :], zero_prev, store=True)[1])

    ret_state = [rstate_ref[h] for h in range(RET_HEADS)]
    stage_ret = [None] * n_stages

    def ret_unit(c, h):
        RC, RD = RET_CHUNK, RET_HEAD_DIM
        s, c_loc = divmod(c * RC, STG)
        rows = slice(c * RC, (c + 1) * RC)
        loc = slice(c_loc, c_loc + RC)
        row_i = lax.broadcasted_iota(jnp.int32, (RC, 1), 0).astype(F32)
        lg = _LOG_GAMMA[h]
        cos, sin = cos_ref[rows, :], sin_ref[rows, :]
        rot = lambda xh: xh * cos + pltpu.roll(xh, RD // 2, 1) * sin

        def part(j):
            c0 = j * RET_WIDTH + h * RD
            if s == 0:
                return pret0_ref[loc, c0:c0 + RD]
            return stage_ret[s][c0 // PROJ_PIECE][loc, c0 % PROJ_PIECE:c0 % PROJ_PIECE + RD]

        q = rot(part(0))
        kr = rot(part(1)) * (RD ** -0.5)
        vr = part(2)
        gt = part(3)
        qb, kb, vb = q.astype(BF16), kr.astype(BF16), vr.astype(BF16)
        scores = (_dot_nt(qb, kb) * dmask_ref[h]).astype(BF16)
        kv = _dot_tn(kb, (vr * jnp.exp((RC - 1.0 - row_i) * lg)).astype(BF16))
        inter = _dot((q * jnp.exp((row_i + 1.0) * lg)).astype(BF16), ret_state[h].astype(BF16))
        ret_state[h] = ret_state[h] * _GAMMA_C[h] + kv

        def second_half():
            y = _dot(scores, vb) + inter
            mu = jnp.mean(y, axis=-1, keepdims=True)
            d = y - mu
            var = jnp.mean(d * d, axis=-1, keepdims=True)
            yn = d * lax.rsqrt(var + RET_GN_EPS) * gnw_ref[:, h * RD:(h + 1) * RD]
            yret_ref[rows, h * RD:(h + 1) * RD] = (gt * jax.nn.sigmoid(gt) * yn).astype(BF16)
        return second_half

    ret_units = [(c, h) for c in range(TT // RET_CHUNK) for h in range(RET_HEADS)]
    ret_pending = []
    slot_queue = []

    def fill():
        if ret_pending:
            ret_pending.pop(0)()
        if ret_units:
            ret_pending.append(ret_unit(*ret_units.pop(0)))
        if slot_queue:
            for thunk in slot_queue.pop(0):
                thunk()

    c_bits = C.bit_length() - 1
    hd_bits = HD.bit_length() - 1
    assert C == 1 << c_bits and HD == 1 << hd_bits and W == GC
    sr = lax.broadcasted_iota(jnp.int32, (GC, W), 0)
    sl_ = lax.broadcasted_iota(jnp.int32, (GC, W), 1)
    stack_mask = (sr >> c_bits) == (sl_ >> hd_bits)
    ti = lax.broadcasted_iota(jnp.int32, (C, GC), 0)
    si = lax.broadcasted_iota(jnp.int32, (C, GC), 1) & (C - 1)
    strict = ti > si
    incl = ti >= si
    eye_c = (ti == si).astype(F32)
    wi = lax.broadcasted_iota(jnp.int32, (W, W), 0)
    wj = lax.broadcasted_iota(jnp.int32, (W, W), 1)
    eye_w = wi == wj

    def stack(x):
        return jnp.where(stack_mask, jnp.concatenate([x] * G, axis=0), 0.0)

    blk = lambda z, c, g: z[c * C:(c + 1) * C, g * W:(g + 1) * W]

    def chains_of(p):
        st = []
        for c in range(STG // C):
            for g in range(n_groups):
                a_c, r_c = blk(p["a_t"], c, g), blk(p["r_t"], c, g)
                st.append(dict(
                    ar=jnp.concatenate([a_c, r_c], axis=0).astype(BF16), r32=r_c,
                    a_s=stack(a_c).astype(BF16),
                    bk_s=jnp.concatenate([stack(blk(p["b_t"], c, g)), stack(blk(p["k_t"], c, g))],
                                         axis=0).astype(BF16),
                    bw_t=stack(blk(p["b_w"], c, g)).T.astype(BF16),
                    kw_t=stack(blk(p["k_w"], c, g)).T.astype(BF16),
                    v_s=stack(blk(p["v"], c, g)).astype(BF16), w_end=blk(p["w_c"], c, g)[0:1, :]))
        for s in st:
            prod = _dot_nt(s["ar"], s["bk_s"])
            s["n"] = jnp.where(strict, prod[0:C, 0:GC], 0.0)
            s["a_ak"] = jnp.where(strict, prod[0:C, GC:], 0.0).astype(BF16)
            s["a_rb"] = jnp.where(incl, prod[C:, 0:GC], 0.0).astype(BF16)
            s["a_rk"] = jnp.where(incl, prod[C:, GC:], 0.0).astype(BF16)
        fill()
        for s in st:
            res = _dot(jnp.concatenate([s["a_ak"], s["a_rk"], s["kw_t"]], axis=0), s["v_s"])
            s["av_s"] = stack(res[0:C]).astype(BF16)
            s["rkv"] = res[C:2 * C]
            s["kwv"] = res[2 * C:]
        fill()
        for s in st:
            nb = s["n"].astype(BF16)
            s["t"] = eye_c + s["n"]
            s["p"] = _dot(nb, stack(s["n"]).astype(BF16))
        fill()
        for i in range(1, 6):
            for s in st:
                p_s = stack(s["p"]).astype(BF16)
                if i < 5:
                    res = _dot(jnp.concatenate([s["t"], s["p"]], axis=0).astype(BF16), p_s)
                    s["t"] = s["t"] + res[0:C]
                    s["p"] = res[C:]
                else:
                    s["t"] = (s["t"] + _dot(s["t"].astype(BF16), p_s)).astype(BF16)
            fill()
        for s in st:
            s["x1_s"] = stack(_dot(s["t"], s["a_s"])).astype(BF16)
            s["x2_s"] = stack(_dot(s["t"], s["av_s"])).astype(BF16)
        for s in st:
            lhs = jnp.concatenate([s["a_rb"], s["bw_t"]], axis=0)
            o1 = _dot(lhs, s["x1_s"])
            o2 = _dot(lhs, s["x2_s"])
            s["q"] = (s["r32"] + o1[0:C]).astype(BF16)
            s["m"] = (o1[C:] + jnp.where(eye_w, s["w_end"], 0.0)).astype(BF16)
            s["y0"] = o2[0:C] + s["rkv"]
            s["g"] = o2[C:] + s["kwv"]
        return st

    def state_steps(st, h_cur):
        y_rows = []
        for c in range(STG // C):
            y_lanes = []
            for g in range(n_groups):
                s = st[c * n_groups + g]
                res = _dot(jnp.concatenate([s["q"], s["m"]], axis=0), h_cur[g].astype(BF16))
                y_lanes.append(res[0:C] + s["y0"])
                h_cur[g] = res[C:] + s["g"]
            y_rows.append(jnp.concatenate(y_lanes, axis=1))
        return jnp.concatenate(y_rows, axis=0)

    def finish(rows, y, p):
        inv_hd = 1.0 / HD
        mu = head_sum(y) * inv_hd
        d = y - mu
        var = head_sum(d * d) * inv_hd
        yn = d * lax.rsqrt(var + RWKV_GN_EPS) * vec(_V_LNW) + vec(_V_LNB)
        out_ref[rows, :] = ((yn + p["bonus"]) * p["gate"]).astype(BF16)

    h_cur = [h_ref[g] for g in range(n_groups)]
    preps = [{name: prep0_ref.at[i] for i, name in enumerate(_PREP_NAMES)}]
    preps[0]["bonus"], preps[0]["gate"] = preps[0]["bonus"][...], preps[0]["gate"][...]
    last_step_of_row = pl.program_id(1) == pl.num_programs(1) - 1

    def chains_with_fillers(s):
        if s + 1 < n_stages:
            if s == 0:
                prev_rows = lambda: (prev_ref[SUBLANES - 1:SUBLANES, 0:N_RKV],
                                     prev_ref[SUBLANES - 1:SUBLANES, N_RKV:N_RKV + N_LORA])
            else:
                prev_rows = lambda: tuple(z[SUBLANES - 1:SUBLANES] for z in preps[s]["last"])
            pp, slots = lookahead(x_ref.at[stage_rows[s + 1], :], prev_rows, store=False)
            preps.append(pp)
            stage_ret[s + 1] = pp["ret"]
        else:
            prev_rows = lambda: tuple(jnp.where(last_step_of_row, 0.0, z[SUBLANES - 1:SUBLANES])
                                      for z in preps[s]["last"])
            _, slots = lookahead(xnext_ref, prev_rows, store=True)
        slot_queue.extend(slots)
        st = chains_of(preps[s])
        while slot_queue:
            fill()
        return st

    chains = [chains_with_fillers(0)]
    for s in range(n_stages):
        y = state_steps(chains[s], h_cur)
        if s + 1 < n_stages:
            chains.append(chains_with_fillers(s + 1))
        finish(stage_rows[s], y, preps[s])
    while ret_units or ret_pending:
        fill()
    for g in range(n_groups):
        h_ref[g] = h_cur[g]
    for h in range(RET_HEADS):
        rstate_ref[h] = ret_state[h]


def _mixer(x2, g_mix, w_ext, vecs, w2cat, seg, tril, freq, gn_w, bsz, seq):
    tt = MIX_TILE
    nt = seq // tt
    w = RWKV_GROUP_HEADS * RWKV_HEAD_DIM
    n_groups = RWKV_WIDTH // w
    cos, sin, dmask = _ret_tables(freq, seq)
    row = lambda b, t: (b * nt + t, 0)
    per_tile = tt // MIX_STAGE
    last_stage_block = bsz * nt * per_tile - 1
    next_stage0 = lambda b, t: (jnp.minimum((b * nt + t + 1) * per_tile, last_stage_block), 0)
    pos = lambda b, t: (t, 0)
    const = lambda b, t: (0, 0)
    return pl.pallas_call(
        _mixer_kernel,
        grid=(bsz, nt),
        in_specs=[
            pl.BlockSpec((tt, D_MODEL), row),
            pl.BlockSpec((MIX_STAGE, D_MODEL), next_stage0),
            pl.BlockSpec((1, D_MODEL), const),
            pl.BlockSpec((D_MODEL, N_PROJ), const, pipeline_mode=pl.Buffered(1)),
            pl.BlockSpec((_N_VEC_ROWS, RWKV_WIDTH), const),
            pl.BlockSpec((LORA_WIDTH, 3 * RWKV_WIDTH), const),
            pl.BlockSpec((LANES, LANES), const),
            pl.BlockSpec((MIX_STAGE, MIX_STAGE), const),
            pl.BlockSpec((tt, RET_HEAD_DIM), pos),
            pl.BlockSpec((tt, RET_HEAD_DIM), pos),
            pl.BlockSpec((RET_HEADS, RET_CHUNK, RET_CHUNK), lambda b, t: (0, 0, 0)),
            pl.BlockSpec((1, RET_WIDTH), const),
        ],
        out_specs=[pl.BlockSpec((tt, RWKV_WIDTH), row), pl.BlockSpec((tt, RET_WIDTH), row)],
        out_shape=[jax.ShapeDtypeStruct((bsz * seq, RWKV_WIDTH), BF16),
                   jax.ShapeDtypeStruct((bsz * seq, RET_WIDTH), BF16)],
        scratch_shapes=[pltpu.VMEM((n_groups, w, w), F32),
                        pltpu.VMEM((RET_HEADS, RET_HEAD_DIM, RET_HEAD_DIM), F32),
                        pltpu.VMEM((SUBLANES, N_RKV + N_LORA), F32),
                        pltpu.VMEM((len(_PREP_NAMES), MIX_STAGE, RWKV_WIDTH), F32),
                        pltpu.VMEM((MIX_STAGE, N_RET), F32)],
        compiler_params=pltpu.CompilerParams(
            dimension_semantics=("arbitrary", "arbitrary"), vmem_limit_bytes=_vmem_limit(56 * 1024 * 1024)),
        name="mixer",
    )(x2, x2, g_mix, w_ext, vecs, w2cat, seg, tril, cos, sin, dmask, gn_w)


def _ffn_kernel(yr_ref, yt_ref, x_ref, wo_ref, gf_ref, wg_ref, wu_ref, cw_ref, cb_ref, wd_ref, gl_ref,
                out_ref, carry_ref):
    sm = FFN_TILE // FFN_STAGES
    stages = [slice(i * sm, (i + 1) * sm) for i in range(FFN_STAGES)]

    @pl.when(pl.program_id(1) == 0)
    def _():
        carry_ref[...] = jnp.zeros_like(carry_ref)

    mix = [_dot(yr_ref[s, :], wo_ref[0:RWKV_WIDTH, :]) + _dot(yt_ref[s, :], wo_ref[RWKV_WIDTH:, :])
           for s in stages]
    acc = [x_ref[s, :] + m for s, m in zip(stages, mix)]
    hb = [_rms_norm(x1, gf_ref[...]).astype(BF16) for x1 in acc]

    cols = D_FF // FFN_COL_SPLIT
    for j in range(FFN_COL_SPLIT):
        cs = slice(j * cols, (j + 1) * cols)
        gate_up = [(_dot(h, wg_ref[:, cs]), _dot(h, wu_ref[:, cs])) for h in hb]
        prev = carry_ref[:, cs]
        p2, p1 = prev[SUBLANES - 2:SUBLANES - 1], prev[SUBLANES - 1:SUBLANES]
        hidden = []
        for gate, up in gate_up:
            row = lax.broadcasted_iota(jnp.int32, gate.shape, 0)
            g1 = jnp.where(row == 0, p1, pltpu.roll(gate, 1, 0))
            g2 = jnp.where(row == 0, p2, jnp.where(row == 1, p1, pltpu.roll(gate, 2, 0)))
            p2, p1 = gate[sm - 2:sm - 1], gate[sm - 1:sm]
            conv = cw_ref[0:1, cs] * g2 + cw_ref[1:2, cs] * g1 + cw_ref[2:3, cs] * gate + cb_ref[:, cs]
            hidden.append((conv * jax.nn.sigmoid(conv) * up).astype(BF16))
        carry_ref[:, cs] = gate_up[-1][0][sm - SUBLANES:sm]
        acc = [a + _dot(h, wd_ref[cs, :]) for a, h in zip(acc, hidden)]
    for s, a in zip(stages, acc):
        out_ref[s, :] = _rms_norm(a, gl_ref[...])


def _ffn(y_rwkv, y_ret, x2, w_out, g_ffn, w_gate, w_up, conv_w, conv_b, w_down, g_final, bsz, seq):
    tm = FFN_TILE
    nt = seq // tm
    row = lambda b, t: (b * nt + t, 0)
    const = lambda b, t: (0, 0)
    single = dict(pipeline_mode=pl.Buffered(1))
    weights = (D_MODEL * D_MODEL + 3 * D_MODEL * D_FF) * 2
    vmem = weights + 4 * tm * D_MODEL * 4 + 8 * tm * D_FF * 4 + 8 * 1024 * 1024
    return pl.pallas_call(
        _ffn_kernel,
        grid=(bsz, nt),
        in_specs=[
            pl.BlockSpec((tm, RWKV_WIDTH), row),
            pl.BlockSpec((tm, RET_WIDTH), row),
            pl.BlockSpec((tm, D_MODEL), row),
            pl.BlockSpec((D_MODEL, D_MODEL), const, **single),
            pl.BlockSpec((1, D_MODEL), const),
            pl.BlockSpec((D_MODEL, D_FF), const, **single),
            pl.BlockSpec((D_MODEL, D_FF), const, **single),
            pl.BlockSpec((SUBLANES, D_FF), const),
            pl.BlockSpec((1, D_FF), const),
            pl.BlockSpec((D_FF, D_MODEL), const, **single),
            pl.BlockSpec((1, D_MODEL), const),
        ],
        out_specs=pl.BlockSpec((tm, D_MODEL), row),
        out_shape=jax.ShapeDtypeStruct((bsz * seq, D_MODEL), F32),
        scratch_shapes=[pltpu.VMEM((SUBLANES, D_FF), F32)],
        compiler_params=pltpu.CompilerParams(
            dimension_semantics=("arbitrary", "arbitrary"), vmem_limit_bytes=_vmem_limit(vmem)),
        name="ffn",
    )(y_rwkv, y_ret, x2, w_out, g_ffn, w_gate, w_up, conv_w, conv_b, w_down, g_final)


def _block_ones(n, block):
    i = jnp.arange(n)
    return ((i[:, None] // block) == (i[None, :] // block)).astype(BF16)


def kernel(x, norm_mix_g, w_in, rwkv_mu_r, rwkv_mu_k, rwkv_mu_v, rwkv_mu_w, rwkv_mu_a, rwkv_mu_g, rwkv_w0, rwkv_w1, rwkv_w2, rwkv_a0, rwkv_a1, rwkv_a2, rwkv_g1, rwkv_g2, rwkv_k_k, rwkv_k_a, rwkv_r_k, rwkv_lnx_w, rwkv_lnx_b, ret_gn_w, w_out, norm_ffn_g, ffn_w_gate, ffn_w_up, ffn_conv_w, ffn_conv_b, ffn_w_down, norm_final_g):
    bsz, seq, d = x.shape
    assert d == D_MODEL and seq % MIX_TILE == 0 and seq % FFN_TILE == 0
    assert norm_mix_g.shape[0] == 1, "one layer"
    x2 = x.reshape(bsz * seq, d)
    row = lambda p: p.reshape(1, -1)

    w_cat = jnp.concatenate([rwkv_w1[0], rwkv_a1[0], rwkv_g1[0]], axis=1)
    mu_cat = jnp.concatenate([
        jnp.broadcast_to(rwkv_mu_w[0][:, None], (d, DECAY_LORA)),
        jnp.broadcast_to(rwkv_mu_a[0][:, None], (d, AAA_LORA)),
        jnp.broadcast_to(rwkv_mu_g[0][:, None], (d, GATE_LORA))], axis=1)
    w_ext = jnp.concatenate([w_in[0].astype(BF16), _fold_lora(w_cat, mu_cat)], axis=1)

    vec_rows = [rwkv_mu_r[0], rwkv_mu_k[0], rwkv_mu_v[0], rwkv_w0[0], rwkv_a0[0], rwkv_k_k[0], rwkv_k_a[0],
                rwkv_r_k[0].reshape(-1), rwkv_lnx_w[0], rwkv_lnx_b[0]]
    vecs = jnp.zeros((_N_VEC_ROWS, RWKV_WIDTH), F32).at[:len(vec_rows)].set(jnp.stack(vec_rows))
    w2cat = jnp.zeros((LORA_WIDTH, 3 * RWKV_WIDTH), BF16)
    w2cat = w2cat.at[0:DECAY_LORA, 0:RWKV_WIDTH].set(rwkv_w2[0].astype(BF16))
    w2cat = w2cat.at[DECAY_LORA:DECAY_LORA + AAA_LORA, RWKV_WIDTH:2 * RWKV_WIDTH].set(rwkv_a2[0].astype(BF16))
    w2cat = w2cat.at[DECAY_LORA + AAA_LORA:, 2 * RWKV_WIDTH:].set(rwkv_g2[0].astype(BF16))
    seg = _block_ones(LANES, RWKV_HEAD_DIM)
    ti = jnp.arange(MIX_STAGE)
    cones = _block_ones(MIX_STAGE, RWKV_CHUNK)
    tril = (cones.astype(F32) * (ti[:, None] >= ti[None, :])).astype(BF16)
    half = RET_HEAD_DIM // 2
    inv_freq = ROPE_BASE ** (-jnp.arange(half, dtype=F32) / half)
    freq = jnp.concatenate([inv_freq, inv_freq]).reshape(1, RET_HEAD_DIM)
    y_rwkv, y_ret = _mixer(x2, row(norm_mix_g[0]), w_ext, vecs, w2cat, seg, tril, freq, row(ret_gn_w[0]),
                           bsz, seq)

    conv_w = jnp.zeros((SUBLANES, D_FF), F32).at[0:3].set(ffn_conv_w[0][:, 0, :])
    out = _ffn(y_rwkv, y_ret, x2, w_out[0].astype(BF16), row(norm_ffn_g[0]), ffn_w_gate[0].astype(BF16),
               ffn_w_up[0].astype(BF16), conv_w, row(ffn_conv_b[0]), ffn_w_down[0].astype(BF16),
               row(norm_final_g), bsz, seq)
    return out.reshape(bsz, seq, d)
```

```python
import math

import jax
import jax.numpy as jnp
from jax import lax
from jax.experimental import pallas as pl
from jax.experimental.pallas import tpu as pltpu

F32 = jnp.float32
BF16 = jnp.bfloat16

D_MODEL = 1024
RWKV_HEADS = 8
RWKV_HEAD_DIM = 64
RWKV_WIDTH = 512
RET_HEADS = 4
RET_HEAD_DIM = 128
RET_WIDTH = 512
DECAY_LORA = 64
AAA_LORA = 64
GATE_LORA = 128
LORA_WIDTH = DECAY_LORA + AAA_LORA + GATE_LORA
RET_CHUNK = 128
ROPE_BASE = 10000.0
D_FF = 2816
NORM_EPS = 1e-6
RWKV_GN_EPS = 64e-5
RET_GN_EPS = 1e-5

V7X_VMEM_BYTES = 64 * 1024 * 1024
SUBLANES = 8
LANES = 128

RWKV_CHUNK = 64
RWKV_GROUP_HEADS = 2
MIX_TILE = 512
MIX_STAGE = 256
PROJ_PIECE = 512
FFN_TILE = 512
FFN_STAGES = 2
FFN_COL_SPLIT = 2

N_RKV = 3 * RWKV_WIDTH
N_RET = 4 * RET_WIDTH
N_LORA = 2 * LORA_WIDTH
N_PROJ = N_RKV + N_RET + N_LORA

_LOG_GAMMA = [math.log(1.0 - 2.0 ** (-5.0 - h)) for h in range(RET_HEADS)]
_GAMMA_C = [math.exp(RET_CHUNK * lg) for lg in _LOG_GAMMA]


def _vmem_limit(nbytes):
    return int(min(nbytes, V7X_VMEM_BYTES - 4 * 1024 * 1024))


def _dot(a, b):
    return jnp.dot(a, b, preferred_element_type=F32)


def _dot_nt(a, b):
    return lax.dot_general(a, b, (((1,), (1,)), ((), ())), preferred_element_type=F32)


def _dot_tn(a, b):
    return lax.dot_general(a, b, (((0,), (0,)), ((), ())), preferred_element_type=F32)


def _split_dot(mat_bf16, x):
    hi = x.astype(BF16)
    lo = (x - hi.astype(F32)).astype(BF16)
    return _dot(mat_bf16, hi) + _dot(mat_bf16, lo)


def _split_dot_right(x, mat_bf16):
    hi = x.astype(BF16)
    lo = (x - hi.astype(F32)).astype(BF16)
    return _dot(hi, mat_bf16) + _dot(lo, mat_bf16)


def _shift_rows(x, prev_row):
    rolled = pltpu.roll(x, 1, 0)
    row = lax.broadcasted_iota(jnp.int32, x.shape, 0)
    return jnp.where(row == 0, prev_row, rolled)


def _rms_norm(x, g):
    ms = jnp.mean(x * x, axis=-1, keepdims=True)
    return x * lax.rsqrt(ms + NORM_EPS) * g


def _fold_lora_kernel(w_ref, mu_ref, o_ref):
    w = w_ref[...]
    mu = mu_ref[...]
    o_ref[:, 0:LORA_WIDTH] = (w * (1.0 - mu)).astype(BF16)
    o_ref[:, LORA_WIDTH:2 * LORA_WIDTH] = (w * mu).astype(BF16)


def _fold_lora(w_cat, mu_cat):
    return pl.pallas_call(
        _fold_lora_kernel,
        out_shape=jax.ShapeDtypeStruct((D_MODEL, 2 * LORA_WIDTH), BF16),
        name="fold_lora",
    )(w_cat, mu_cat)


def _ret_tables_kernel(freq_ref, cos_ref, sin_ref, dmask_ref):
    C, HD = RET_CHUNK, RET_HEAD_DIM
    n = cos_ref.shape[0]
    ang = lax.broadcasted_iota(jnp.int32, (n, HD), 0).astype(F32) * freq_ref[...]
    lane = lax.broadcasted_iota(jnp.int32, (n, HD), 1)
    sin = jnp.sin(ang)
    cos_ref[...] = jnp.cos(ang)
    sin_ref[...] = jnp.where(lane < HD // 2, -sin, sin)
    ii = lax.broadcasted_iota(jnp.int32, (C, C), 0)
    jj = lax.broadcasted_iota(jnp.int32, (C, C), 1)
    diff = jnp.maximum((ii - jj).astype(F32), 0.0)
    for h in range(RET_HEADS):
        dmask_ref[h] = jnp.where(ii >= jj, jnp.exp(diff * _LOG_GAMMA[h]), 0.0)


def _ret_tables(freq, seq):
    return pl.pallas_call(
        _ret_tables_kernel,
        out_shape=[jax.ShapeDtypeStruct((seq, RET_HEAD_DIM), F32),
                   jax.ShapeDtypeStruct((seq, RET_HEAD_DIM), F32),
                   jax.ShapeDtypeStruct((RET_HEADS, RET_CHUNK, RET_CHUNK), F32)],
        name="ret_tables",
    )(freq)


(_V_MU_R, _V_MU_K, _V_MU_V, _V_W0, _V_A0, _V_KK, _V_KA, _V_RK, _V_LNW, _V_LNB) = range(10)
_N_VEC_ROWS = 16

_PREP_NAMES = ("a_t", "r_t", "b_t", "k_t", "b_w", "k_w", "w_c", "v", "bonus", "gate")

_PROJ_PIECES = ([("rkv", c) for c in range(0, N_RKV, PROJ_PIECE)]
                + [("ret", c) for c in range(N_RKV, N_RKV + N_RET, PROJ_PIECE)]
                + [("lora", c) for c in range(N_RKV + N_RET, N_PROJ, PROJ_PIECE)])


def _mixer_kernel(xfirst_ref, xnext_ref, gmix_ref, wext_ref, vec_ref, w2_ref, seg_ref, tril_ref, cos_ref,
                  sin_ref, dmask_ref, gnw_ref, out_ref, yret_ref,
                  h_ref, rstate_ref, prev_ref, prep0_ref, pret0_ref, proj1_ref):
    C, G, TT, STG = RWKV_CHUNK, RWKV_GROUP_HEADS, MIX_TILE, MIX_STAGE
    HD = RWKV_HEAD_DIM
    W = G * HD
    GC = G * C
    n_groups = RWKV_WIDTH // W
    n_stages = TT // STG
    stage_rows = [slice(i * STG, (i + 1) * STG) for i in range(n_stages)]

    @pl.when(pl.program_id(1) == 0)
    def _():
        h_ref[...] = jnp.zeros_like(h_ref)
        rstate_ref[...] = jnp.zeros_like(rstate_ref)

    vec = lambda i: vec_ref[i:i + 1, :]

    seg = seg_ref[...]

    def head_sum(x):
        n_blk = x.shape[1] // LANES
        n_rows = x.shape[0]
        rows = jnp.concatenate([x[:, j * LANES:(j + 1) * LANES] for j in range(n_blk)], axis=0)
        s = _split_dot_right(rows, seg)
        return jnp.concatenate([s[j * n_rows:(j + 1) * n_rows] for j in range(n_blk)], axis=1)

    def project(x_rows_ref, stage):
        got = dict(rkv=[], lora=[])

        def piece(i):
            def run():
                if "xb" not in got:
                    got["xb"] = _rms_norm(x_rows_ref[...], gmix_ref[...]).astype(BF16)
                name, c0 = _PROJ_PIECES[i]
                res = _dot(got["xb"], wext_ref[:, c0:c0 + PROJ_PIECE])
                if stage == 1:
                    proj1_ref[:, c0:c0 + PROJ_PIECE] = res
                elif name == "ret":
                    pret0_ref[:, c0 - N_RKV:c0 - N_RKV + PROJ_PIECE] = res
                else:
                    got[name].append(res)
            return run
        return got, [piece(i) for i in range(len(_PROJ_PIECES))]

    def prepare(src, prev_rows, store):
        pp = {}

        def put(name, val):
            if store:
                prep0_ref[_PREP_NAMES.index(name)] = val
            else:
                pp[name] = val

        def prep_a():
            if src is None:
                p_rkv, p_lora = proj1_ref[:, 0:N_RKV], proj1_ref[:, N_RKV + N_RET:N_PROJ]
            else:
                p_rkv, p_lora = jnp.concatenate(src["rkv"], axis=1), src["lora"][0]
            prev_rkv, prev_lora = prev_rows()
            s_rkv = _shift_rows(p_rkv, prev_rkv)
            lerp = lambda j, mu: (p_rkv[:, j * RWKV_WIDTH:(j + 1) * RWKV_WIDTH]
                                  + (s_rkv[:, j * RWKV_WIDTH:(j + 1) * RWKV_WIDTH]
                                     - p_rkv[:, j * RWKV_WIDTH:(j + 1) * RWKV_WIDTH]) * mu)
            if store:
                prev_ref[:, 0:N_RKV] = p_rkv[STG - SUBLANES:]
                prev_ref[:, N_RKV:N_RKV + N_LORA] = p_lora[STG - SUBLANES:]
            pp["r"] = lerp(0, vec(_V_MU_R))
            pp["k"] = lerp(1, vec(_V_MU_K))
            pp["v"] = lerp(2, vec(_V_MU_V))
            put("v", pp["v"])
            low = p_lora[:, 0:LORA_WIDTH] + _shift_rows(p_lora, prev_lora)[:, LORA_WIDTH:2 * LORA_WIDTH]
            lane = lax.broadcasted_iota(jnp.int32, low.shape, 1)
            act = jnp.where(lane < DECAY_LORA, jnp.tanh(low),
                            jnp.where(lane < DECAY_LORA + AAA_LORA, low, jax.nn.sigmoid(low)))
            pp["second"] = _dot(act.astype(BF16), w2_ref[...])

        def prep_b():
            second = pp["second"]
            pp["ld"] = -math.exp(-0.5) * jax.nn.sigmoid(second[:, 0:RWKV_WIDTH] + vec(_V_W0))
            pp["a"] = jax.nn.sigmoid(second[:, RWKV_WIDTH:2 * RWKV_WIDTH] + vec(_V_A0))
            put("gate", second[:, 2 * RWKV_WIDTH:3 * RWKV_WIDTH])
            pp["kk"] = pp["k"] * vec(_V_KK)
            pp["kk_ss"] = head_sum(pp["kk"] * pp["kk"])
            pp["cum"] = _split_dot(tril_ref[...], pp["ld"])

        def prep_c():
            pp["kk"] = pp["kk"] * lax.rsqrt(jnp.maximum(pp["kk_ss"], 1e-24))
            pp["k2"] = pp["k"] * (1.0 + (pp["a"] - 1.0) * vec(_V_KA))
            put("bonus", head_sum(pp["r"] * pp["k2"] * vec(_V_RK)) * pp["v"])

        def prep_d():
            cum, ld, kk, k2 = pp["cum"], pp["ld"], pp["kk"], pp["k2"]
            b = kk * pp["a"]
            cum_end = jnp.concatenate(
                [jnp.broadcast_to(cum[(c + 1) * C - 1:(c + 1) * C, :], (C, RWKV_WIDTH))
                 for c in range(STG // C)], axis=0)
            e_neg = jnp.exp(-cum)
            e_end = jnp.exp(cum_end - cum)
            put("a_t", -kk * jnp.exp(cum - ld))
            put("r_t", pp["r"] * jnp.exp(cum))
            put("b_t", b * e_neg)
            put("k_t", k2 * e_neg)
            put("b_w", b * e_end)
            put("k_w", k2 * e_end)
            put("w_c", jnp.exp(cum_end))

        return pp, [prep_a, prep_b, prep_c, prep_d]

    def run_all(thunks):
        for thunk in thunks:
            thunk()

    zero_prev = lambda: (jnp.zeros((1, N_RKV), F32), jnp.zeros((1, N_LORA), F32))

    @pl.when((pl.program_id(0) == 0) & (pl.program_id(1) == 0))
    def _():
        got, pieces = project(xfirst_ref.at[stage_rows[0], :], 0)
        run_all(pieces)
        run_all(prepare(got, zero_prev, store=True)[1])
        run_all(project(xfirst_ref.at[stage_rows[1], :], 1)[1])

    ret_state = [rstate_ref[h] for h in range(RET_HEADS)]

    def ret_unit(c, h):
        RC, RD = RET_CHUNK, RET_HEAD_DIM
        s, c_loc = divmod(c * RC, STG)
        rows = slice(c * RC, (c + 1) * RC)
        loc = slice(c_loc, c_loc + RC)
        row_i = lax.broadcasted_iota(jnp.int32, (RC, 1), 0).astype(F32)
        lg = _LOG_GAMMA[h]
        cos, sin = cos_ref[rows, :], sin_ref[rows, :]
        rot = lambda xh: xh * cos + pltpu.roll(xh, RD // 2, 1) * sin

        def part(j):
            c0 = j * RET_WIDTH + h * RD
            if s == 0:
                return pret0_ref[loc, c0:c0 + RD]
            return proj1_ref[loc, N_RKV + c0:N_RKV + c0 + RD]

        q = rot(part(0))
        kr = rot(part(1)) * (RD ** -0.5)
        vr = part(2)
        gt = part(3)
        qb, kb, vb = q.astype(BF16), kr.astype(BF16), vr.astype(BF16)
        scores = (_dot_nt(qb, kb) * dmask_ref[h]).astype(BF16)
        kv = _dot_tn(kb, (vr * jnp.exp((RC - 1.0 - row_i) * lg)).astype(BF16))
        inter = _dot((q * jnp.exp((row_i + 1.0) * lg)).astype(BF16), ret_state[h].astype(BF16))
        ret_state[h] = ret_state[h] * _GAMMA_C[h] + kv

        def second_half():
            y = _dot(scores, vb) + inter
            mu = jnp.mean(y, axis=-1, keepdims=True)
            d = y - mu
            var = jnp.mean(d * d, axis=-1, keepdims=True)
            yn = d * lax.rsqrt(var + RET_GN_EPS) * gnw_ref[:, h * RD:(h + 1) * RD]
            yret_ref[rows, h * RD:(h + 1) * RD] = (gt * jax.nn.sigmoid(gt) * yn).astype(BF16)
        return second_half

    ret_units = [(c, h) for c in range(TT // RET_CHUNK) for h in range(RET_HEADS)]
    ret_pending = []
    slot_queue = []

    def fill():
        if ret_pending:
            ret_pending.pop(0)()
        if ret_units:
            ret_pending.append(ret_unit(*ret_units.pop(0)))
        if slot_queue:
            for thunk in slot_queue.pop(0):
                thunk()

    c_bits = C.bit_length() - 1
    hd_bits = HD.bit_length() - 1
    assert C == 1 << c_bits and HD == 1 << hd_bits and W == GC
    sr = lax.broadcasted_iota(jnp.int32, (GC, W), 0)
    sl_ = lax.broadcasted_iota(jnp.int32, (GC, W), 1)
    stack_mask = (sr >> c_bits) == (sl_ >> hd_bits)
    ti = lax.broadcasted_iota(jnp.int32, (C, GC), 0)
    si = lax.broadcasted_iota(jnp.int32, (C, GC), 1) & (C - 1)
    strict = ti > si
    incl = ti >= si
    eye_c = (ti == si).astype(F32)
    wi = lax.broadcasted_iota(jnp.int32, (W, W), 0)
    wj = lax.broadcasted_iota(jnp.int32, (W, W), 1)
    eye_w = wi == wj

    def stack(x):
        return jnp.where(stack_mask, jnp.concatenate([x] * G, axis=0), 0.0)

    blk = lambda z, c, g: z[c * C:(c + 1) * C, g * W:(g + 1) * W]

    def chains_of(p):
        st = []
        for c in range(STG // C):
            for g in range(n_groups):
                a_c, r_c = blk(p["a_t"], c, g), blk(p["r_t"], c, g)
                st.append(dict(
                    ar=jnp.concatenate([a_c, r_c], axis=0).astype(BF16), r32=r_c,
                    a_s=stack(a_c).astype(BF16),
                    bk_s=jnp.concatenate([stack(blk(p["b_t"], c, g)), stack(blk(p["k_t"], c, g))],
                                         axis=0).astype(BF16),
                    bw_t=stack(blk(p["b_w"], c, g)).T.astype(BF16),
                    kw_t=stack(blk(p["k_w"], c, g)).T.astype(BF16),
                    v_s=stack(blk(p["v"], c, g)).astype(BF16), w_end=blk(p["w_c"], c, g)[0:1, :]))
        for s in st:
            prod = _dot_nt(s["ar"], s["bk_s"])
            s["n"] = jnp.where(strict, prod[0:C, 0:GC], 0.0)
            s["a_ak"] = jnp.where(strict, prod[0:C, GC:], 0.0).astype(BF16)
            s["a_rb"] = jnp.where(incl, prod[C:, 0:GC], 0.0).astype(BF16)
            s["a_rk"] = jnp.where(incl, prod[C:, GC:], 0.0).astype(BF16)
        fill()
        for s in st:
            res = _dot(jnp.concatenate([s["a_ak"], s["a_rk"], s["kw_t"]], axis=0), s["v_s"])
            s["av_s"] = stack(res[0:C]).astype(BF16)
            s["rkv"] = res[C:2 * C]
            s["kwv"] = res[2 * C:]
        fill()
        for s in st:
            nb = s["n"].astype(BF16)
            s["t"] = eye_c + s["n"]
            s["p"] = _dot(nb, stack(s["n"]).astype(BF16))
        fill()
        for i in range(1, 6):
            for s in st:
                p_s = stack(s["p"]).astype(BF16)
                if i < 5:
                    res = _dot(jnp.concatenate([s["t"], s["p"]], axis=0).astype(BF16), p_s)
                    s["t"] = s["t"] + res[0:C]
                    s["p"] = res[C:]
                else:
                    s["t"] = (s["t"] + _dot(s["t"].astype(BF16), p_s)).astype(BF16)
            fill()
        for s in st:
            s["x1_s"] = stack(_dot(s["t"], s["a_s"])).astype(BF16)
            s["x2_s"] = stack(_dot(s["t"], s["av_s"])).astype(BF16)
        for s in st:
            lhs = jnp.concatenate([s["a_rb"], s["bw_t"]], axis=0)
            o1 = _dot(lhs, s["x1_s"])
            o2 = _dot(lhs, s["x2_s"])
            s["q"] = (s["r32"] + o1[0:C]).astype(BF16)
            s["m"] = (o1[C:] + jnp.where(eye_w, s["w_end"], 0.0)).astype(BF16)
            s["y0"] = o2[0:C] + s["rkv"]
            s["g"] = o2[C:] + s["kwv"]
        return st

    def state_steps(st, h_cur, fillers):
        y_rows = []
        n_chunks = STG // C
        per_step = -(-len(fillers) // n_chunks)
        for c in range(n_chunks):
            y_lanes = []
            for g in range(n_groups):
                s = st[c * n_groups + g]
                res = _dot(jnp.concatenate([s["q"], s["m"]], axis=0), h_cur[g].astype(BF16))
                y_lanes.append(res[0:C] + s["y0"])
                h_cur[g] = res[C:] + s["g"]
            y_rows.append(jnp.concatenate(y_lanes, axis=1))
            run_all(fillers[c * per_step:(c + 1) * per_step])
        return jnp.concatenate(y_rows, axis=0)

    def finish(rows, y, p):
        inv_hd = 1.0 / HD
        mu = head_sum(y) * inv_hd
        d = y - mu
        var = head_sum(d * d) * inv_hd
        yn = d * lax.rsqrt(var + RWKV_GN_EPS) * vec(_V_LNW) + vec(_V_LNB)
        out_ref[rows, :] = ((yn + p["bonus"]) * p["gate"]).astype(BF16)

    assert n_stages == 2
    h_cur = [h_ref[g] for g in range(n_groups)]
    prep_s0 = {name: prep0_ref.at[i] for i, name in enumerate(_PREP_NAMES)}
    prep_s0["bonus"], prep_s0["gate"] = prep_s0["bonus"][...], prep_s0["gate"][...]
    last_step_of_row = pl.program_id(1) == pl.num_programs(1) - 1

    def chains_with_fillers(p, thunks):
        for t in thunks:
            slot_queue.extend([[t], []])
        st = chains_of(p)
        while slot_queue:
            fill()
        return st

    prev_s1 = lambda: (prev_ref[SUBLANES - 1:SUBLANES, 0:N_RKV],
                       prev_ref[SUBLANES - 1:SUBLANES, N_RKV:N_RKV + N_LORA])
    prep_s1, thunks = prepare(None, prev_s1, store=False)
    chains_s0 = chains_with_fillers(prep_s0, thunks)

    got_n0, pieces = project(xnext_ref.at[stage_rows[0], :], 0)
    y_s0 = state_steps(chains_s0, h_cur, pieces)

    prev_n0 = lambda: (jnp.where(last_step_of_row, 0.0, proj1_ref[STG - 1:STG, 0:N_RKV]),
                       jnp.where(last_step_of_row, 0.0, proj1_ref[STG - 1:STG, N_RKV + N_RET:N_PROJ]))
    chains_s1 = chains_with_fillers(prep_s1, prepare(got_n0, prev_n0, store=True)[1])

    finish(stage_rows[0], y_s0, prep_s0)
    y_s1 = state_steps(chains_s1, h_cur, project(xnext_ref.at[stage_rows[1], :], 1)[1])
    finish(stage_rows[1], y_s1, prep_s1)
    while ret_units or ret_pending:
        fill()
    for g in range(n_groups):
        h_ref[g] = h_cur[g]
    for h in range(RET_HEADS):
        rstate_ref[h] = ret_state[h]


def _mixer(x2, g_mix, w_ext, vecs, w2cat, seg, tril, freq, gn_w, bsz, seq):
    tt = MIX_TILE
    nt = seq // tt
    w = RWKV_GROUP_HEADS * RWKV_HEAD_DIM
    n_groups = RWKV_WIDTH // w
    cos, sin, dmask = _ret_tables(freq, seq)
    row = lambda b, t: (b * nt + t, 0)
    next_tile = lambda b, t: (jnp.minimum(b * nt + t + 1, bsz * nt - 1), 0)
    pos = lambda b, t: (t, 0)
    const = lambda b, t: (0, 0)
    return pl.pallas_call(
        _mixer_kernel,
        grid=(bsz, nt),
        in_specs=[
            pl.BlockSpec((tt, D_MODEL), const),
            pl.BlockSpec((tt, D_MODEL), next_tile),
            pl.BlockSpec((1, D_MODEL), const),
            pl.BlockSpec((D_MODEL, N_PROJ), const, pipeline_mode=pl.Buffered(1)),
            pl.BlockSpec((_N_VEC_ROWS, RWKV_WIDTH), const),
            pl.BlockSpec((LORA_WIDTH, 3 * RWKV_WIDTH), const),
            pl.BlockSpec((LANES, LANES), const),
            pl.BlockSpec((MIX_STAGE, MIX_STAGE), const),
            pl.BlockSpec((tt, RET_HEAD_DIM), pos),
            pl.BlockSpec((tt, RET_HEAD_DIM), pos),
            pl.BlockSpec((RET_HEADS, RET_CHUNK, RET_CHUNK), lambda b, t: (0, 0, 0)),
            pl.BlockSpec((1, RET_WIDTH), const),
        ],
        out_specs=[pl.BlockSpec((tt, RWKV_WIDTH), row), pl.BlockSpec((tt, RET_WIDTH), row)],
        out_shape=[jax.ShapeDtypeStruct((bsz * seq, RWKV_WIDTH), BF16),
                   jax.ShapeDtypeStruct((bsz * seq, RET_WIDTH), BF16)],
        scratch_shapes=[pltpu.VMEM((n_groups, w, w), F32),
                        pltpu.VMEM((RET_HEADS, RET_HEAD_DIM, RET_HEAD_DIM), F32),
                        pltpu.VMEM((SUBLANES, N_RKV + N_LORA), F32),
                        pltpu.VMEM((len(_PREP_NAMES), MIX_STAGE, RWKV_WIDTH), F32),
                        pltpu.VMEM((MIX_STAGE, N_RET), F32),
                        pltpu.VMEM((MIX_STAGE, N_PROJ), F32)],
        compiler_params=pltpu.CompilerParams(
            dimension_semantics=("arbitrary", "arbitrary"), vmem_limit_bytes=_vmem_limit(56 * 1024 * 1024)),
        name="mixer",
    )(x2, x2, g_mix, w_ext, vecs, w2cat, seg, tril, cos, sin, dmask, gn_w)


def _ffn_kernel(yr_ref, yt_ref, x_ref, wo_ref, gf_ref, wg_ref, wu_ref, cw_ref, cb_ref, wd_ref, gl_ref,
                out_ref, carry_ref):
    sm = FFN_TILE // FFN_STAGES
    stages = [slice(i * sm, (i + 1) * sm) for i in range(FFN_STAGES)]

    @pl.when(pl.program_id(1) == 0)
    def _():
        carry_ref[...] = jnp.zeros_like(carry_ref)

    mix = [_dot(yr_ref[s, :], wo_ref[0:RWKV_WIDTH, :]) + _dot(yt_ref[s, :], wo_ref[RWKV_WIDTH:, :])
           for s in stages]
    acc = [x_ref[s, :] + m for s, m in zip(stages, mix)]
    hb = [_rms_norm(x1, gf_ref[...]).astype(BF16) for x1 in acc]

    cols = D_FF // FFN_COL_SPLIT
    for j in range(FFN_COL_SPLIT):
        cs = slice(j * cols, (j + 1) * cols)
        gate_up = [(_dot(h, wg_ref[:, cs]), _dot(h, wu_ref[:, cs])) for h in hb]
        prev = carry_ref[:, cs]
        p2, p1 = prev[SUBLANES - 2:SUBLANES - 1], prev[SUBLANES - 1:SUBLANES]
        hidden = []
        for gate, up in gate_up:
            row = lax.broadcasted_iota(jnp.int32, gate.shape, 0)
            g1 = jnp.where(row == 0, p1, pltpu.roll(gate, 1, 0))
            g2 = jnp.where(row == 0, p2, jnp.where(row == 1, p1, pltpu.roll(gate, 2, 0)))
            p2, p1 = gate[sm - 2:sm - 1], gate[sm - 1:sm]
            conv = cw_ref[0:1, cs] * g2 + cw_ref[1:2, cs] * g1 + cw_ref[2:3, cs] * gate + cb_ref[:, cs]
            hidden.append((conv * jax.nn.sigmoid(conv) * up).astype(BF16))
        carry_ref[:, cs] = gate_up[-1][0][sm - SUBLANES:sm]
        acc = [a + _dot(h, wd_ref[cs, :]) for a, h in zip(acc, hidden)]
    for s, a in zip(stages, acc):
        out_ref[s, :] = _rms_norm(a, gl_ref[...])


def _ffn(y_rwkv, y_ret, x2, w_out, g_ffn, w_gate, w_up, conv_w, conv_b, w_down, g_final, bsz, seq):
    tm = FFN_TILE
    nt = seq // tm
    row = lambda b, t: (b * nt + t, 0)
    const = lambda b, t: (0, 0)
    single = dict(pipeline_mode=pl.Buffered(1))
    weights = (D_MODEL * D_MODEL + 3 * D_MODEL * D_FF) * 2
    vmem = weights + 4 * tm * D_MODEL * 4 + 8 * tm * D_FF * 4 + 8 * 1024 * 1024
    return pl.pallas_call(
        _ffn_kernel,
        grid=(bsz, nt),
        in_specs=[
            pl.BlockSpec((tm, RWKV_WIDTH), row),
            pl.BlockSpec((tm, RET_WIDTH), row),
            pl.BlockSpec((tm, D_MODEL), row),
            pl.BlockSpec((D_MODEL, D_MODEL), const, **single),
            pl.BlockSpec((1, D_MODEL), const),
            pl.BlockSpec((D_MODEL, D_FF), const, **single),
            pl.BlockSpec((D_MODEL, D_FF), const, **single),
            pl.BlockSpec((SUBLANES, D_FF), const),
            pl.BlockSpec((1, D_FF), const),
            pl.BlockSpec((D_FF, D_MODEL), const, **single),
            pl.BlockSpec((1, D_MODEL), const),
        ],
        out_specs=pl.BlockSpec((tm, D_MODEL), row),
        out_shape=jax.ShapeDtypeStruct((bsz * seq, D_MODEL), F32),
        scratch_shapes=[pltpu.VMEM((SUBLANES, D_FF), F32)],
        compiler_params=pltpu.CompilerParams(
            dimension_semantics=("arbitrary", "arbitrary"), vmem_limit_bytes=_vmem_limit(vmem)),
        name="ffn",
    )(y_rwkv, y_ret, x2, w_out, g_ffn, w_gate, w_up, conv_w, conv_b, w_down, g_final)


def _block_ones(n, block):
    i = jnp.arange(n)
    return ((i[:, None] // block) == (i[None, :] // block)).astype(BF16)


def kernel(x, norm_mix_g, w_in, rwkv_mu_r, rwkv_mu_k, rwkv_mu_v, rwkv_mu_w, rwkv_mu_a, rwkv_mu_g, rwkv_w0, rwkv_w1, rwkv_w2, rwkv_a0, rwkv_a1, rwkv_a2, rwkv_g1, rwkv_g2, rwkv_k_k, rwkv_k_a, rwkv_r_k, rwkv_lnx_w, rwkv_lnx_b, ret_gn_w, w_out, norm_ffn_g, ffn_w_gate, ffn_w_up, ffn_conv_w, ffn_conv_b, ffn_w_down, norm_final_g):
    bsz, seq, d = x.shape
    assert d == D_MODEL and seq % MIX_TILE == 0 and seq % FFN_TILE == 0
    assert norm_mix_g.shape[0] == 1, "one layer"
    x2 = x.reshape(bsz * seq, d)
    row = lambda p: p.reshape(1, -1)

    w_cat = jnp.concatenate([rwkv_w1[0], rwkv_a1[0], rwkv_g1[0]], axis=1)
    mu_cat = jnp.concatenate([
        jnp.broadcast_to(rwkv_mu_w[0][:, None], (d, DECAY_LORA)),
        jnp.broadcast_to(rwkv_mu_a[0][:, None], (d, AAA_LORA)),
        jnp.broadcast_to(rwkv_mu_g[0][:, None], (d, GATE_LORA))], axis=1)
    w_ext = jnp.concatenate([w_in[0].astype(BF16), _fold_lora(w_cat, mu_cat)], axis=1)

    vec_rows = [rwkv_mu_r[0], rwkv_mu_k[0], rwkv_mu_v[0], rwkv_w0[0], rwkv_a0[0], rwkv_k_k[0], rwkv_k_a[0],
                rwkv_r_k[0].reshape(-1), rwkv_lnx_w[0], rwkv_lnx_b[0]]
    vecs = jnp.zeros((_N_VEC_ROWS, RWKV_WIDTH), F32).at[:len(vec_rows)].set(jnp.stack(vec_rows))
    w2cat = jnp.zeros((LORA_WIDTH, 3 * RWKV_WIDTH), BF16)
    w2cat = w2cat.at[0:DECAY_LORA, 0:RWKV_WIDTH].set(rwkv_w2[0].astype(BF16))
    w2cat = w2cat.at[DECAY_LORA:DECAY_LORA + AAA_LORA, RWKV_WIDTH:2 * RWKV_WIDTH].set(rwkv_a2[0].astype(BF16))
    w2cat = w2cat.at[DECAY_LORA + AAA_LORA:, 2 * RWKV_WIDTH:].set(rwkv_g2[0].astype(BF16))
    seg = _block_ones(LANES, RWKV_HEAD_DIM)
    ti = jnp.arange(MIX_STAGE)
    cones = _block_ones(MIX_STAGE, RWKV_CHUNK)
    tril = (cones.astype(F32) * (ti[:, None] >= ti[None, :])).astype(BF16)
    half = RET_HEAD_DIM // 2
    inv_freq = ROPE_BASE ** (-jnp.arange(half, dtype=F32) / half)
    freq = jnp.concatenate([inv_freq, inv_freq]).reshape(1, RET_HEAD_DIM)
    y_rwkv, y_ret = _mixer(x2, row(norm_mix_g[0]), w_ext, vecs, w2cat, seg, tril, freq, row(ret_gn_w[0]),
                           bsz, seq)

    conv_w = jnp.zeros((SUBLANES, D_FF), F32).at[0:3].set(ffn_conv_w[0][:, 0, :])
    out = _ffn(y_rwkv, y_ret, x2, w_out[0].astype(BF16), row(norm_ffn_g[0]), ffn_w_gate[0].astype(BF16),
               ffn_w_up[0].astype(BF16), conv_w, row(ffn_conv_b[0]), ffn_w_down[0].astype(BF16),
               row(norm_final_g), bsz, seq)
    return out.reshape(bsz, seq, d)
```

```python
import math

import jax
import jax.numpy as jnp
from jax import lax
from jax.experimental import pallas as pl
from jax.experimental.pallas import tpu as pltpu

F32 = jnp.float32
BF16 = jnp.bfloat16

D_MODEL = 1024
RWKV_HEADS = 8
RWKV_HEAD_DIM = 64
RWKV_WIDTH = 512
RET_HEADS = 4
RET_HEAD_DIM = 128
RET_WIDTH = 512
DECAY_LORA = 64
AAA_LORA = 64
GATE_LORA = 128
LORA_WIDTH = DECAY_LORA + AAA_LORA + GATE_LORA
RET_CHUNK = 128
ROPE_BASE = 10000.0
D_FF = 2816
NORM_EPS = 1e-6
RWKV_GN_EPS = 64e-5
RET_GN_EPS = 1e-5

V7X_VMEM_BYTES = 64 * 1024 * 1024
SUBLANES = 8
LANES = 128

RWKV_CHUNK = 64
RWKV_GROUP_HEADS = 2
MIX_TILE = 512
MIX_STAGE = 256
PROJ_PIECE = 512
FFN_TILE = 512
FFN_STAGES = 2
FFN_COL_SPLIT = 2

N_RKV = 3 * RWKV_WIDTH
N_RET = 4 * RET_WIDTH
N_LORA = 2 * LORA_WIDTH
N_PROJ = N_RKV + N_RET + N_LORA

_LOG_GAMMA = [math.log(1.0 - 2.0 ** (-5.0 - h)) for h in range(RET_HEADS)]
_GAMMA_C = [math.exp(RET_CHUNK * lg) for lg in _LOG_GAMMA]


def _vmem_limit(nbytes):
    return int(min(nbytes, V7X_VMEM_BYTES - 4 * 1024 * 1024))


def _dot(a, b):
    return jnp.dot(a, b, preferred_element_type=F32)


def _dot_nt(a, b):
    return lax.dot_general(a, b, (((1,), (1,)), ((), ())), preferred_element_type=F32)


def _dot_tn(a, b):
    return lax.dot_general(a, b, (((0,), (0,)), ((), ())), preferred_element_type=F32)


def _split_dot(mat_bf16, x):
    hi = x.astype(BF16)
    lo = (x - hi.astype(F32)).astype(BF16)
    return _dot(mat_bf16, hi) + _dot(mat_bf16, lo)


def _shift_rows(x, prev_row):
    rolled = pltpu.roll(x, 1, 0)
    row = lax.broadcasted_iota(jnp.int32, x.shape, 0)
    return jnp.where(row == 0, prev_row, rolled)


def _rms_norm(x, g):
    ms = jnp.mean(x * x, axis=-1, keepdims=True)
    return x * lax.rsqrt(ms + NORM_EPS) * g


def _fold_lora_kernel(w_ref, mu_ref, o_ref):
    w = w_ref[...]
    mu = mu_ref[...]
    o_ref[:, 0:LORA_WIDTH] = (w * (1.0 - mu)).astype(BF16)
    o_ref[:, LORA_WIDTH:2 * LORA_WIDTH] = (w * mu).astype(BF16)


def _fold_lora(w_cat, mu_cat):
    return pl.pallas_call(
        _fold_lora_kernel,
        out_shape=jax.ShapeDtypeStruct((D_MODEL, 2 * LORA_WIDTH), BF16),
        name="fold_lora",
    )(w_cat, mu_cat)


def _ret_tables_kernel(freq_ref, cos_ref, sin_ref, dmask_ref):
    C, HD = RET_CHUNK, RET_HEAD_DIM
    n = cos_ref.shape[0]
    ang = lax.broadcasted_iota(jnp.int32, (n, HD), 0).astype(F32) * freq_ref[...]
    lane = lax.broadcasted_iota(jnp.int32, (n, HD), 1)
    sin = jnp.sin(ang)
    cos_ref[...] = jnp.cos(ang)
    sin_ref[...] = jnp.where(lane < HD // 2, -sin, sin)
    ii = lax.broadcasted_iota(jnp.int32, (C, C), 0)
    jj = lax.broadcasted_iota(jnp.int32, (C, C), 1)
    diff = jnp.maximum((ii - jj).astype(F32), 0.0)
    for h in range(RET_HEADS):
        dmask_ref[h] = jnp.where(ii >= jj, jnp.exp(diff * _LOG_GAMMA[h]), 0.0)


def _ret_tables(freq, seq):
    return pl.pallas_call(
        _ret_tables_kernel,
        out_shape=[jax.ShapeDtypeStruct((seq, RET_HEAD_DIM), F32),
                   jax.ShapeDtypeStruct((seq, RET_HEAD_DIM), F32),
                   jax.ShapeDtypeStruct((RET_HEADS, RET_CHUNK, RET_CHUNK), F32)],
        name="ret_tables",
    )(freq)


(_V_MU_R, _V_MU_K, _V_MU_V, _V_W0, _V_A0, _V_KK, _V_KA, _V_RK, _V_LNW, _V_LNB) = range(10)
_N_VEC_ROWS = 16

_PREP_NAMES = ("a_t", "r_t", "b_t", "k_t", "b_w", "k_w", "w_c", "v", "bonus", "gate")

_PROJ_PIECES = ([("rkv", c) for c in range(0, N_RKV, PROJ_PIECE)]
                + [("ret", c) for c in range(N_RKV, N_RKV + N_RET, PROJ_PIECE)]
                + [("lora", c) for c in range(N_RKV + N_RET, N_PROJ, PROJ_PIECE)])


def _mixer_kernel(xfirst_ref, xnext_ref, gmix_ref, win_ref, wlo_ref, vec_ref, w2_ref, seg_ref, tril_ref, cos_ref,
                  sin_ref, dmask_ref, gnw_ref, out_ref, yret_ref,
                  h_ref, rstate_ref, prev_ref, prep0_ref, pret0_ref, proj1_ref):
    C, G, TT, STG = RWKV_CHUNK, RWKV_GROUP_HEADS, MIX_TILE, MIX_STAGE
    HD = RWKV_HEAD_DIM
    W = G * HD
    GC = G * C
    n_groups = RWKV_WIDTH // W
    n_stages = TT // STG
    stage_rows = [slice(i * STG, (i + 1) * STG) for i in range(n_stages)]

    @pl.when(pl.program_id(1) == 0)
    def _():
        h_ref[...] = jnp.zeros_like(h_ref)
        rstate_ref[...] = jnp.zeros_like(rstate_ref)

    vec = lambda i: vec_ref[i:i + 1, :]

    seg = seg_ref[...]

    def head_sum(x):
        n_blk = x.shape[1] // LANES
        n_rows = x.shape[0]
        rows = jnp.concatenate([x[:, j * LANES:(j + 1) * LANES] for j in range(n_blk)], axis=0)
        s = _dot(rows.astype(BF16), seg)
        return jnp.concatenate([s[j * n_rows:(j + 1) * n_rows] for j in range(n_blk)], axis=1)

    def project(x_rows_ref, stage):
        got = dict(rkv=[], lora=[])

        def piece(i):
            def run():
                if "xb" not in got:
                    got["xb"] = _rms_norm(x_rows_ref[...], gmix_ref[...]).astype(BF16)
                name, c0 = _PROJ_PIECES[i]
                n_in = N_RKV + N_RET
                w = win_ref[:, c0:c0 + PROJ_PIECE] if c0 < n_in else wlo_ref[:, c0 - n_in:c0 - n_in + PROJ_PIECE]
                res = _dot(got["xb"], w)
                if stage == 1:
                    proj1_ref[:, c0:c0 + PROJ_PIECE] = res
                elif name == "ret":
                    pret0_ref[:, c0 - N_RKV:c0 - N_RKV + PROJ_PIECE] = res
                else:
                    got[name].append(res)
            return run
        return got, [piece(i) for i in range(len(_PROJ_PIECES))]

    def prepare(src, prev_rows, store):
        pp = {}

        def put(name, val):
            if store:
                prep0_ref[_PREP_NAMES.index(name)] = val
            else:
                pp[name] = val

        def prep_a():
            if src is None:
                p_rkv, p_lora = proj1_ref[:, 0:N_RKV], proj1_ref[:, N_RKV + N_RET:N_PROJ]
            else:
                p_rkv, p_lora = jnp.concatenate(src["rkv"], axis=1), src["lora"][0]
            prev_rkv, prev_lora = prev_rows()
            s_rkv = _shift_rows(p_rkv, prev_rkv)
            lerp = lambda j, mu: (p_rkv[:, j * RWKV_WIDTH:(j + 1) * RWKV_WIDTH]
                                  + (s_rkv[:, j * RWKV_WIDTH:(j + 1) * RWKV_WIDTH]
                                     - p_rkv[:, j * RWKV_WIDTH:(j + 1) * RWKV_WIDTH]) * mu)
            if store:
                prev_ref[:, 0:N_RKV] = p_rkv[STG - SUBLANES:]
                prev_ref[:, N_RKV:N_RKV + N_LORA] = p_lora[STG - SUBLANES:]
            pp["r"] = lerp(0, vec(_V_MU_R))
            pp["k"] = lerp(1, vec(_V_MU_K))
            pp["v"] = lerp(2, vec(_V_MU_V))
            put("v", pp["v"])
            low = p_lora[:, 0:LORA_WIDTH] + _shift_rows(p_lora, prev_lora)[:, LORA_WIDTH:2 * LORA_WIDTH]
            lane = lax.broadcasted_iota(jnp.int32, low.shape, 1)
            act = jnp.where(lane < DECAY_LORA, jnp.tanh(low),
                            jnp.where(lane < DECAY_LORA + AAA_LORA, low, jax.nn.sigmoid(low)))
            pp["second"] = _dot(act.astype(BF16), w2_ref[...])

        def prep_b():
            second = pp["second"]
            pp["ld"] = -math.exp(-0.5) * jax.nn.sigmoid(second[:, 0:RWKV_WIDTH] + vec(_V_W0))
            pp["a"] = jax.nn.sigmoid(second[:, RWKV_WIDTH:2 * RWKV_WIDTH] + vec(_V_A0))
            put("gate", second[:, 2 * RWKV_WIDTH:3 * RWKV_WIDTH])
            pp["kk"] = pp["k"] * vec(_V_KK)
            pp["kk_ss"] = head_sum(pp["kk"] * pp["kk"])
            pp["cum"] = _split_dot(tril_ref[...], pp["ld"])

        def prep_c():
            pp["kk"] = pp["kk"] * lax.rsqrt(jnp.maximum(pp["kk_ss"], 1e-24))
            pp["k2"] = pp["k"] * (1.0 + (pp["a"] - 1.0) * vec(_V_KA))
            put("bonus", head_sum(pp["r"] * pp["k2"] * vec(_V_RK)) * pp["v"])

        def prep_d():
            cum, ld, kk, k2 = pp["cum"], pp["ld"], pp["kk"], pp["k2"]
            b = kk * pp["a"]
            cum_end = jnp.concatenate(
                [jnp.broadcast_to(cum[(c + 1) * C - 1:(c + 1) * C, :], (C, RWKV_WIDTH))
                 for c in range(STG // C)], axis=0)
            e_neg = jnp.exp(-cum)
            e_end = jnp.exp(cum_end - cum)
            put("a_t", -kk * jnp.exp(cum - ld))
            put("r_t", pp["r"] * jnp.exp(cum))
            put("b_t", b * e_neg)
            put("k_t", k2 * e_neg)
            put("b_w", b * e_end)
            put("k_w", k2 * e_end)
            put("w_c", jnp.exp(cum_end))

        return pp, [prep_a, prep_b, prep_c, prep_d]

    def run_all(thunks):
        for thunk in thunks:
            thunk()

    zero_prev = lambda: (jnp.zeros((1, N_RKV), F32), jnp.zeros((1, N_LORA), F32))

    @pl.when((pl.program_id(0) == 0) & (pl.program_id(1) == 0))
    def _():
        got, pieces = project(xfirst_ref.at[stage_rows[0], :], 0)
        run_all(pieces)
        run_all(prepare(got, zero_prev, store=True)[1])
        run_all(project(xfirst_ref.at[stage_rows[1], :], 1)[1])

    ret_state = [rstate_ref[h] for h in range(RET_HEADS)]

    def ret_unit(c, h):
        RC, RD = RET_CHUNK, RET_HEAD_DIM
        s, c_loc = divmod(c * RC, STG)
        rows = slice(c * RC, (c + 1) * RC)
        loc = slice(c_loc, c_loc + RC)
        row_i = lax.broadcasted_iota(jnp.int32, (RC, 1), 0).astype(F32)
        lg = _LOG_GAMMA[h]
        cos, sin = cos_ref[rows, :], sin_ref[rows, :]
        rot = lambda xh: xh * cos + pltpu.roll(xh, RD // 2, 1) * sin

        def part(j):
            c0 = j * RET_WIDTH + h * RD
            if s == 0:
                return pret0_ref[loc, c0:c0 + RD]
            return proj1_ref[loc, N_RKV + c0:N_RKV + c0 + RD]

        q = rot(part(0))
        kr = rot(part(1)) * (RD ** -0.5)
        vr = part(2)
        gt = part(3)
        qb, kb, vb = q.astype(BF16), kr.astype(BF16), vr.astype(BF16)
        scores = (_dot_nt(qb, kb) * dmask_ref[h]).astype(BF16)
        kv = _dot_tn(kb, (vr * jnp.exp((RC - 1.0 - row_i) * lg)).astype(BF16))
        inter = _dot((q * jnp.exp((row_i + 1.0) * lg)).astype(BF16), ret_state[h].astype(BF16))
        ret_state[h] = ret_state[h] * _GAMMA_C[h] + kv

        def second_half():
            y = _dot(scores, vb) + inter
            mu = jnp.mean(y, axis=-1, keepdims=True)
            d = y - mu
            var = jnp.mean(d * d, axis=-1, keepdims=True)
            yn = d * lax.rsqrt(var + RET_GN_EPS) * gnw_ref[:, h * RD:(h + 1) * RD]
            yret_ref[rows, h * RD:(h + 1) * RD] = (gt * jax.nn.sigmoid(gt) * yn).astype(BF16)
        return second_half

    ret_units = [(c, h) for c in range(TT // RET_CHUNK) for h in range(RET_HEADS)]
    ret_pending = []
    slot_queue = []

    def fill():
        if ret_pending:
            ret_pending.pop(0)()
        if ret_units:
            ret_pending.append(ret_unit(*ret_units.pop(0)))
        if slot_queue:
            for thunk in slot_queue.pop(0):
                thunk()

    c_bits = C.bit_length() - 1
    hd_bits = HD.bit_length() - 1
    assert C == 1 << c_bits and HD == 1 << hd_bits and W == GC
    sr = lax.broadcasted_iota(jnp.int32, (GC, W), 0)
    sl_ = lax.broadcasted_iota(jnp.int32, (GC, W), 1)
    stack_mask = (sr >> c_bits) == (sl_ >> hd_bits)
    ti = lax.broadcasted_iota(jnp.int32, (C, GC), 0)
    si = lax.broadcasted_iota(jnp.int32, (C, GC), 1) & (C - 1)
    strict = ti > si
    incl = ti >= si
    eye_c = (ti == si).astype(F32)
    wi = lax.broadcasted_iota(jnp.int32, (W, W), 0)
    wj = lax.broadcasted_iota(jnp.int32, (W, W), 1)
    eye_w = wi == wj

    def stack(x):
        return jnp.where(stack_mask, jnp.concatenate([x] * G, axis=0), 0.0)

    blk = lambda z, c, g: z[c * C:(c + 1) * C, g * W:(g + 1) * W]

    def chains_of(p):
        st = []
        for c in range(STG // C):
            for g in range(n_groups):
                a_c, r_c = blk(p["a_t"], c, g), blk(p["r_t"], c, g)
                st.append(dict(
                    ar=jnp.concatenate([a_c, r_c], axis=0).astype(BF16), r32=r_c,
                    a_s=stack(a_c).astype(BF16),
                    bk_s=jnp.concatenate([stack(blk(p["b_t"], c, g)), stack(blk(p["k_t"], c, g))],
                                         axis=0).astype(BF16),
                    bw_t=stack(blk(p["b_w"], c, g)).T.astype(BF16),
                    kw_t=stack(blk(p["k_w"], c, g)).T.astype(BF16),
                    v_s=stack(blk(p["v"], c, g)).astype(BF16), w_end=blk(p["w_c"], c, g)[0:1, :]))
        for s in st:
            prod = _dot_nt(s["ar"], s["bk_s"])
            s["n"] = jnp.where(strict, prod[0:C, 0:GC], 0.0)
            s["a_ak"] = jnp.where(strict, prod[0:C, GC:], 0.0).astype(BF16)
            s["a_rb"] = jnp.where(incl, prod[C:, 0:GC], 0.0).astype(BF16)
            s["a_rk"] = jnp.where(incl, prod[C:, GC:], 0.0).astype(BF16)
        fill()
        for s in st:
            res = _dot(jnp.concatenate([s["a_ak"], s["a_rk"], s["kw_t"]], axis=0), s["v_s"])
            s["av_s"] = stack(res[0:C]).astype(BF16)
            s["rkv"] = res[C:2 * C]
            s["kwv"] = res[2 * C:]
        fill()
        for s in st:
            nb = s["n"].astype(BF16)
            s["t"] = eye_c + s["n"]
            s["p"] = _dot(nb, stack(s["n"]).astype(BF16))
        fill()
        for i in range(1, 6):
            for s in st:
                p_s = stack(s["p"]).astype(BF16)
                if i < 5:
                    res = _dot(jnp.concatenate([s["t"], s["p"]], axis=0).astype(BF16), p_s)
                    s["t"] = s["t"] + res[0:C]
                    s["p"] = res[C:]
                else:
                    s["t"] = (s["t"] + _dot(s["t"].astype(BF16), p_s)).astype(BF16)
            fill()
        for s in st:
            s["x1_s"] = stack(_dot(s["t"], s["a_s"])).astype(BF16)
            s["x2_s"] = stack(_dot(s["t"], s["av_s"])).astype(BF16)
        for s in st:
            lhs = jnp.concatenate([s["a_rb"], s["bw_t"]], axis=0)
            o1 = _dot(lhs, s["x1_s"])
            o2 = _dot(lhs, s["x2_s"])
            s["q"] = (s["r32"] + o1[0:C]).astype(BF16)
            s["m"] = (o1[C:] + jnp.where(eye_w, s["w_end"], 0.0)).astype(BF16)
            s["y0"] = o2[0:C] + s["rkv"]
            s["g"] = o2[C:] + s["kwv"]
        return st

    def state_steps(st, h_cur, fillers):
        y_rows = []
        n_chunks = STG // C
        per_step = -(-len(fillers) // n_chunks)
        for c in range(n_chunks):
            y_lanes = []
            for g in range(n_groups):
                s = st[c * n_groups + g]
                res = _dot(jnp.concatenate([s["q"], s["m"]], axis=0), h_cur[g].astype(BF16))
                y_lanes.append(res[0:C] + s["y0"])
                h_cur[g] = res[C:] + s["g"]
            y_rows.append(jnp.concatenate(y_lanes, axis=1))
            run_all(fillers[c * per_step:(c + 1) * per_step])
        return jnp.concatenate(y_rows, axis=0)

    def finish(rows, y, p):
        inv_hd = 1.0 / HD
        mu = head_sum(y) * inv_hd
        d = y - mu
        var = head_sum(d * d) * inv_hd
        yn = d * lax.rsqrt(var + RWKV_GN_EPS) * vec(_V_LNW) + vec(_V_LNB)
        out_ref[rows, :] = ((yn + p["bonus"]) * p["gate"]).astype(BF16)

    assert n_stages == 2
    h_cur = [h_ref[g] for g in range(n_groups)]
    prep_s0 = {name: prep0_ref.at[i] for i, name in enumerate(_PREP_NAMES)}
    prep_s0["bonus"], prep_s0["gate"] = prep_s0["bonus"][...], prep_s0["gate"][...]
    last_step_of_row = pl.program_id(1) == pl.num_programs(1) - 1

    def chains_with_fillers(p, thunks):
        for t in thunks:
            slot_queue.extend([[t], []])
        st = chains_of(p)
        while slot_queue:
            fill()
        return st

    prev_s1 = lambda: (prev_ref[SUBLANES - 1:SUBLANES, 0:N_RKV],
                       prev_ref[SUBLANES - 1:SUBLANES, N_RKV:N_RKV + N_LORA])
    prep_s1, thunks = prepare(None, prev_s1, store=False)
    chains_s0 = chains_with_fillers(prep_s0, thunks)

    got_n0, pieces = project(xnext_ref.at[stage_rows[0], :], 0)
    y_s0 = state_steps(chains_s0, h_cur, pieces)

    prev_n0 = lambda: (jnp.where(last_step_of_row, 0.0, proj1_ref[STG - 1:STG, 0:N_RKV]),
                       jnp.where(last_step_of_row, 0.0, proj1_ref[STG - 1:STG, N_RKV + N_RET:N_PROJ]))
    chains_s1 = chains_with_fillers(prep_s1, prepare(got_n0, prev_n0, store=True)[1])

    finish(stage_rows[0], y_s0, prep_s0)
    y_s1 = state_steps(chains_s1, h_cur, project(xnext_ref.at[stage_rows[1], :], 1)[1])
    finish(stage_rows[1], y_s1, prep_s1)
    while ret_units or ret_pending:
        fill()
    for g in range(n_groups):
        h_ref[g] = h_cur[g]
    for h in range(RET_HEADS):
        rstate_ref[h] = ret_state[h]


def _mixer(x2, g_mix, w_in, w_low, vecs, w2cat, seg, tril, freq, gn_w, bsz, seq):
    tt = MIX_TILE
    nt = seq // tt
    w = RWKV_GROUP_HEADS * RWKV_HEAD_DIM
    n_groups = RWKV_WIDTH // w
    cos, sin, dmask = _ret_tables(freq, seq)
    row = lambda b, t: (b * nt + t, 0)
    next_tile = lambda b, t: (jnp.minimum(b * nt + t + 1, bsz * nt - 1), 0)
    pos = lambda b, t: (t, 0)
    const = lambda b, t: (0, 0)
    return pl.pallas_call(
        _mixer_kernel,
        grid=(bsz, nt),
        in_specs=[
            pl.BlockSpec((tt, D_MODEL), const),
            pl.BlockSpec((tt, D_MODEL), next_tile),
            pl.BlockSpec((1, D_MODEL), const),
            pl.BlockSpec((D_MODEL, N_RKV + N_RET), const, pipeline_mode=pl.Buffered(1)),
            pl.BlockSpec((D_MODEL, N_LORA), const, pipeline_mode=pl.Buffered(1)),
            pl.BlockSpec((_N_VEC_ROWS, RWKV_WIDTH), const),
            pl.BlockSpec((LORA_WIDTH, 3 * RWKV_WIDTH), const),
            pl.BlockSpec((LANES, LANES), const),
            pl.BlockSpec((MIX_STAGE, MIX_STAGE), const),
            pl.BlockSpec((tt, RET_HEAD_DIM), pos),
            pl.BlockSpec((tt, RET_HEAD_DIM), pos),
            pl.BlockSpec((RET_HEADS, RET_CHUNK, RET_CHUNK), lambda b, t: (0, 0, 0)),
            pl.BlockSpec((1, RET_WIDTH), const),
        ],
        out_specs=[pl.BlockSpec((tt, RWKV_WIDTH), row), pl.BlockSpec((tt, RET_WIDTH), row)],
        out_shape=[jax.ShapeDtypeStruct((bsz * seq, RWKV_WIDTH), BF16),
                   jax.ShapeDtypeStruct((bsz * seq, RET_WIDTH), BF16)],
        scratch_shapes=[pltpu.VMEM((n_groups, w, w), F32),
                        pltpu.VMEM((RET_HEADS, RET_HEAD_DIM, RET_HEAD_DIM), F32),
                        pltpu.VMEM((SUBLANES, N_RKV + N_LORA), F32),
                        pltpu.VMEM((len(_PREP_NAMES), MIX_STAGE, RWKV_WIDTH), F32),
                        pltpu.VMEM((MIX_STAGE, N_RET), F32),
                        pltpu.VMEM((MIX_STAGE, N_PROJ), F32)],
        compiler_params=pltpu.CompilerParams(
            dimension_semantics=("arbitrary", "arbitrary"), vmem_limit_bytes=_vmem_limit(56 * 1024 * 1024)),
        name="mixer",
    )(x2, x2, g_mix, w_in, w_low, vecs, w2cat, seg, tril, cos, sin, dmask, gn_w)


def _ffn_kernel(yr_ref, yt_ref, x_ref, wo_ref, gf_ref, wg_ref, wu_ref, cw_ref, cb_ref, wd_ref, gl_ref,
                out_ref, carry_ref):
    sm = FFN_TILE // FFN_STAGES
    stages = [slice(i * sm, (i + 1) * sm) for i in range(FFN_STAGES)]

    @pl.when(pl.program_id(1) == 0)
    def _():
        carry_ref[...] = jnp.zeros_like(carry_ref)

    mix = [_dot(yr_ref[s, :], wo_ref[0:RWKV_WIDTH, :]) + _dot(yt_ref[s, :], wo_ref[RWKV_WIDTH:, :])
           for s in stages]
    acc = [x_ref[s, :] + m for s, m in zip(stages, mix)]
    hb = [_rms_norm(x1, gf_ref[...]).astype(BF16) for x1 in acc]

    cols = D_FF // FFN_COL_SPLIT
    for j in range(FFN_COL_SPLIT):
        cs = slice(j * cols, (j + 1) * cols)
        gate_up = [(_dot(h, wg_ref[:, cs]), _dot(h, wu_ref[:, cs])) for h in hb]
        prev = carry_ref[:, cs]
        p2, p1 = prev[SUBLANES - 2:SUBLANES - 1], prev[SUBLANES - 1:SUBLANES]
        hidden = []
        for gate, up in gate_up:
            row = lax.broadcasted_iota(jnp.int32, gate.shape, 0)
            g1 = jnp.where(row == 0, p1, pltpu.roll(gate, 1, 0))
            g2 = jnp.where(row == 0, p2, jnp.where(row == 1, p1, pltpu.roll(gate, 2, 0)))
            p2, p1 = gate[sm - 2:sm - 1], gate[sm - 1:sm]
            conv = cw_ref[0:1, cs] * g2 + cw_ref[1:2, cs] * g1 + cw_ref[2:3, cs] * gate + cb_ref[:, cs]
            hidden.append((conv * jax.nn.sigmoid(conv) * up).astype(BF16))
        carry_ref[:, cs] = gate_up[-1][0][sm - SUBLANES:sm]
        acc = [a + _dot(h, wd_ref[cs, :]) for a, h in zip(acc, hidden)]
    for s, a in zip(stages, acc):
        out_ref[s, :] = _rms_norm(a, gl_ref[...])


def _ffn(y_rwkv, y_ret, x2, w_out, g_ffn, w_gate, w_up, conv_w, conv_b, w_down, g_final, bsz, seq):
    tm = FFN_TILE
    nt = seq // tm
    row = lambda b, t: (b * nt + t, 0)
    const = lambda b, t: (0, 0)
    single = dict(pipeline_mode=pl.Buffered(1))
    weights = (D_MODEL * D_MODEL + 3 * D_MODEL * D_FF) * 2
    vmem = weights + 4 * tm * D_MODEL * 4 + 8 * tm * D_FF * 4 + 8 * 1024 * 1024
    return pl.pallas_call(
        _ffn_kernel,
        grid=(bsz, nt),
        in_specs=[
            pl.BlockSpec((tm, RWKV_WIDTH), row),
            pl.BlockSpec((tm, RET_WIDTH), row),
            pl.BlockSpec((tm, D_MODEL), row),
            pl.BlockSpec((D_MODEL, D_MODEL), const, **single),
            pl.BlockSpec((1, D_MODEL), const),
            pl.BlockSpec((D_MODEL, D_FF), const, **single),
            pl.BlockSpec((D_MODEL, D_FF), const, **single),
            pl.BlockSpec((SUBLANES, D_FF), const),
            pl.BlockSpec((1, D_FF), const),
            pl.BlockSpec((D_FF, D_MODEL), const, **single),
            pl.BlockSpec((1, D_MODEL), const),
        ],
        out_specs=pl.BlockSpec((tm, D_MODEL), row),
        out_shape=jax.ShapeDtypeStruct((bsz * seq, D_MODEL), F32),
        scratch_shapes=[pltpu.VMEM((SUBLANES, D_FF), F32)],
        compiler_params=pltpu.CompilerParams(
            dimension_semantics=("arbitrary", "arbitrary"), vmem_limit_bytes=_vmem_limit(vmem)),
        name="ffn",
    )(y_rwkv, y_ret, x2, w_out, g_ffn, w_gate, w_up, conv_w, conv_b, w_down, g_final)


def _block_ones(n, block):
    i = jnp.arange(n)
    return ((i[:, None] // block) == (i[None, :] // block)).astype(BF16)


def kernel(x, norm_mix_g, w_in, rwkv_mu_r, rwkv_mu_k, rwkv_mu_v, rwkv_mu_w, rwkv_mu_a, rwkv_mu_g, rwkv_w0, rwkv_w1, rwkv_w2, rwkv_a0, rwkv_a1, rwkv_a2, rwkv_g1, rwkv_g2, rwkv_k_k, rwkv_k_a, rwkv_r_k, rwkv_lnx_w, rwkv_lnx_b, ret_gn_w, w_out, norm_ffn_g, ffn_w_gate, ffn_w_up, ffn_conv_w, ffn_conv_b, ffn_w_down, norm_final_g):
    bsz, seq, d = x.shape
    assert d == D_MODEL and seq % MIX_TILE == 0 and seq % FFN_TILE == 0
    assert norm_mix_g.shape[0] == 1, "one layer"
    x2 = x.reshape(bsz * seq, d)
    row = lambda p: p.reshape(1, -1)

    w_cat = jnp.concatenate([rwkv_w1[0], rwkv_a1[0], rwkv_g1[0]], axis=1)
    mu_cat = jnp.concatenate([
        jnp.broadcast_to(rwkv_mu_w[0][:, None], (d, DECAY_LORA)),
        jnp.broadcast_to(rwkv_mu_a[0][:, None], (d, AAA_LORA)),
        jnp.broadcast_to(rwkv_mu_g[0][:, None], (d, GATE_LORA))], axis=1)
    w_low = _fold_lora(w_cat, mu_cat)

    vec_rows = [rwkv_mu_r[0], rwkv_mu_k[0], rwkv_mu_v[0], rwkv_w0[0], rwkv_a0[0], rwkv_k_k[0], rwkv_k_a[0],
                rwkv_r_k[0].reshape(-1), rwkv_lnx_w[0], rwkv_lnx_b[0]]
    vecs = jnp.zeros((_N_VEC_ROWS, RWKV_WIDTH), F32).at[:len(vec_rows)].set(jnp.stack(vec_rows))
    w2cat = jnp.zeros((LORA_WIDTH, 3 * RWKV_WIDTH), BF16)
    w2cat = w2cat.at[0:DECAY_LORA, 0:RWKV_WIDTH].set(rwkv_w2[0].astype(BF16))
    w2cat = w2cat.at[DECAY_LORA:DECAY_LORA + AAA_LORA, RWKV_WIDTH:2 * RWKV_WIDTH].set(rwkv_a2[0].astype(BF16))
    w2cat = w2cat.at[DECAY_LORA + AAA_LORA:, 2 * RWKV_WIDTH:].set(rwkv_g2[0].astype(BF16))
    seg = _block_ones(LANES, RWKV_HEAD_DIM)
    ti = jnp.arange(MIX_STAGE)
    cones = _block_ones(MIX_STAGE, RWKV_CHUNK)
    tril = (cones.astype(F32) * (ti[:, None] >= ti[None, :])).astype(BF16)
    half = RET_HEAD_DIM // 2
    inv_freq = ROPE_BASE ** (-jnp.arange(half, dtype=F32) / half)
    freq = jnp.concatenate([inv_freq, inv_freq]).reshape(1, RET_HEAD_DIM)
    y_rwkv, y_ret = _mixer(x2, row(norm_mix_g[0]), w_in[0].astype(BF16), w_low, vecs, w2cat, seg, tril, freq,
                           row(ret_gn_w[0]),
                           bsz, seq)

    conv_w = jnp.zeros((SUBLANES, D_FF), F32).at[0:3].set(ffn_conv_w[0][:, 0, :])
    out = _ffn(y_rwkv, y_ret, x2, w_out[0].astype(BF16), row(norm_ffn_g[0]), ffn_w_gate[0].astype(BF16),
               ffn_w_up[0].astype(BF16), conv_w, row(ffn_conv_b[0]), ffn_w_down[0].astype(BF16),
               row(norm_final_g), bsz, seq)
    return out.reshape(bsz, seq, d)
```

```python
import math

import jax
import jax.numpy as jnp
import numpy as np
from jax import lax
from jax.experimental import pallas as pl
from jax.experimental.pallas import tpu as pltpu

F32 = jnp.float32
BF16 = jnp.bfloat16

D_MODEL = 1024
RWKV_HEADS = 8
RWKV_HEAD_DIM = 64
RWKV_WIDTH = 512
RET_HEADS = 4
RET_HEAD_DIM = 128
RET_WIDTH = 512
DECAY_LORA = 64
AAA_LORA = 64
GATE_LORA = 128
LORA_WIDTH = DECAY_LORA + AAA_LORA + GATE_LORA
RET_CHUNK = 128
ROPE_BASE = 10000.0
D_FF = 2816
NORM_EPS = 1e-6
RWKV_GN_EPS = 64e-5
RET_GN_EPS = 1e-5

V7X_VMEM_BYTES = 64 * 1024 * 1024
SUBLANES = 8
LANES = 128

RWKV_CHUNK = 64
RWKV_GROUP_HEADS = 2
MIX_TILE = 512
MIX_STAGE = 256
PROJ_PIECE = 512
FFN_TILE = 512
FFN_STAGES = 2
FFN_COL_SPLIT = 2

N_RKV = 3 * RWKV_WIDTH
N_RET = 4 * RET_WIDTH
N_LORA = 2 * LORA_WIDTH
N_PROJ = N_RKV + N_RET + N_LORA

_LOG_GAMMA = [math.log(1.0 - 2.0 ** (-5.0 - h)) for h in range(RET_HEADS)]
_GAMMA_C = [math.exp(RET_CHUNK * lg) for lg in _LOG_GAMMA]


def _vmem_limit(nbytes):
    return int(min(nbytes, V7X_VMEM_BYTES - 4 * 1024 * 1024))


def _dot(a, b):
    return jnp.dot(a, b, preferred_element_type=F32)


def _dot_nt(a, b):
    return lax.dot_general(a, b, (((1,), (1,)), ((), ())), preferred_element_type=F32)


def _dot_tn(a, b):
    return lax.dot_general(a, b, (((0,), (0,)), ((), ())), preferred_element_type=F32)


def _split_dot(mat_bf16, x):
    hi = x.astype(BF16)
    lo = (x - hi.astype(F32)).astype(BF16)
    return _dot(mat_bf16, hi) + _dot(mat_bf16, lo)


def _shift_rows(x, prev_row):
    rolled = pltpu.roll(x, 1, 0)
    row = lax.broadcasted_iota(jnp.int32, x.shape, 0)
    return jnp.where(row == 0, prev_row, rolled)


def _rms_norm(x, g):
    ms = jnp.mean(x * x, axis=-1, keepdims=True)
    return x * lax.rsqrt(ms + NORM_EPS) * g


def _fold_lora_kernel(w1_ref, a1_ref, g1_ref, muw_ref, mua_ref, mug_ref, o_ref):
    c0 = 0
    for w_ref, mu_ref in ((w1_ref, muw_ref), (a1_ref, mua_ref), (g1_ref, mug_ref)):
        w, mu = w_ref[...], mu_ref[...]
        n = w.shape[1]
        o_ref[:, c0:c0 + n] = (w * (1.0 - mu)).astype(BF16)
        o_ref[:, LORA_WIDTH + c0:LORA_WIDTH + c0 + n] = (w * mu).astype(BF16)
        c0 += n


def _fold_lora(w1, a1, g1, mu_w, mu_a, mu_g):
    col = lambda v: v.reshape(-1, 1)
    return pl.pallas_call(
        _fold_lora_kernel,
        out_shape=jax.ShapeDtypeStruct((D_MODEL, 2 * LORA_WIDTH), BF16),
        name="fold_lora",
    )(w1, a1, g1, col(mu_w), col(mu_a), col(mu_g))


def _ret_tables_kernel(freq_ref, cos_ref, sin_ref, dmask_ref):
    C, HD = RET_CHUNK, RET_HEAD_DIM
    n = cos_ref.shape[0]
    ang = lax.broadcasted_iota(jnp.int32, (n, HD), 0).astype(F32) * freq_ref[...]
    lane = lax.broadcasted_iota(jnp.int32, (n, HD), 1)
    sin = jnp.sin(ang)
    cos_ref[...] = jnp.cos(ang)
    sin_ref[...] = jnp.where(lane < HD // 2, -sin, sin)
    ii = lax.broadcasted_iota(jnp.int32, (C, C), 0)
    jj = lax.broadcasted_iota(jnp.int32, (C, C), 1)
    diff = jnp.maximum((ii - jj).astype(F32), 0.0)
    for h in range(RET_HEADS):
        dmask_ref[h] = jnp.where(ii >= jj, jnp.exp(diff * _LOG_GAMMA[h]), 0.0)


def _ret_tables(freq, seq):
    return pl.pallas_call(
        _ret_tables_kernel,
        out_shape=[jax.ShapeDtypeStruct((seq, RET_HEAD_DIM), F32),
                   jax.ShapeDtypeStruct((seq, RET_HEAD_DIM), F32),
                   jax.ShapeDtypeStruct((RET_HEADS, RET_CHUNK, RET_CHUNK), F32)],
        name="ret_tables",
    )(freq)


(_V_MU_R, _V_MU_K, _V_MU_V, _V_W0, _V_A0, _V_KK, _V_KA, _V_RK, _V_LNW, _V_LNB) = range(10)
_N_VEC_ROWS = 16

_PREP_NAMES = ("a_t", "r_t", "b_t", "k_t", "b_w", "k_w", "w_c", "v", "bonus", "gate")

_PROJ_PIECES = ([("rkv", c) for c in range(0, N_RKV, PROJ_PIECE)]
                + [("ret", c) for c in range(N_RKV, N_RKV + N_RET, PROJ_PIECE)]
                + [("lora", c) for c in range(N_RKV + N_RET, N_PROJ, PROJ_PIECE)])


def _mixer_kernel(xfirst_ref, xnext_ref, gmix_ref, win_ref, wlo_ref, vec_ref, w2_ref, seg_ref, tril_ref, cos_ref,
                  sin_ref, dmask_ref, gnw_ref, out_ref, yret_ref,
                  h_ref, rstate_ref, prev_ref, prep0_ref, pret0_ref, proj1_ref, prj0_ref):
    C, G, TT, STG = RWKV_CHUNK, RWKV_GROUP_HEADS, MIX_TILE, MIX_STAGE
    HD = RWKV_HEAD_DIM
    W = G * HD
    GC = G * C
    n_groups = RWKV_WIDTH // W
    n_stages = TT // STG
    stage_rows = [slice(i * STG, (i + 1) * STG) for i in range(n_stages)]

    @pl.when(pl.program_id(1) == 0)
    def _():
        h_ref[...] = jnp.zeros_like(h_ref)
        rstate_ref[...] = jnp.zeros_like(rstate_ref)

    vec = lambda i: vec_ref[i:i + 1, :]

    seg = seg_ref[...]

    def head_sum(x):
        n_blk = x.shape[1] // LANES
        n_rows = x.shape[0]
        rows = jnp.concatenate([x[:, j * LANES:(j + 1) * LANES] for j in range(n_blk)], axis=0)
        s = _dot(rows.astype(BF16), seg)
        return jnp.concatenate([s[j * n_rows:(j + 1) * n_rows] for j in range(n_blk)], axis=1)

    def project(x_rows_ref, stage):
        got = {}

        def piece(i):
            def run():
                if "xb" not in got:
                    got["xb"] = _rms_norm(x_rows_ref[...], gmix_ref[...]).astype(BF16)
                name, c0 = _PROJ_PIECES[i]
                n_in = N_RKV + N_RET
                w = win_ref[:, c0:c0 + PROJ_PIECE] if c0 < n_in else wlo_ref[:, c0 - n_in:c0 - n_in + PROJ_PIECE]
                res = _dot(got["xb"], w)
                if stage == 1:
                    proj1_ref[:, c0:c0 + PROJ_PIECE] = res
                elif name == "ret":
                    pret0_ref[:, c0 - N_RKV:c0 - N_RKV + PROJ_PIECE] = res
                else:
                    d0 = c0 if name == "rkv" else c0 - N_RET
                    prj0_ref[:, d0:d0 + PROJ_PIECE] = res
            return run
        front = [piece(i) for i, (name, _) in enumerate(_PROJ_PIECES) if name != "ret"]
        ret = [piece(i) for i, (name, _) in enumerate(_PROJ_PIECES) if name == "ret"]
        return front, ret

    def prepare(stage, prev_rows, store):
        pp = {}

        def put(name, val):
            if store:
                prep0_ref[_PREP_NAMES.index(name)] = val
            else:
                pp[name] = val

        def prep_a():
            if stage == 1:
                p_rkv, p_lora = proj1_ref[:, 0:N_RKV], proj1_ref[:, N_RKV + N_RET:N_PROJ]
            else:
                p_rkv, p_lora = prj0_ref[:, 0:N_RKV], prj0_ref[:, N_RKV:N_RKV + N_LORA]
            prev_rkv, prev_lora = prev_rows()
            s_rkv = _shift_rows(p_rkv, prev_rkv)
            lerp = lambda j, mu: (p_rkv[:, j * RWKV_WIDTH:(j + 1) * RWKV_WIDTH]
                                  + (s_rkv[:, j * RWKV_WIDTH:(j + 1) * RWKV_WIDTH]
                                     - p_rkv[:, j * RWKV_WIDTH:(j + 1) * RWKV_WIDTH]) * mu)
            if store:
                prev_ref[:, 0:N_RKV] = p_rkv[STG - SUBLANES:]
                prev_ref[:, N_RKV:N_RKV + N_LORA] = p_lora[STG - SUBLANES:]
            pp["r"] = lerp(0, vec(_V_MU_R))
            pp["k"] = lerp(1, vec(_V_MU_K))
            pp["v"] = lerp(2, vec(_V_MU_V))
            put("v", pp["v"])
            low = p_lora[:, 0:LORA_WIDTH] + _shift_rows(p_lora, prev_lora)[:, LORA_WIDTH:2 * LORA_WIDTH]
            lane = lax.broadcasted_iota(jnp.int32, low.shape, 1)
            act = jnp.where(lane < DECAY_LORA, jnp.tanh(low),
                            jnp.where(lane < DECAY_LORA + AAA_LORA, low, jax.nn.sigmoid(low)))
            pp["second"] = _dot(act.astype(BF16), w2_ref[...])

        def prep_b():
            second = pp["second"]
            pp["ld"] = -math.exp(-0.5) * jax.nn.sigmoid(second[:, 0:RWKV_WIDTH] + vec(_V_W0))
            pp["a"] = jax.nn.sigmoid(second[:, RWKV_WIDTH:2 * RWKV_WIDTH] + vec(_V_A0))
            put("gate", second[:, 2 * RWKV_WIDTH:3 * RWKV_WIDTH])
            pp["kk"] = pp["k"] * vec(_V_KK)
            pp["kk_ss"] = head_sum(pp["kk"] * pp["kk"])
            pp["cum"] = _split_dot(tril_ref[...], pp["ld"])

        def prep_c():
            pp["kk"] = pp["kk"] * lax.rsqrt(jnp.maximum(pp["kk_ss"], 1e-24))
            pp["k2"] = pp["k"] * (1.0 + (pp["a"] - 1.0) * vec(_V_KA))
            put("bonus", head_sum(pp["r"] * pp["k2"] * vec(_V_RK)) * pp["v"])

        def prep_d():
            cum, ld, kk, k2 = pp["cum"], pp["ld"], pp["kk"], pp["k2"]
            b = kk * pp["a"]
            cum_end = jnp.concatenate(
                [jnp.broadcast_to(cum[(c + 1) * C - 1:(c + 1) * C, :], (C, RWKV_WIDTH))
                 for c in range(STG // C)], axis=0)
            e_neg = jnp.exp(-cum)
            e_end = jnp.exp(cum_end - cum)
            put("a_t", -kk * jnp.exp(cum - ld))
            put("r_t", pp["r"] * jnp.exp(cum))
            put("b_t", b * e_neg)
            put("k_t", k2 * e_neg)
            put("b_w", b * e_end)
            put("k_w", k2 * e_end)
            put("w_c", jnp.exp(cum_end))

        return pp, [prep_a, prep_b, prep_c, prep_d]

    def run_all(thunks):
        for thunk in thunks:
            thunk()

    zero_prev = lambda: (jnp.zeros((1, N_RKV), F32), jnp.zeros((1, N_LORA), F32))

    @pl.when((pl.program_id(0) == 0) & (pl.program_id(1) == 0))
    def _():
        for s in range(n_stages):
            front, ret = project(xfirst_ref.at[stage_rows[s], :], s)
            run_all(front + ret)
        run_all(prepare(0, zero_prev, store=True)[1])

    ret_state = [rstate_ref[h] for h in range(RET_HEADS)]

    def ret_unit(c, h):
        RC, RD = RET_CHUNK, RET_HEAD_DIM
        s, c_loc = divmod(c * RC, STG)
        rows = slice(c * RC, (c + 1) * RC)
        loc = slice(c_loc, c_loc + RC)
        row_i = lax.broadcasted_iota(jnp.int32, (RC, 1), 0).astype(F32)
        lg = _LOG_GAMMA[h]
        cos, sin = cos_ref[rows, :], sin_ref[rows, :]
        rot = lambda xh: xh * cos + pltpu.roll(xh, RD // 2, 1) * sin

        def part(j):
            c0 = j * RET_WIDTH + h * RD
            if s == 0:
                return pret0_ref[loc, c0:c0 + RD]
            return proj1_ref[loc, N_RKV + c0:N_RKV + c0 + RD]

        q = rot(part(0))
        kr = rot(part(1)) * (RD ** -0.5)
        vr = part(2)
        gt = part(3)
        qb, kb, vb = q.astype(BF16), kr.astype(BF16), vr.astype(BF16)
        scores = (_dot_nt(qb, kb) * dmask_ref[h]).astype(BF16)
        kv = _dot_tn(kb, (vr * jnp.exp((RC - 1.0 - row_i) * lg)).astype(BF16))
        inter = _dot((q * jnp.exp((row_i + 1.0) * lg)).astype(BF16), ret_state[h].astype(BF16))
        ret_state[h] = ret_state[h] * _GAMMA_C[h] + kv

        def second_half():
            y = _dot(scores, vb) + inter
            mu = jnp.mean(y, axis=-1, keepdims=True)
            d = y - mu
            var = jnp.mean(d * d, axis=-1, keepdims=True)
            yn = d * lax.rsqrt(var + RET_GN_EPS) * gnw_ref[:, h * RD:(h + 1) * RD]
            yret_ref[rows, h * RD:(h + 1) * RD] = (gt * jax.nn.sigmoid(gt) * yn).astype(BF16)
        return second_half

    ret_units = [(c, h) for c in range(TT // RET_CHUNK) for h in range(RET_HEADS)]
    ret_pending = []
    slot_queue = []

    def fill():
        if ret_pending:
            ret_pending.pop(0)()
        if ret_units:
            ret_pending.append(ret_unit(*ret_units.pop(0)))
        if slot_queue:
            for thunk in slot_queue.pop(0):
                thunk()

    c_bits = C.bit_length() - 1
    hd_bits = HD.bit_length() - 1
    assert C == 1 << c_bits and HD == 1 << hd_bits and W == GC
    sr = lax.broadcasted_iota(jnp.int32, (GC, W), 0)
    sl_ = lax.broadcasted_iota(jnp.int32, (GC, W), 1)
    stack_mask = (sr >> c_bits) == (sl_ >> hd_bits)
    ti = lax.broadcasted_iota(jnp.int32, (C, GC), 0)
    si = lax.broadcasted_iota(jnp.int32, (C, GC), 1) & (C - 1)
    strict = ti > si
    incl = ti >= si
    eye_c = (ti == si).astype(F32)
    wi = lax.broadcasted_iota(jnp.int32, (W, W), 0)
    wj = lax.broadcasted_iota(jnp.int32, (W, W), 1)
    eye_w = wi == wj

    def stack(x):
        return jnp.where(stack_mask, jnp.concatenate([x] * G, axis=0), 0.0)

    blk = lambda z, c, g: z[c * C:(c + 1) * C, g * W:(g + 1) * W]

    def chains_of(p):
        st = []
        for c in range(STG // C):
            for g in range(n_groups):
                a_c, r_c = blk(p["a_t"], c, g), blk(p["r_t"], c, g)
                st.append(dict(
                    ar=jnp.concatenate([a_c, r_c], axis=0).astype(BF16), r32=r_c,
                    a_s=stack(a_c).astype(BF16),
                    bk_s=jnp.concatenate([stack(blk(p["b_t"], c, g)), stack(blk(p["k_t"], c, g))],
                                         axis=0).astype(BF16),
                    bw_t=stack(blk(p["b_w"], c, g)).T.astype(BF16),
                    kw_t=stack(blk(p["k_w"], c, g)).T.astype(BF16),
                    v_s=stack(blk(p["v"], c, g)).astype(BF16), w_end=blk(p["w_c"], c, g)[0:1, :]))
        for s in st:
            prod = _dot_nt(s["ar"], s["bk_s"])
            s["n"] = jnp.where(strict, prod[0:C, 0:GC], 0.0)
            s["a_ak"] = jnp.where(strict, prod[0:C, GC:], 0.0).astype(BF16)
            s["a_rb"] = jnp.where(incl, prod[C:, 0:GC], 0.0).astype(BF16)
            s["a_rk"] = jnp.where(incl, prod[C:, GC:], 0.0).astype(BF16)
        fill()
        for s in st:
            res = _dot(jnp.concatenate([s["a_ak"], s["a_rk"], s["kw_t"]], axis=0), s["v_s"])
            s["av_s"] = stack(res[0:C]).astype(BF16)
            s["rkv"] = res[C:2 * C]
            s["kwv"] = res[2 * C:]
        fill()
        for s in st:
            nb = s["n"].astype(BF16)
            s["t"] = eye_c + s["n"]
            s["p"] = _dot(nb, stack(s["n"]).astype(BF16))
        fill()
        for i in range(1, 6):
            for s in st:
                p_s = stack(s["p"]).astype(BF16)
                if i < 5:
                    res = _dot(jnp.concatenate([s["t"], s["p"]], axis=0).astype(BF16), p_s)
                    s["t"] = s["t"] + res[0:C]
                    s["p"] = res[C:]
                else:
                    s["t"] = (s["t"] + _dot(s["t"].astype(BF16), p_s)).astype(BF16)
            fill()
        for s in st:
            s["x1_s"] = stack(_dot(s["t"], s["a_s"])).astype(BF16)
            s["x2_s"] = stack(_dot(s["t"], s["av_s"])).astype(BF16)
        for s in st:
            lhs = jnp.concatenate([s["a_rb"], s["bw_t"]], axis=0)
            o1 = _dot(lhs, s["x1_s"])
            o2 = _dot(lhs, s["x2_s"])
            s["q"] = (s["r32"] + o1[0:C]).astype(BF16)
            s["m"] = (o1[C:] + jnp.where(eye_w, s["w_end"], 0.0)).astype(BF16)
            s["y0"] = o2[0:C] + s["rkv"]
            s["g"] = o2[C:] + s["kwv"]
        return st

    def state_steps(st, h_cur, fillers):
        y_rows = []
        n_chunks = STG // C
        per_step = -(-len(fillers) // n_chunks)
        for c in range(n_chunks):
            y_lanes = []
            for g in range(n_groups):
                s = st[c * n_groups + g]
                res = _dot(jnp.concatenate([s["q"], s["m"]], axis=0), h_cur[g].astype(BF16))
                y_lanes.append(res[0:C] + s["y0"])
                h_cur[g] = res[C:] + s["g"]
            y_rows.append(jnp.concatenate(y_lanes, axis=1))
            run_all(fillers[c * per_step:(c + 1) * per_step])
        return jnp.concatenate(y_rows, axis=0)

    def finish(rows, y, p):
        inv_hd = 1.0 / HD
        mu = head_sum(y) * inv_hd
        d = y - mu
        var = head_sum(d * d) * inv_hd
        yn = d * lax.rsqrt(var + RWKV_GN_EPS) * vec(_V_LNW) + vec(_V_LNB)
        out_ref[rows, :] = ((yn + p["bonus"]) * p["gate"]).astype(BF16)

    assert n_stages == 2
    h_cur = [h_ref[g] for g in range(n_groups)]
    prep_s0 = {name: prep0_ref.at[i] for i, name in enumerate(_PREP_NAMES)}
    prep_s0["bonus"], prep_s0["gate"] = prep_s0["bonus"][...], prep_s0["gate"][...]
    last_s1 = (proj1_ref[STG - 1:STG, 0:N_RKV], proj1_ref[STG - 1:STG, N_RKV + N_RET:N_PROJ])
    last_step_of_row = pl.program_id(1) == pl.num_programs(1) - 1

    def chains_with_fillers(p, thunks):
        for t in thunks:
            slot_queue.extend([[t], []])
        st = chains_of(p)
        while slot_queue:
            fill()
        return st

    def interleave(a, b):
        return [t for pair in zip(a, b) for t in pair]

    front_n0, ret_n0 = project(xnext_ref.at[stage_rows[0], :], 0)
    chains_s0 = chains_with_fillers(prep_s0, front_n0)

    prev_s1 = lambda: (prev_ref[SUBLANES - 1:SUBLANES, 0:N_RKV],
                       prev_ref[SUBLANES - 1:SUBLANES, N_RKV:N_RKV + N_LORA])
    prep_s1, thunks = prepare(1, prev_s1, store=False)
    y_s0 = state_steps(chains_s0, h_cur, interleave(thunks, ret_n0))

    front_n1, ret_n1 = project(xnext_ref.at[stage_rows[1], :], 1)
    chains_s1 = chains_with_fillers(prep_s1, front_n1)

    finish(stage_rows[0], y_s0, prep_s0)
    prev_n0 = lambda: tuple(jnp.where(last_step_of_row, 0.0, z) for z in last_s1)
    y_s1 = state_steps(chains_s1, h_cur, interleave(prepare(0, prev_n0, store=True)[1], ret_n1))
    finish(stage_rows[1], y_s1, prep_s1)
    while ret_units or ret_pending:
        fill()
    for g in range(n_groups):
        h_ref[g] = h_cur[g]
    for h in range(RET_HEADS):
        rstate_ref[h] = ret_state[h]


def _mixer(x2, g_mix, w_in, w_low, vecs, w2cat, seg, tril, freq, gn_w, bsz, seq):
    tt = MIX_TILE
    nt = seq // tt
    w = RWKV_GROUP_HEADS * RWKV_HEAD_DIM
    n_groups = RWKV_WIDTH // w
    cos, sin, dmask = _ret_tables(freq, seq)
    row = lambda b, t: (b * nt + t, 0)
    next_tile = lambda b, t: (jnp.minimum(b * nt + t + 1, bsz * nt - 1), 0)
    pos = lambda b, t: (t, 0)
    const = lambda b, t: (0, 0)
    return pl.pallas_call(
        _mixer_kernel,
        grid=(bsz, nt),
        in_specs=[
            pl.BlockSpec((tt, D_MODEL), const),
            pl.BlockSpec((tt, D_MODEL), next_tile),
            pl.BlockSpec((1, D_MODEL), const),
            pl.BlockSpec((D_MODEL, N_RKV + N_RET), const, pipeline_mode=pl.Buffered(1)),
            pl.BlockSpec((D_MODEL, N_LORA), const, pipeline_mode=pl.Buffered(1)),
            pl.BlockSpec((_N_VEC_ROWS, RWKV_WIDTH), const),
            pl.BlockSpec((LORA_WIDTH, 3 * RWKV_WIDTH), const),
            pl.BlockSpec((LANES, LANES), const),
            pl.BlockSpec((MIX_STAGE, MIX_STAGE), const),
            pl.BlockSpec((tt, RET_HEAD_DIM), pos),
            pl.BlockSpec((tt, RET_HEAD_DIM), pos),
            pl.BlockSpec((RET_HEADS, RET_CHUNK, RET_CHUNK), lambda b, t: (0, 0, 0)),
            pl.BlockSpec((1, RET_WIDTH), const),
        ],
        out_specs=[pl.BlockSpec((tt, RWKV_WIDTH), row), pl.BlockSpec((tt, RET_WIDTH), row)],
        out_shape=[jax.ShapeDtypeStruct((bsz * seq, RWKV_WIDTH), BF16),
                   jax.ShapeDtypeStruct((bsz * seq, RET_WIDTH), BF16)],
        scratch_shapes=[pltpu.VMEM((n_groups, w, w), F32),
                        pltpu.VMEM((RET_HEADS, RET_HEAD_DIM, RET_HEAD_DIM), F32),
                        pltpu.VMEM((SUBLANES, N_RKV + N_LORA), F32),
                        pltpu.VMEM((len(_PREP_NAMES), MIX_STAGE, RWKV_WIDTH), F32),
                        pltpu.VMEM((MIX_STAGE, N_RET), F32),
                        pltpu.VMEM((MIX_STAGE, N_PROJ), F32),
                        pltpu.VMEM((MIX_STAGE, N_RKV + N_LORA), F32)],
        compiler_params=pltpu.CompilerParams(
            dimension_semantics=("arbitrary", "arbitrary"), vmem_limit_bytes=_vmem_limit(56 * 1024 * 1024)),
        name="mixer",
    )(x2, x2, g_mix, w_in, w_low, vecs, w2cat, seg, tril, cos, sin, dmask, gn_w)


def _ffn_kernel(yr_ref, yt_ref, x_ref, wo_ref, gf_ref, wg_ref, wu_ref, cw_ref, cb_ref, wd_ref, gl_ref,
                out_ref, carry_ref):
    sm = FFN_TILE // FFN_STAGES
    stages = [slice(i * sm, (i + 1) * sm) for i in range(FFN_STAGES)]

    @pl.when(pl.program_id(1) == 0)
    def _():
        carry_ref[...] = jnp.zeros_like(carry_ref)

    mix = [_dot(yr_ref[s, :], wo_ref[0:RWKV_WIDTH, :]) + _dot(yt_ref[s, :], wo_ref[RWKV_WIDTH:, :])
           for s in stages]
    acc = [x_ref[s, :] + m for s, m in zip(stages, mix)]
    hb = [_rms_norm(x1, gf_ref[...]).astype(BF16) for x1 in acc]

    cols = D_FF // FFN_COL_SPLIT
    for j in range(FFN_COL_SPLIT):
        cs = slice(j * cols, (j + 1) * cols)
        gate_up = [(_dot(h, wg_ref[:, cs]), _dot(h, wu_ref[:, cs])) for h in hb]
        prev = carry_ref[:, cs]
        p2, p1 = prev[SUBLANES - 2:SUBLANES - 1], prev[SUBLANES - 1:SUBLANES]
        hidden = []
        for gate, up in gate_up:
            row = lax.broadcasted_iota(jnp.int32, gate.shape, 0)
            g1 = jnp.where(row == 0, p1, pltpu.roll(gate, 1, 0))
            g2 = jnp.where(row == 0, p2, jnp.where(row == 1, p1, pltpu.roll(gate, 2, 0)))
            p2, p1 = gate[sm - 2:sm - 1], gate[sm - 1:sm]
            conv = cw_ref[0:1, cs] * g2 + cw_ref[1:2, cs] * g1 + cw_ref[2:3, cs] * gate + cb_ref[:, cs]
            hidden.append((conv * jax.nn.sigmoid(conv) * up).astype(BF16))
        carry_ref[:, cs] = gate_up[-1][0][sm - SUBLANES:sm]
        acc = [a + _dot(h, wd_ref[cs, :]) for a, h in zip(acc, hidden)]
    for s, a in zip(stages, acc):
        out_ref[s, :] = _rms_norm(a, gl_ref[...])


def _ffn(y_rwkv, y_ret, x2, w_out, g_ffn, w_gate, w_up, conv_w, conv_b, w_down, g_final, bsz, seq):
    tm = FFN_TILE
    nt = seq // tm
    row = lambda b, t: (b * nt + t, 0)
    const = lambda b, t: (0, 0)
    single = dict(pipeline_mode=pl.Buffered(1))
    weights = (D_MODEL * D_MODEL + 3 * D_MODEL * D_FF) * 2
    vmem = weights + 4 * tm * D_MODEL * 4 + 8 * tm * D_FF * 4 + 8 * 1024 * 1024
    return pl.pallas_call(
        _ffn_kernel,
        grid=(bsz, nt),
        in_specs=[
            pl.BlockSpec((tm, RWKV_WIDTH), row),
            pl.BlockSpec((tm, RET_WIDTH), row),
            pl.BlockSpec((tm, D_MODEL), row),
            pl.BlockSpec((D_MODEL, D_MODEL), const, **single),
            pl.BlockSpec((1, D_MODEL), const),
            pl.BlockSpec((D_MODEL, D_FF), const, **single),
            pl.BlockSpec((D_MODEL, D_FF), const, **single),
            pl.BlockSpec((SUBLANES, D_FF), const),
            pl.BlockSpec((1, D_FF), const),
            pl.BlockSpec((D_FF, D_MODEL), const, **single),
            pl.BlockSpec((1, D_MODEL), const),
        ],
        out_specs=pl.BlockSpec((tm, D_MODEL), row),
        out_shape=jax.ShapeDtypeStruct((bsz * seq, D_MODEL), F32),
        scratch_shapes=[pltpu.VMEM((SUBLANES, D_FF), F32)],
        compiler_params=pltpu.CompilerParams(
            dimension_semantics=("arbitrary", "arbitrary"), vmem_limit_bytes=_vmem_limit(vmem)),
        name="ffn",
    )(y_rwkv, y_ret, x2, w_out, g_ffn, w_gate, w_up, conv_w, conv_b, w_down, g_final)


def _block_ones(n, block):
    i = np.arange(n)
    return (i[:, None] // block) == (i[None, :] // block)


def kernel(x, norm_mix_g, w_in, rwkv_mu_r, rwkv_mu_k, rwkv_mu_v, rwkv_mu_w, rwkv_mu_a, rwkv_mu_g, rwkv_w0, rwkv_w1, rwkv_w2, rwkv_a0, rwkv_a1, rwkv_a2, rwkv_g1, rwkv_g2, rwkv_k_k, rwkv_k_a, rwkv_r_k, rwkv_lnx_w, rwkv_lnx_b, ret_gn_w, w_out, norm_ffn_g, ffn_w_gate, ffn_w_up, ffn_conv_w, ffn_conv_b, ffn_w_down, norm_final_g):
    bsz, seq, d = x.shape
    assert d == D_MODEL and seq % MIX_TILE == 0 and seq % FFN_TILE == 0
    assert norm_mix_g.shape[0] == 1, "one layer"
    x2 = x.reshape(bsz * seq, d)
    row = lambda p: p.reshape(1, -1)

    w_low = _fold_lora(rwkv_w1[0], rwkv_a1[0], rwkv_g1[0], rwkv_mu_w[0], rwkv_mu_a[0], rwkv_mu_g[0])

    vec_rows = [rwkv_mu_r[0], rwkv_mu_k[0], rwkv_mu_v[0], rwkv_w0[0], rwkv_a0[0], rwkv_k_k[0], rwkv_k_a[0],
                rwkv_r_k[0].reshape(-1), rwkv_lnx_w[0], rwkv_lnx_b[0]]
    vecs = jnp.zeros((_N_VEC_ROWS, RWKV_WIDTH), F32).at[:len(vec_rows)].set(jnp.stack(vec_rows))
    w2cat = jnp.zeros((LORA_WIDTH, 3 * RWKV_WIDTH), BF16)
    w2cat = w2cat.at[0:DECAY_LORA, 0:RWKV_WIDTH].set(rwkv_w2[0].astype(BF16))
    w2cat = w2cat.at[DECAY_LORA:DECAY_LORA + AAA_LORA, RWKV_WIDTH:2 * RWKV_WIDTH].set(rwkv_a2[0].astype(BF16))
    w2cat = w2cat.at[DECAY_LORA + AAA_LORA:, 2 * RWKV_WIDTH:].set(rwkv_g2[0].astype(BF16))
    seg = jnp.asarray(_block_ones(LANES, RWKV_HEAD_DIM), BF16)
    ti = np.arange(MIX_STAGE)
    tril = jnp.asarray(_block_ones(MIX_STAGE, RWKV_CHUNK) & (ti[:, None] >= ti[None, :]), BF16)
    half = RET_HEAD_DIM // 2
    inv_freq = ROPE_BASE ** (-jnp.arange(half, dtype=F32) / half)
    freq = jnp.concatenate([inv_freq, inv_freq]).reshape(1, RET_HEAD_DIM)
    y_rwkv, y_ret = _mixer(x2, row(norm_mix_g[0]), w_in[0].astype(BF16), w_low, vecs, w2cat, seg, tril, freq,
                           row(ret_gn_w[0]),
                           bsz, seq)

    conv_w = jnp.zeros((SUBLANES, D_FF), F32).at[0:3].set(ffn_conv_w[0][:, 0, :])
    out = _ffn(y_rwkv, y_ret, x2, w_out[0].astype(BF16), row(norm_ffn_g[0]), ffn_w_gate[0].astype(BF16),
               ffn_w_up[0].astype(BF16), conv_w, row(ffn_conv_b[0]), ffn_w_down[0].astype(BF16),
               row(norm_final_g), bsz, seq)
    return out.reshape(bsz, seq, d)
```

```python
import math

import jax
import jax.numpy as jnp
import numpy as np
from jax import lax
from jax.experimental import pallas as pl
from jax.experimental.pallas import tpu as pltpu

F32 = jnp.float32
BF16 = jnp.bfloat16

D_MODEL = 1024
RWKV_HEADS = 8
RWKV_HEAD_DIM = 64
RWKV_WIDTH = 512
RET_HEADS = 4
RET_HEAD_DIM = 128
RET_WIDTH = 512
DECAY_LORA = 64
AAA_LORA = 64
GATE_LORA = 128
LORA_WIDTH = DECAY_LORA + AAA_LORA + GATE_LORA
RET_CHUNK = 128
ROPE_BASE = 10000.0
D_FF = 2816
NORM_EPS = 1e-6
RWKV_GN_EPS = 64e-5
RET_GN_EPS = 1e-5

V7X_VMEM_BYTES = 64 * 1024 * 1024
SUBLANES = 8
LANES = 128

RWKV_CHUNK = 64
RWKV_GROUP_HEADS = 2
MIX_TILE = 512
MIX_STAGE = 256
PROJ_PIECE = 512
MIX_WIN_ROWS = 128
FFN_TILE = 512
FFN_STAGES = 2
FFN_COL_SPLIT = 2
FFN_WIDE_ROWS = 256
FFN_TALL_ROWS = 704

N_RKV = 3 * RWKV_WIDTH
N_RET = 4 * RET_WIDTH
N_LORA = 2 * LORA_WIDTH
N_PROJ = N_RKV + N_RET + N_LORA

_LOG_GAMMA = [math.log(1.0 - 2.0 ** (-5.0 - h)) for h in range(RET_HEADS)]
_GAMMA_C = [math.exp(RET_CHUNK * lg) for lg in _LOG_GAMMA]


def _vmem_limit(nbytes):
    return int(min(nbytes, V7X_VMEM_BYTES - 4 * 1024 * 1024))


def _dot(a, b):
    return jnp.dot(a, b, preferred_element_type=F32)


def _dot_nt(a, b):
    return lax.dot_general(a, b, (((1,), (1,)), ((), ())), preferred_element_type=F32)


def _dot_tn(a, b):
    return lax.dot_general(a, b, (((0,), (0,)), ((), ())), preferred_element_type=F32)


def _split_dot(mat_bf16, x):
    hi = x.astype(BF16)
    lo = (x - hi.astype(F32)).astype(BF16)
    return _dot(mat_bf16, hi) + _dot(mat_bf16, lo)


def _shift_rows(x, prev_row):
    rolled = pltpu.roll(x, 1, 0)
    row = lax.broadcasted_iota(jnp.int32, x.shape, 0)
    return jnp.where(row == 0, prev_row, rolled)


def _rms_norm(x, g):
    ms = jnp.mean(x * x, axis=-1, keepdims=True)
    return x * lax.rsqrt(ms + NORM_EPS) * g


def _fold_lora_kernel(w1_ref, a1_ref, g1_ref, muw_ref, mua_ref, mug_ref, o_ref):
    c0 = 0
    for w_ref, mu_ref in ((w1_ref, muw_ref), (a1_ref, mua_ref), (g1_ref, mug_ref)):
        w, mu = w_ref[...], mu_ref[...]
        n = w.shape[1]
        o_ref[:, c0:c0 + n] = (w * (1.0 - mu)).astype(BF16)
        o_ref[:, LORA_WIDTH + c0:LORA_WIDTH + c0 + n] = (w * mu).astype(BF16)
        c0 += n


def _fold_lora(w1, a1, g1, mu_w, mu_a, mu_g):
    col = lambda v: v.reshape(-1, 1)
    return pl.pallas_call(
        _fold_lora_kernel,
        out_shape=jax.ShapeDtypeStruct((D_MODEL, 2 * LORA_WIDTH), BF16),
        name="fold_lora",
    )(w1, a1, g1, col(mu_w), col(mu_a), col(mu_g))


def _ret_tables_kernel(freq_ref, cos_ref, sin_ref, dmask_ref):
    C, HD = RET_CHUNK, RET_HEAD_DIM
    n = cos_ref.shape[0]
    ang = lax.broadcasted_iota(jnp.int32, (n, HD), 0).astype(F32) * freq_ref[...]
    lane = lax.broadcasted_iota(jnp.int32, (n, HD), 1)
    sin = jnp.sin(ang)
    cos_ref[...] = jnp.cos(ang)
    sin_ref[...] = jnp.where(lane < HD // 2, -sin, sin)
    ii = lax.broadcasted_iota(jnp.int32, (C, C), 0)
    jj = lax.broadcasted_iota(jnp.int32, (C, C), 1)
    diff = jnp.maximum((ii - jj).astype(F32), 0.0)
    for h in range(RET_HEADS):
        dmask_ref[h] = jnp.where(ii >= jj, jnp.exp(diff * _LOG_GAMMA[h]), 0.0)


def _ret_tables(freq, seq):
    return pl.pallas_call(
        _ret_tables_kernel,
        out_shape=[jax.ShapeDtypeStruct((seq, RET_HEAD_DIM), F32),
                   jax.ShapeDtypeStruct((seq, RET_HEAD_DIM), F32),
                   jax.ShapeDtypeStruct((RET_HEADS, RET_CHUNK, RET_CHUNK), F32)],
        name="ret_tables",
    )(freq)


(_V_MU_R, _V_MU_K, _V_MU_V, _V_W0, _V_A0, _V_KK, _V_KA, _V_RK, _V_LNW, _V_LNB) = range(10)
_N_VEC_ROWS = 16

_PREP_NAMES = ("a_t", "r_t", "b_t", "k_t", "b_w", "k_w", "w_c", "v", "bonus", "gate")

_PROJ_PIECES = ([("rkv", c) for c in range(0, N_RKV, PROJ_PIECE)]
                + [("ret", c) for c in range(N_RKV, N_RKV + N_RET, PROJ_PIECE)]
                + [("lora", c) for c in range(N_RKV + N_RET, N_PROJ, PROJ_PIECE)])


def _mixer_kernel(xfirst_ref, xnext_ref, gmix_ref, win_hbm, wlo_ref, vec_ref, w2_ref, seg_ref, tril_ref, cos_ref,
                  sin_ref, dmask_ref, gnw_ref, out_ref, yret_ref,
                  h_ref, rstate_ref, prev_ref, prep0_ref, pret0_ref, proj1_ref, prj0_ref,
                  win_ref, wstage_ref, wsem):
    C, G, TT, STG = RWKV_CHUNK, RWKV_GROUP_HEADS, MIX_TILE, MIX_STAGE
    HD = RWKV_HEAD_DIM
    W = G * HD
    GC = G * C
    n_groups = RWKV_WIDTH // W
    n_stages = TT // STG
    stage_rows = [slice(i * STG, (i + 1) * STG) for i in range(n_stages)]

    @pl.when(pl.program_id(1) == 0)
    def _():
        h_ref[...] = jnp.zeros_like(h_ref)
        rstate_ref[...] = jnp.zeros_like(rstate_ref)

    vec = lambda i: vec_ref[i:i + 1, :]

    seg = seg_ref[...]

    def head_sum(x):
        n_blk = x.shape[1] // LANES
        n_rows = x.shape[0]
        rows = jnp.concatenate([x[:, j * LANES:(j + 1) * LANES] for j in range(n_blk)], axis=0)
        s = _dot(rows.astype(BF16), seg)
        return jnp.concatenate([s[j * n_rows:(j + 1) * n_rows] for j in range(n_blk)], axis=1)

    def project(x_rows_ref, stage):
        got = {}

        def piece(i):
            def run():
                if "xb" not in got:
                    got["xb"] = _rms_norm(x_rows_ref[...], gmix_ref[...]).astype(BF16)
                name, c0 = _PROJ_PIECES[i]
                n_in = N_RKV + N_RET
                w = win_ref[:, c0:c0 + PROJ_PIECE] if c0 < n_in else wlo_ref[:, c0 - n_in:c0 - n_in + PROJ_PIECE]
                res = _dot(got["xb"], w)
                if stage == 1:
                    proj1_ref[:, c0:c0 + PROJ_PIECE] = res
                elif name == "ret":
                    pret0_ref[:, c0 - N_RKV:c0 - N_RKV + PROJ_PIECE] = res
                else:
                    d0 = c0 if name == "rkv" else c0 - N_RET
                    prj0_ref[:, d0:d0 + PROJ_PIECE] = res
            return run
        front = [piece(i) for i, (name, _) in enumerate(_PROJ_PIECES) if name != "ret"]
        ret = [piece(i) for i, (name, _) in enumerate(_PROJ_PIECES) if name == "ret"]
        return front, ret

    def prepare(stage, prev_rows, store):
        pp = {}

        def put(name, val):
            if store:
                prep0_ref[_PREP_NAMES.index(name)] = val
            else:
                pp[name] = val

        def prep_a():
            if stage == 1:
                p_rkv, p_lora = proj1_ref[:, 0:N_RKV], proj1_ref[:, N_RKV + N_RET:N_PROJ]
            else:
                p_rkv, p_lora = prj0_ref[:, 0:N_RKV], prj0_ref[:, N_RKV:N_RKV + N_LORA]
            prev_rkv, prev_lora = prev_rows()
            s_rkv = _shift_rows(p_rkv, prev_rkv)
            lerp = lambda j, mu: (p_rkv[:, j * RWKV_WIDTH:(j + 1) * RWKV_WIDTH]
                                  + (s_rkv[:, j * RWKV_WIDTH:(j + 1) * RWKV_WIDTH]
                                     - p_rkv[:, j * RWKV_WIDTH:(j + 1) * RWKV_WIDTH]) * mu)
            if store:
                prev_ref[:, 0:N_RKV] = p_rkv[STG - SUBLANES:]
                prev_ref[:, N_RKV:N_RKV + N_LORA] = p_lora[STG - SUBLANES:]
            pp["r"] = lerp(0, vec(_V_MU_R))
            pp["k"] = lerp(1, vec(_V_MU_K))
            pp["v"] = lerp(2, vec(_V_MU_V))
            put("v", pp["v"])
            low = p_lora[:, 0:LORA_WIDTH] + _shift_rows(p_lora, prev_lora)[:, LORA_WIDTH:2 * LORA_WIDTH]
            lane = lax.broadcasted_iota(jnp.int32, low.shape, 1)
            act = jnp.where(lane < DECAY_LORA, jnp.tanh(low),
                            jnp.where(lane < DECAY_LORA + AAA_LORA, low, jax.nn.sigmoid(low)))
            pp["second"] = _dot(act.astype(BF16), w2_ref[...])

        def prep_b():
            second = pp["second"]
            pp["ld"] = -math.exp(-0.5) * jax.nn.sigmoid(second[:, 0:RWKV_WIDTH] + vec(_V_W0))
            pp["a"] = jax.nn.sigmoid(second[:, RWKV_WIDTH:2 * RWKV_WIDTH] + vec(_V_A0))
            put("gate", second[:, 2 * RWKV_WIDTH:3 * RWKV_WIDTH])
            pp["kk"] = pp["k"] * vec(_V_KK)
            pp["kk_ss"] = head_sum(pp["kk"] * pp["kk"])
            pp["cum"] = _split_dot(tril_ref[...], pp["ld"])

        def prep_c():
            pp["kk"] = pp["kk"] * lax.rsqrt(jnp.maximum(pp["kk_ss"], 1e-24))
            pp["k2"] = pp["k"] * (1.0 + (pp["a"] - 1.0) * vec(_V_KA))
            put("bonus", head_sum(pp["r"] * pp["k2"] * vec(_V_RK)) * pp["v"])

        def prep_d():
            cum, ld, kk, k2 = pp["cum"], pp["ld"], pp["kk"], pp["k2"]
            b = kk * pp["a"]
            cum_end = jnp.concatenate(
                [jnp.broadcast_to(cum[(c + 1) * C - 1:(c + 1) * C, :], (C, RWKV_WIDTH))
                 for c in range(STG // C)], axis=0)
            e_neg = jnp.exp(-cum)
            e_end = jnp.exp(cum_end - cum)
            put("a_t", -kk * jnp.exp(cum - ld))
            put("r_t", pp["r"] * jnp.exp(cum))
            put("b_t", b * e_neg)
            put("k_t", k2 * e_neg)
            put("b_w", b * e_end)
            put("k_w", k2 * e_end)
            put("w_c", jnp.exp(cum_end))

        return pp, [prep_a, prep_b, prep_c, prep_d]

    def run_all(thunks):
        for thunk in thunks:
            thunk()

    zero_prev = lambda: (jnp.zeros((1, N_RKV), F32), jnp.zeros((1, N_LORA), F32))

    @pl.when((pl.program_id(0) == 0) & (pl.program_id(1) == 0))
    def _():
        _cast_rows_to_bf16([(win_hbm.at[pl.ds(r, MIX_WIN_ROWS), :], wstage_ref.at[i % 2],
                             win_ref.at[pl.ds(r, MIX_WIN_ROWS), :])
                            for i, r in enumerate(range(0, D_MODEL, MIX_WIN_ROWS))], wsem)
        for s in range(n_stages):
            front, ret = project(xfirst_ref.at[stage_rows[s], :], s)
            run_all(front + ret)
        run_all(prepare(0, zero_prev, store=True)[1])

    ret_state = [rstate_ref[h] for h in range(RET_HEADS)]

    def ret_unit(c, h):
        RC, RD = RET_CHUNK, RET_HEAD_DIM
        s, c_loc = divmod(c * RC, STG)
        rows = slice(c * RC, (c + 1) * RC)
        loc = slice(c_loc, c_loc + RC)
        row_i = lax.broadcasted_iota(jnp.int32, (RC, 1), 0).astype(F32)
        lg = _LOG_GAMMA[h]
        cos, sin = cos_ref[rows, :], sin_ref[rows, :]
        rot = lambda xh: xh * cos + pltpu.roll(xh, RD // 2, 1) * sin

        def part(j):
            c0 = j * RET_WIDTH + h * RD
            if s == 0:
                return pret0_ref[loc, c0:c0 + RD]
            return proj1_ref[loc, N_RKV + c0:N_RKV + c0 + RD]

        q = rot(part(0))
        kr = rot(part(1)) * (RD ** -0.5)
        vr = part(2)
        gt = part(3)
        qb, kb, vb = q.astype(BF16), kr.astype(BF16), vr.astype(BF16)
        scores = (_dot_nt(qb, kb) * dmask_ref[h]).astype(BF16)
        kv = _dot_tn(kb, (vr * jnp.exp((RC - 1.0 - row_i) * lg)).astype(BF16))
        inter = _dot((q * jnp.exp((row_i + 1.0) * lg)).astype(BF16), ret_state[h].astype(BF16))
        ret_state[h] = ret_state[h] * _GAMMA_C[h] + kv

        def second_half():
            y = _dot(scores, vb) + inter
            mu = jnp.mean(y, axis=-1, keepdims=True)
            d = y - mu
            var = jnp.mean(d * d, axis=-1, keepdims=True)
            yn = d * lax.rsqrt(var + RET_GN_EPS) * gnw_ref[:, h * RD:(h + 1) * RD]
            yret_ref[rows, h * RD:(h + 1) * RD] = (gt * jax.nn.sigmoid(gt) * yn).astype(BF16)
        return second_half

    ret_units = [(c, h) for c in range(TT // RET_CHUNK) for h in range(RET_HEADS)]
    ret_pending = []
    slot_queue = []

    def fill():
        if ret_pending:
            ret_pending.pop(0)()
        if ret_units:
            ret_pending.append(ret_unit(*ret_units.pop(0)))
        if slot_queue:
            for thunk in slot_queue.pop(0):
                thunk()

    c_bits = C.bit_length() - 1
    hd_bits = HD.bit_length() - 1
    assert C == 1 << c_bits and HD == 1 << hd_bits and W == GC
    sr = lax.broadcasted_iota(jnp.int32, (GC, W), 0)
    sl_ = lax.broadcasted_iota(jnp.int32, (GC, W), 1)
    stack_mask = (sr >> c_bits) == (sl_ >> hd_bits)
    ti = lax.broadcasted_iota(jnp.int32, (C, GC), 0)
    si = lax.broadcasted_iota(jnp.int32, (C, GC), 1) & (C - 1)
    strict = ti > si
    incl = ti >= si
    eye_c = (ti == si).astype(F32)
    wi = lax.broadcasted_iota(jnp.int32, (W, W), 0)
    wj = lax.broadcasted_iota(jnp.int32, (W, W), 1)
    eye_w = wi == wj

    def stack(x):
        return jnp.where(stack_mask, jnp.concatenate([x] * G, axis=0), 0.0)

    blk = lambda z, c, g: z[c * C:(c + 1) * C, g * W:(g + 1) * W]

    def chains_of(p):
        st = []
        for c in range(STG // C):
            for g in range(n_groups):
                a_c, r_c = blk(p["a_t"], c, g), blk(p["r_t"], c, g)
                st.append(dict(
                    ar=jnp.concatenate([a_c, r_c], axis=0).astype(BF16), r32=r_c,
                    a_s=stack(a_c).astype(BF16),
                    bk_s=jnp.concatenate([stack(blk(p["b_t"], c, g)), stack(blk(p["k_t"], c, g))],
                                         axis=0).astype(BF16),
                    bw_t=stack(blk(p["b_w"], c, g)).T.astype(BF16),
                    kw_t=stack(blk(p["k_w"], c, g)).T.astype(BF16),
                    v_s=stack(blk(p["v"], c, g)).astype(BF16), w_end=blk(p["w_c"], c, g)[0:1, :]))
        for s in st:
            prod = _dot_nt(s["ar"], s["bk_s"])
            s["n"] = jnp.where(strict, prod[0:C, 0:GC], 0.0)
            s["a_ak"] = jnp.where(strict, prod[0:C, GC:], 0.0).astype(BF16)
            s["a_rb"] = jnp.where(incl, prod[C:, 0:GC], 0.0).astype(BF16)
            s["a_rk"] = jnp.where(incl, prod[C:, GC:], 0.0).astype(BF16)
        fill()
        for s in st:
            res = _dot(jnp.concatenate([s["a_ak"], s["a_rk"], s["kw_t"]], axis=0), s["v_s"])
            s["av_s"] = stack(res[0:C]).astype(BF16)
            s["rkv"] = res[C:2 * C]
            s["kwv"] = res[2 * C:]
        fill()
        for s in st:
            nb = s["n"].astype(BF16)
            s["t"] = eye_c + s["n"]
            s["p"] = _dot(nb, stack(s["n"]).astype(BF16))
        fill()
        for i in range(1, 6):
            for s in st:
                p_s = stack(s["p"]).astype(BF16)
                if i < 5:
                    res = _dot(jnp.concatenate([s["t"], s["p"]], axis=0).astype(BF16), p_s)
                    s["t"] = s["t"] + res[0:C]
                    s["p"] = res[C:]
                else:
                    s["t"] = (s["t"] + _dot(s["t"].astype(BF16), p_s)).astype(BF16)
            fill()
        for s in st:
            s["x1_s"] = stack(_dot(s["t"], s["a_s"])).astype(BF16)
            s["x2_s"] = stack(_dot(s["t"], s["av_s"])).astype(BF16)
        for s in st:
            lhs = jnp.concatenate([s["a_rb"], s["bw_t"]], axis=0)
            o1 = _dot(lhs, s["x1_s"])
            o2 = _dot(lhs, s["x2_s"])
            s["q"] = (s["r32"] + o1[0:C]).astype(BF16)
            s["m"] = (o1[C:] + jnp.where(eye_w, s["w_end"], 0.0)).astype(BF16)
            s["y0"] = o2[0:C] + s["rkv"]
            s["g"] = o2[C:] + s["kwv"]
        return st

    def state_steps(st, h_cur, fillers):
        y_rows = []
        n_chunks = STG // C
        per_step = -(-len(fillers) // n_chunks)
        for c in range(n_chunks):
            y_lanes = []
            for g in range(n_groups):
                s = st[c * n_groups + g]
                res = _dot(jnp.concatenate([s["q"], s["m"]], axis=0), h_cur[g].astype(BF16))
                y_lanes.append(res[0:C] + s["y0"])
                h_cur[g] = res[C:] + s["g"]
            y_rows.append(jnp.concatenate(y_lanes, axis=1))
            run_all(fillers[c * per_step:(c + 1) * per_step])
        return jnp.concatenate(y_rows, axis=0)

    def finish(rows, y, p):
        inv_hd = 1.0 / HD
        mu = head_sum(y) * inv_hd
        d = y - mu
        var = head_sum(d * d) * inv_hd
        yn = d * lax.rsqrt(var + RWKV_GN_EPS) * vec(_V_LNW) + vec(_V_LNB)
        out_ref[rows, :] = ((yn + p["bonus"]) * p["gate"]).astype(BF16)

    assert n_stages == 2
    h_cur = [h_ref[g] for g in range(n_groups)]
    prep_s0 = {name: prep0_ref.at[i] for i, name in enumerate(_PREP_NAMES)}
    prep_s0["bonus"], prep_s0["gate"] = prep_s0["bonus"][...], prep_s0["gate"][...]
    last_s1 = (proj1_ref[STG - 1:STG, 0:N_RKV], proj1_ref[STG - 1:STG, N_RKV + N_RET:N_PROJ])
    last_step_of_row = pl.program_id(1) == pl.num_programs(1) - 1

    def chains_with_fillers(p, thunks):
        for t in thunks:
            slot_queue.extend([[t], []])
        st = chains_of(p)
        while slot_queue:
            fill()
        return st

    def interleave(a, b):
        return [t for pair in zip(a, b) for t in pair]

    front_n0, ret_n0 = project(xnext_ref.at[stage_rows[0], :], 0)
    chains_s0 = chains_with_fillers(prep_s0, front_n0)

    prev_s1 = lambda: (prev_ref[SUBLANES - 1:SUBLANES, 0:N_RKV],
                       prev_ref[SUBLANES - 1:SUBLANES, N_RKV:N_RKV + N_LORA])
    prep_s1, thunks = prepare(1, prev_s1, store=False)
    y_s0 = state_steps(chains_s0, h_cur, interleave(thunks, ret_n0))

    front_n1, ret_n1 = project(xnext_ref.at[stage_rows[1], :], 1)
    chains_s1 = chains_with_fillers(prep_s1, front_n1)

    finish(stage_rows[0], y_s0, prep_s0)
    prev_n0 = lambda: tuple(jnp.where(last_step_of_row, 0.0, z) for z in last_s1)
    y_s1 = state_steps(chains_s1, h_cur, interleave(prepare(0, prev_n0, store=True)[1], ret_n1))
    finish(stage_rows[1], y_s1, prep_s1)
    while ret_units or ret_pending:
        fill()
    for g in range(n_groups):
        h_ref[g] = h_cur[g]
    for h in range(RET_HEADS):
        rstate_ref[h] = ret_state[h]


def _mixer(x2, g_mix, w_in, w_low, vecs, w2cat, seg, tril, freq, gn_w, bsz, seq):
    tt = MIX_TILE
    nt = seq // tt
    w = RWKV_GROUP_HEADS * RWKV_HEAD_DIM
    n_groups = RWKV_WIDTH // w
    cos, sin, dmask = _ret_tables(freq, seq)
    row = lambda b, t: (b * nt + t, 0)
    next_tile = lambda b, t: (jnp.minimum(b * nt + t + 1, bsz * nt - 1), 0)
    pos = lambda b, t: (t, 0)
    const = lambda b, t: (0, 0)
    return pl.pallas_call(
        _mixer_kernel,
        grid=(bsz, nt),
        in_specs=[
            pl.BlockSpec((tt, D_MODEL), const),
            pl.BlockSpec((tt, D_MODEL), next_tile),
            pl.BlockSpec((1, D_MODEL), const),
            pl.BlockSpec(memory_space=pl.ANY),
            pl.BlockSpec((D_MODEL, N_LORA), const, pipeline_mode=pl.Buffered(1)),
            pl.BlockSpec((_N_VEC_ROWS, RWKV_WIDTH), const),
            pl.BlockSpec((LORA_WIDTH, 3 * RWKV_WIDTH), const),
            pl.BlockSpec((LANES, LANES), const),
            pl.BlockSpec((MIX_STAGE, MIX_STAGE), const),
            pl.BlockSpec((tt, RET_HEAD_DIM), pos),
            pl.BlockSpec((tt, RET_HEAD_DIM), pos),
            pl.BlockSpec((RET_HEADS, RET_CHUNK, RET_CHUNK), lambda b, t: (0, 0, 0)),
            pl.BlockSpec((1, RET_WIDTH), const),
        ],
        out_specs=[pl.BlockSpec((tt, RWKV_WIDTH), row), pl.BlockSpec((tt, RET_WIDTH), row)],
        out_shape=[jax.ShapeDtypeStruct((bsz * seq, RWKV_WIDTH), BF16),
                   jax.ShapeDtypeStruct((bsz * seq, RET_WIDTH), BF16)],
        scratch_shapes=[pltpu.VMEM((n_groups, w, w), F32),
                        pltpu.VMEM((RET_HEADS, RET_HEAD_DIM, RET_HEAD_DIM), F32),
                        pltpu.VMEM((SUBLANES, N_RKV + N_LORA), F32),
                        pltpu.VMEM((len(_PREP_NAMES), MIX_STAGE, RWKV_WIDTH), F32),
                        pltpu.VMEM((MIX_STAGE, N_RET), F32),
                        pltpu.VMEM((MIX_STAGE, N_PROJ), F32),
                        pltpu.VMEM((MIX_STAGE, N_RKV + N_LORA), F32),
                        pltpu.VMEM((D_MODEL, N_RKV + N_RET), BF16),
                        pltpu.VMEM((2, MIX_WIN_ROWS, N_RKV + N_RET), F32),
                        pltpu.SemaphoreType.DMA((2,))],
        compiler_params=pltpu.CompilerParams(
            dimension_semantics=("arbitrary", "arbitrary"), vmem_limit_bytes=_vmem_limit(56 * 1024 * 1024)),
        name="mixer",
    )(x2, x2, g_mix, w_in, w_low, vecs, w2cat, seg, tril, cos, sin, dmask, gn_w)


def _cast_rows_to_bf16(jobs, sem):
    copies = [pltpu.make_async_copy(src, stg, sem.at[i % 2]) for i, (src, stg, _) in enumerate(jobs)]
    copies[0].start()
    for i, (_, stg, dst) in enumerate(jobs):
        if i + 1 < len(jobs):
            copies[i + 1].start()
        copies[i].wait()
        dst[...] = stg[...].astype(BF16)


def _ffn_weight_jobs(wo_hbm, wg_hbm, wu_hbm, wd_hbm, wo_ref, wg_ref, wu_ref, wd_ref, wide_ref, tall_ref):
    jobs = []
    for hbm, vm in ((wg_hbm, wg_ref), (wu_hbm, wu_ref)):
        for r in range(0, D_MODEL, FFN_WIDE_ROWS):
            rows = pl.ds(r, FFN_WIDE_ROWS)
            jobs.append((hbm.at[rows, :], wide_ref.at[len(jobs) % 2], vm.at[rows, :]))
    for r in range(0, D_FF, FFN_TALL_ROWS):
        rows = pl.ds(r, FFN_TALL_ROWS)
        jobs.append((wd_hbm.at[rows, :], tall_ref.at[len(jobs) % 2], wd_ref.at[rows, :]))
    half = D_MODEL // 2
    for r in range(0, D_MODEL, half):
        rows = pl.ds(r, half)
        jobs.append((wo_hbm.at[rows, :], tall_ref.at[len(jobs) % 2, pl.ds(0, half), :], wo_ref.at[rows, :]))
    return jobs


def _ffn_kernel(yr_ref, yt_ref, x_ref, wo_hbm, gf_ref, wg_hbm, wu_hbm, cw_ref, cb_ref, wd_hbm, gl_ref,
                out_ref, carry_ref, wo_ref, wg_ref, wu_ref, wd_ref, wide_ref, tall_ref, sem):
    sm = FFN_TILE // FFN_STAGES
    stages = [slice(i * sm, (i + 1) * sm) for i in range(FFN_STAGES)]

    @pl.when((pl.program_id(0) == 0) & (pl.program_id(1) == 0))
    def _():
        _cast_rows_to_bf16(_ffn_weight_jobs(wo_hbm, wg_hbm, wu_hbm, wd_hbm, wo_ref, wg_ref, wu_ref, wd_ref,
                                            wide_ref, tall_ref), sem)

    @pl.when(pl.program_id(1) == 0)
    def _():
        carry_ref[...] = jnp.zeros_like(carry_ref)

    mix = [_dot(yr_ref[s, :], wo_ref[0:RWKV_WIDTH, :]) + _dot(yt_ref[s, :], wo_ref[RWKV_WIDTH:, :])
           for s in stages]
    acc = [x_ref[s, :] + m for s, m in zip(stages, mix)]
    hb = [_rms_norm(x1, gf_ref[...]).astype(BF16) for x1 in acc]

    cols = D_FF // FFN_COL_SPLIT
    for j in range(FFN_COL_SPLIT):
        cs = slice(j * cols, (j + 1) * cols)
        gate_up = [(_dot(h, wg_ref[:, cs]), _dot(h, wu_ref[:, cs])) for h in hb]
        prev = carry_ref[:, cs]
        p2, p1 = prev[SUBLANES - 2:SUBLANES - 1], prev[SUBLANES - 1:SUBLANES]
        hidden = []
        for gate, up in gate_up:
            row = lax.broadcasted_iota(jnp.int32, gate.shape, 0)
            g1 = jnp.where(row == 0, p1, pltpu.roll(gate, 1, 0))
            g2 = jnp.where(row == 0, p2, jnp.where(row == 1, p1, pltpu.roll(gate, 2, 0)))
            p2, p1 = gate[sm - 2:sm - 1], gate[sm - 1:sm]
            conv = cw_ref[0:1, cs] * g2 + cw_ref[1:2, cs] * g1 + cw_ref[2:3, cs] * gate + cb_ref[:, cs]
            hidden.append((conv * jax.nn.sigmoid(conv) * up).astype(BF16))
        carry_ref[:, cs] = gate_up[-1][0][sm - SUBLANES:sm]
        acc = [a + _dot(h, wd_ref[cs, :]) for a, h in zip(acc, hidden)]
    for s, a in zip(stages, acc):
        out_ref[s, :] = _rms_norm(a, gl_ref[...])


def _ffn(y_rwkv, y_ret, x2, w_out, g_ffn, w_gate, w_up, conv_w, conv_b, w_down, g_final, bsz, seq):
    tm = FFN_TILE
    nt = seq // tm
    row = lambda b, t: (b * nt + t, 0)
    const = lambda b, t: (0, 0)
    in_hbm = pl.BlockSpec(memory_space=pl.ANY)
    weights = (D_MODEL * D_MODEL + 3 * D_MODEL * D_FF) * 2
    staging = 2 * (FFN_WIDE_ROWS * D_FF + FFN_TALL_ROWS * D_MODEL) * 4
    vmem = weights + staging + 4 * tm * D_MODEL * 4 + 8 * tm * D_FF * 4 + 8 * 1024 * 1024
    return pl.pallas_call(
        _ffn_kernel,
        grid=(bsz, nt),
        in_specs=[
            pl.BlockSpec((tm, RWKV_WIDTH), row),
            pl.BlockSpec((tm, RET_WIDTH), row),
            pl.BlockSpec((tm, D_MODEL), row),
            in_hbm,
            pl.BlockSpec((1, D_MODEL), const),
            in_hbm,
            in_hbm,
            pl.BlockSpec((SUBLANES, D_FF), const),
            pl.BlockSpec((1, D_FF), const),
            in_hbm,
            pl.BlockSpec((1, D_MODEL), const),
        ],
        out_specs=pl.BlockSpec((tm, D_MODEL), row),
        out_shape=jax.ShapeDtypeStruct((bsz * seq, D_MODEL), F32),
        scratch_shapes=[pltpu.VMEM((SUBLANES, D_FF), F32),
                        pltpu.VMEM((D_MODEL, D_MODEL), BF16),
                        pltpu.VMEM((D_MODEL, D_FF), BF16),
                        pltpu.VMEM((D_MODEL, D_FF), BF16),
                        pltpu.VMEM((D_FF, D_MODEL), BF16),
                        pltpu.VMEM((2, FFN_WIDE_ROWS, D_FF), F32),
                        pltpu.VMEM((2, FFN_TALL_ROWS, D_MODEL), F32),
                        pltpu.SemaphoreType.DMA((2,))],
        compiler_params=pltpu.CompilerParams(
            dimension_semantics=("arbitrary", "arbitrary"), vmem_limit_bytes=_vmem_limit(vmem)),
        name="ffn",
    )(y_rwkv, y_ret, x2, w_out, g_ffn, w_gate, w_up, conv_w, conv_b, w_down, g_final)


def _block_ones(n, block):
    i = np.arange(n)
    return (i[:, None] // block) == (i[None, :] // block)


def kernel(x, norm_mix_g, w_in, rwkv_mu_r, rwkv_mu_k, rwkv_mu_v, rwkv_mu_w, rwkv_mu_a, rwkv_mu_g, rwkv_w0, rwkv_w1, rwkv_w2, rwkv_a0, rwkv_a1, rwkv_a2, rwkv_g1, rwkv_g2, rwkv_k_k, rwkv_k_a, rwkv_r_k, rwkv_lnx_w, rwkv_lnx_b, ret_gn_w, w_out, norm_ffn_g, ffn_w_gate, ffn_w_up, ffn_conv_w, ffn_conv_b, ffn_w_down, norm_final_g):
    bsz, seq, d = x.shape
    assert d == D_MODEL and seq % MIX_TILE == 0 and seq % FFN_TILE == 0
    assert norm_mix_g.shape[0] == 1, "one layer"
    x2 = x.reshape(bsz * seq, d)
    row = lambda p: p.reshape(1, -1)

    w_low = _fold_lora(rwkv_w1[0], rwkv_a1[0], rwkv_g1[0], rwkv_mu_w[0], rwkv_mu_a[0], rwkv_mu_g[0])

    vec_rows = [rwkv_mu_r[0], rwkv_mu_k[0], rwkv_mu_v[0], rwkv_w0[0], rwkv_a0[0], rwkv_k_k[0], rwkv_k_a[0],
                rwkv_r_k[0].reshape(-1), rwkv_lnx_w[0], rwkv_lnx_b[0]]
    vecs = jnp.zeros((_N_VEC_ROWS, RWKV_WIDTH), F32).at[:len(vec_rows)].set(jnp.stack(vec_rows))
    w2cat = jnp.zeros((LORA_WIDTH, 3 * RWKV_WIDTH), BF16)
    w2cat = w2cat.at[0:DECAY_LORA, 0:RWKV_WIDTH].set(rwkv_w2[0].astype(BF16))
    w2cat = w2cat.at[DECAY_LORA:DECAY_LORA + AAA_LORA, RWKV_WIDTH:2 * RWKV_WIDTH].set(rwkv_a2[0].astype(BF16))
    w2cat = w2cat.at[DECAY_LORA + AAA_LORA:, 2 * RWKV_WIDTH:].set(rwkv_g2[0].astype(BF16))
    seg = jnp.asarray(_block_ones(LANES, RWKV_HEAD_DIM), BF16)
    ti = np.arange(MIX_STAGE)
    tril = jnp.asarray(_block_ones(MIX_STAGE, RWKV_CHUNK) & (ti[:, None] >= ti[None, :]), BF16)
    half = RET_HEAD_DIM // 2
    inv_freq = ROPE_BASE ** (-jnp.arange(half, dtype=F32) / half)
    freq = jnp.concatenate([inv_freq, inv_freq]).reshape(1, RET_HEAD_DIM)
    y_rwkv, y_ret = _mixer(x2, row(norm_mix_g[0]), w_in[0], w_low, vecs, w2cat, seg, tril, freq,
                           row(ret_gn_w[0]),
                           bsz, seq)

    conv_w = jnp.zeros((SUBLANES, D_FF), F32).at[0:3].set(ffn_conv_w[0][:, 0, :])
    out = _ffn(y_rwkv, y_ret, x2, w_out[0], row(norm_ffn_g[0]), ffn_w_gate[0], ffn_w_up[0], conv_w,
               row(ffn_conv_b[0]), ffn_w_down[0], row(norm_final_g), bsz, seq)
    return out.reshape(bsz, seq, d)
```

```python
import math

import jax
import jax.numpy as jnp
import numpy as np
from jax import lax
from jax.experimental import pallas as pl
from jax.experimental.pallas import tpu as pltpu

F32 = jnp.float32
BF16 = jnp.bfloat16

D_MODEL = 1024
RWKV_HEADS = 8
RWKV_HEAD_DIM = 64
RWKV_WIDTH = 512
RET_HEADS = 4
RET_HEAD_DIM = 128
RET_WIDTH = 512
DECAY_LORA = 64
AAA_LORA = 64
GATE_LORA = 128
LORA_WIDTH = DECAY_LORA + AAA_LORA + GATE_LORA
RET_CHUNK = 128
ROPE_BASE = 10000.0
D_FF = 2816
NORM_EPS = 1e-6
RWKV_GN_EPS = 64e-5
RET_GN_EPS = 1e-5

V7X_VMEM_BYTES = 64 * 1024 * 1024
SUBLANES = 8
LANES = 128

RWKV_CHUNK = 64
RWKV_GROUP_HEADS = 2
MIX_TILE = 512
MIX_STAGE = 256
PROJ_PIECE = 512
FFN_TILE = 512
FFN_STAGES = 2
FFN_COL_SPLIT = 2
FFN_WIDE_ROWS = 256
FFN_TALL_ROWS = 704

N_RKV = 3 * RWKV_WIDTH
N_RET = 4 * RET_WIDTH
N_LORA = 2 * LORA_WIDTH
N_PROJ = N_RKV + N_RET + N_LORA

_LOG_GAMMA = [math.log(1.0 - 2.0 ** (-5.0 - h)) for h in range(RET_HEADS)]
_GAMMA_C = [math.exp(RET_CHUNK * lg) for lg in _LOG_GAMMA]


def _vmem_limit(nbytes):
    return int(min(nbytes, V7X_VMEM_BYTES - 4 * 1024 * 1024))


def _dot(a, b):
    return jnp.dot(a, b, preferred_element_type=F32)


def _dot_nt(a, b):
    return lax.dot_general(a, b, (((1,), (1,)), ((), ())), preferred_element_type=F32)


def _dot_tn(a, b):
    return lax.dot_general(a, b, (((0,), (0,)), ((), ())), preferred_element_type=F32)


def _split_dot(mat_bf16, x):
    hi = x.astype(BF16)
    lo = (x - hi.astype(F32)).astype(BF16)
    return _dot(mat_bf16, hi) + _dot(mat_bf16, lo)


def _shift_rows(x, prev_row):
    rolled = pltpu.roll(x, 1, 0)
    row = lax.broadcasted_iota(jnp.int32, x.shape, 0)
    return jnp.where(row == 0, prev_row, rolled)


def _rms_norm(x, g):
    ms = jnp.mean(x * x, axis=-1, keepdims=True)
    return x * lax.rsqrt(ms + NORM_EPS) * g


def _fold_lora_kernel(w1_ref, a1_ref, g1_ref, muw_ref, mua_ref, mug_ref, o_ref):
    c0 = 0
    for w_ref, mu_ref in ((w1_ref, muw_ref), (a1_ref, mua_ref), (g1_ref, mug_ref)):
        w, mu = w_ref[...], mu_ref[...]
        n = w.shape[1]
        o_ref[:, c0:c0 + n] = (w * (1.0 - mu)).astype(BF16)
        o_ref[:, LORA_WIDTH + c0:LORA_WIDTH + c0 + n] = (w * mu).astype(BF16)
        c0 += n


def _fold_lora(w1, a1, g1, mu_w, mu_a, mu_g):
    col = lambda v: v.reshape(-1, 1)
    return pl.pallas_call(
        _fold_lora_kernel,
        out_shape=jax.ShapeDtypeStruct((D_MODEL, 2 * LORA_WIDTH), BF16),
        name="fold_lora",
    )(w1, a1, g1, col(mu_w), col(mu_a), col(mu_g))


def _ret_tables_kernel(freq_ref, cos_ref, sin_ref, dmask_ref):
    C, HD = RET_CHUNK, RET_HEAD_DIM
    n = cos_ref.shape[0]
    ang = lax.broadcasted_iota(jnp.int32, (n, HD), 0).astype(F32) * freq_ref[...]
    lane = lax.broadcasted_iota(jnp.int32, (n, HD), 1)
    sin = jnp.sin(ang)
    cos_ref[...] = jnp.cos(ang)
    sin_ref[...] = jnp.where(lane < HD // 2, -sin, sin)
    ii = lax.broadcasted_iota(jnp.int32, (C, C), 0)
    jj = lax.broadcasted_iota(jnp.int32, (C, C), 1)
    diff = jnp.maximum((ii - jj).astype(F32), 0.0)
    for h in range(RET_HEADS):
        dmask_ref[h] = jnp.where(ii >= jj, jnp.exp(diff * _LOG_GAMMA[h]), 0.0)


def _ret_tables(freq, seq):
    return pl.pallas_call(
        _ret_tables_kernel,
        out_shape=[jax.ShapeDtypeStruct((seq, RET_HEAD_DIM), F32),
                   jax.ShapeDtypeStruct((seq, RET_HEAD_DIM), F32),
                   jax.ShapeDtypeStruct((RET_HEADS, RET_CHUNK, RET_CHUNK), F32)],
        name="ret_tables",
    )(freq)


(_V_MU_R, _V_MU_K, _V_MU_V, _V_W0, _V_A0, _V_KK, _V_KA, _V_RK, _V_LNW, _V_LNB) = range(10)
_N_VEC_ROWS = 16

_PREP_NAMES = ("a_t", "r_t", "b_t", "k_t", "b_w", "k_w", "w_c", "v", "bonus", "gate")

_PROJ_PIECES = ([("rkv", c) for c in range(0, N_RKV, PROJ_PIECE)]
                + [("ret", c) for c in range(N_RKV, N_RKV + N_RET, PROJ_PIECE)]
                + [("lora", c) for c in range(N_RKV + N_RET, N_PROJ, PROJ_PIECE)])


def _mixer_kernel(xfirst_ref, xnext_ref, gmix_ref, win_hbm, wlo_ref, vec_ref, w2_ref, seg_ref, tril_ref, cos_ref,
                  sin_ref, dmask_ref, gnw_ref, out_ref, yret_ref,
                  h_ref, rstate_ref, prev_ref, prep0_ref, pret0_ref, proj1_ref, prj0_ref,
                  win_ref, wstage_ref, wsem):
    C, G, TT, STG = RWKV_CHUNK, RWKV_GROUP_HEADS, MIX_TILE, MIX_STAGE
    HD = RWKV_HEAD_DIM
    W = G * HD
    GC = G * C
    n_groups = RWKV_WIDTH // W
    n_stages = TT // STG
    stage_rows = [slice(i * STG, (i + 1) * STG) for i in range(n_stages)]

    @pl.when(pl.program_id(1) == 0)
    def _():
        h_ref[...] = jnp.zeros_like(h_ref)
        rstate_ref[...] = jnp.zeros_like(rstate_ref)

    vec = lambda i: vec_ref[i:i + 1, :]

    seg = seg_ref[...]

    def head_sum(x):
        n_blk = x.shape[1] // LANES
        n_rows = x.shape[0]
        rows = jnp.concatenate([x[:, j * LANES:(j + 1) * LANES] for j in range(n_blk)], axis=0)
        s = _dot(rows.astype(BF16), seg)
        return jnp.concatenate([s[j * n_rows:(j + 1) * n_rows] for j in range(n_blk)], axis=1)

    def project(x_rows_ref, stage):
        got = {}

        def piece(i):
            def run():
                if "xb" not in got:
                    got["xb"] = _rms_norm(x_rows_ref[...], gmix_ref[...]).astype(BF16)
                name, c0 = _PROJ_PIECES[i]
                n_in = N_RKV + N_RET
                w = win_ref[:, c0:c0 + PROJ_PIECE] if c0 < n_in else wlo_ref[:, c0 - n_in:c0 - n_in + PROJ_PIECE]
                res = _dot(got["xb"], w)
                if stage == 1:
                    proj1_ref[:, c0:c0 + PROJ_PIECE] = res
                elif name == "ret":
                    pret0_ref[:, c0 - N_RKV:c0 - N_RKV + PROJ_PIECE] = res
                else:
                    d0 = c0 if name == "rkv" else c0 - N_RET
                    prj0_ref[:, d0:d0 + PROJ_PIECE] = res
            return run
        front = [piece(i) for i, (name, _) in enumerate(_PROJ_PIECES) if name != "ret"]
        ret = [piece(i) for i, (name, _) in enumerate(_PROJ_PIECES) if name == "ret"]
        return front, ret

    def prepare(stage, prev_rows, store):
        pp = {}

        def put(name, val):
            if store:
                prep0_ref[_PREP_NAMES.index(name)] = val
            else:
                pp[name] = val

        def prep_a():
            if stage == 1:
                p_rkv, p_lora = proj1_ref[:, 0:N_RKV], proj1_ref[:, N_RKV + N_RET:N_PROJ]
            else:
                p_rkv, p_lora = prj0_ref[:, 0:N_RKV], prj0_ref[:, N_RKV:N_RKV + N_LORA]
            prev_rkv, prev_lora = prev_rows()
            s_rkv = _shift_rows(p_rkv, prev_rkv)
            lerp = lambda j, mu: (p_rkv[:, j * RWKV_WIDTH:(j + 1) * RWKV_WIDTH]
                                  + (s_rkv[:, j * RWKV_WIDTH:(j + 1) * RWKV_WIDTH]
                                     - p_rkv[:, j * RWKV_WIDTH:(j + 1) * RWKV_WIDTH]) * mu)
            if store:
                prev_ref[:, 0:N_RKV] = p_rkv[STG - SUBLANES:]
                prev_ref[:, N_RKV:N_RKV + N_LORA] = p_lora[STG - SUBLANES:]
            pp["r"] = lerp(0, vec(_V_MU_R))
            pp["k"] = lerp(1, vec(_V_MU_K))
            pp["v"] = lerp(2, vec(_V_MU_V))
            put("v", pp["v"])
            low = p_lora[:, 0:LORA_WIDTH] + _shift_rows(p_lora, prev_lora)[:, LORA_WIDTH:2 * LORA_WIDTH]
            lane = lax.broadcasted_iota(jnp.int32, low.shape, 1)
            act = jnp.where(lane < DECAY_LORA, jnp.tanh(low),
                            jnp.where(lane < DECAY_LORA + AAA_LORA, low, jax.nn.sigmoid(low)))
            pp["second"] = _dot(act.astype(BF16), w2_ref[...])

        def prep_b():
            second = pp["second"]
            pp["ld"] = -math.exp(-0.5) * jax.nn.sigmoid(second[:, 0:RWKV_WIDTH] + vec(_V_W0))
            pp["a"] = jax.nn.sigmoid(second[:, RWKV_WIDTH:2 * RWKV_WIDTH] + vec(_V_A0))
            put("gate", second[:, 2 * RWKV_WIDTH:3 * RWKV_WIDTH])
            pp["kk"] = pp["k"] * vec(_V_KK)
            pp["kk_ss"] = head_sum(pp["kk"] * pp["kk"])
            pp["cum"] = _split_dot(tril_ref[...], pp["ld"])

        def prep_c():
            pp["kk"] = pp["kk"] * lax.rsqrt(jnp.maximum(pp["kk_ss"], 1e-24))
            pp["k2"] = pp["k"] * (1.0 + (pp["a"] - 1.0) * vec(_V_KA))
            put("bonus", head_sum(pp["r"] * pp["k2"] * vec(_V_RK)) * pp["v"])

        def prep_d():
            cum, ld, kk, k2 = pp["cum"], pp["ld"], pp["kk"], pp["k2"]
            b = kk * pp["a"]
            cum_end = jnp.concatenate(
                [jnp.broadcast_to(cum[(c + 1) * C - 1:(c + 1) * C, :], (C, RWKV_WIDTH))
                 for c in range(STG // C)], axis=0)
            e_neg = jnp.exp(-cum)
            e_end = jnp.exp(cum_end - cum)
            put("a_t", -kk * jnp.exp(cum - ld))
            put("r_t", pp["r"] * jnp.exp(cum))
            put("b_t", b * e_neg)
            put("k_t", k2 * e_neg)
            put("b_w", b * e_end)
            put("k_w", k2 * e_end)
            put("w_c", jnp.exp(cum_end))

        return pp, [prep_a, prep_b, prep_c, prep_d]

    def run_all(thunks):
        for thunk in thunks:
            thunk()

    zero_prev = lambda: (jnp.zeros((1, N_RKV), F32), jnp.zeros((1, N_LORA), F32))

    @pl.when((pl.program_id(0) == 0) & (pl.program_id(1) == 0))
    def _():
        pieces = []
        for s in range(n_stages):
            front, ret = project(xfirst_ref.at[stage_rows[s], :], s)
            n_rkv = N_RKV // PROJ_PIECE
            pieces.append(front[:n_rkv] + ret + front[n_rkv:])
        copies = [pltpu.make_async_copy(win_hbm.at[:, pl.ds(c0, PROJ_PIECE)], wstage_ref.at[i % 2], wsem.at[i % 2])
                  for i, c0 in enumerate(range(0, N_RKV + N_RET, PROJ_PIECE))]
        copies[0].start()
        for i in range(len(_PROJ_PIECES)):
            if i < len(copies):
                if i + 1 < len(copies):
                    copies[i + 1].start()
                copies[i].wait()
                win_ref[:, i * PROJ_PIECE:(i + 1) * PROJ_PIECE] = wstage_ref[i % 2].astype(BF16)
            for s in range(n_stages):
                pieces[s][i]()
        run_all(prepare(0, zero_prev, store=True)[1])

    ret_state = [rstate_ref[h] for h in range(RET_HEADS)]

    def ret_unit(c, h):
        RC, RD = RET_CHUNK, RET_HEAD_DIM
        s, c_loc = divmod(c * RC, STG)
        rows = slice(c * RC, (c + 1) * RC)
        loc = slice(c_loc, c_loc + RC)
        row_i = lax.broadcasted_iota(jnp.int32, (RC, 1), 0).astype(F32)
        lg = _LOG_GAMMA[h]
        cos, sin = cos_ref[rows, :], sin_ref[rows, :]
        rot = lambda xh: xh * cos + pltpu.roll(xh, RD // 2, 1) * sin

        def part(j):
            c0 = j * RET_WIDTH + h * RD
            if s == 0:
                return pret0_ref[loc, c0:c0 + RD]
            return proj1_ref[loc, N_RKV + c0:N_RKV + c0 + RD]

        q = rot(part(0))
        kr = rot(part(1)) * (RD ** -0.5)
        vr = part(2)
        gt = part(3)
        qb, kb, vb = q.astype(BF16), kr.astype(BF16), vr.astype(BF16)
        scores = (_dot_nt(qb, kb) * dmask_ref[h]).astype(BF16)
        kv = _dot_tn(kb, (vr * jnp.exp((RC - 1.0 - row_i) * lg)).astype(BF16))
        inter = _dot((q * jnp.exp((row_i + 1.0) * lg)).astype(BF16), ret_state[h].astype(BF16))
        ret_state[h] = ret_state[h] * _GAMMA_C[h] + kv

        def second_half():
            y = _dot(scores, vb) + inter
            mu = jnp.mean(y, axis=-1, keepdims=True)
            d = y - mu
            var = jnp.mean(d * d, axis=-1, keepdims=True)
            yn = d * lax.rsqrt(var + RET_GN_EPS) * gnw_ref[:, h * RD:(h + 1) * RD]
            yret_ref[rows, h * RD:(h + 1) * RD] = (gt * jax.nn.sigmoid(gt) * yn).astype(BF16)
        return second_half

    ret_units = [(c, h) for c in range(TT // RET_CHUNK) for h in range(RET_HEADS)]
    ret_pending = []
    slot_queue = []

    def fill():
        if ret_pending:
            ret_pending.pop(0)()
        if ret_units:
            ret_pending.append(ret_unit(*ret_units.pop(0)))
        if slot_queue:
            for thunk in slot_queue.pop(0):
                thunk()

    c_bits = C.bit_length() - 1
    hd_bits = HD.bit_length() - 1
    assert C == 1 << c_bits and HD == 1 << hd_bits and W == GC
    sr = lax.broadcasted_iota(jnp.int32, (GC, W), 0)
    sl_ = lax.broadcasted_iota(jnp.int32, (GC, W), 1)
    stack_mask = (sr >> c_bits) == (sl_ >> hd_bits)
    ti = lax.broadcasted_iota(jnp.int32, (C, GC), 0)
    si = lax.broadcasted_iota(jnp.int32, (C, GC), 1) & (C - 1)
    strict = ti > si
    incl = ti >= si
    eye_c = (ti == si).astype(F32)
    wi = lax.broadcasted_iota(jnp.int32, (W, W), 0)
    wj = lax.broadcasted_iota(jnp.int32, (W, W), 1)
    eye_w = wi == wj

    def stack(x):
        return jnp.where(stack_mask, jnp.concatenate([x] * G, axis=0), 0.0)

    blk = lambda z, c, g: z[c * C:(c + 1) * C, g * W:(g + 1) * W]

    def chains_of(p):
        st = []
        for c in range(STG // C):
            for g in range(n_groups):
                a_c, r_c = blk(p["a_t"], c, g), blk(p["r_t"], c, g)
                st.append(dict(
                    ar=jnp.concatenate([a_c, r_c], axis=0).astype(BF16), r32=r_c,
                    a_s=stack(a_c).astype(BF16),
                    bk_s=jnp.concatenate([stack(blk(p["b_t"], c, g)), stack(blk(p["k_t"], c, g))],
                                         axis=0).astype(BF16),
                    bw_t=stack(blk(p["b_w"], c, g)).T.astype(BF16),
                    kw_t=stack(blk(p["k_w"], c, g)).T.astype(BF16),
                    v_s=stack(blk(p["v"], c, g)).astype(BF16), w_end=blk(p["w_c"], c, g)[0:1, :]))
        for s in st:
            prod = _dot_nt(s["ar"], s["bk_s"])
            s["n"] = jnp.where(strict, prod[0:C, 0:GC], 0.0)
            s["a_ak"] = jnp.where(strict, prod[0:C, GC:], 0.0).astype(BF16)
            s["a_rb"] = jnp.where(incl, prod[C:, 0:GC], 0.0).astype(BF16)
            s["a_rk"] = jnp.where(incl, prod[C:, GC:], 0.0).astype(BF16)
        fill()
        for s in st:
            res = _dot(jnp.concatenate([s["a_ak"], s["a_rk"], s["kw_t"]], axis=0), s["v_s"])
            s["av_s"] = stack(res[0:C]).astype(BF16)
            s["rkv"] = res[C:2 * C]
            s["kwv"] = res[2 * C:]
        fill()
        for s in st:
            nb = s["n"].astype(BF16)
            s["t"] = eye_c + s["n"]
            s["p"] = _dot(nb, stack(s["n"]).astype(BF16))
        fill()
        for i in range(1, 6):
            for s in st:
                p_s = stack(s["p"]).astype(BF16)
                if i < 5:
                    res = _dot(jnp.concatenate([s["t"], s["p"]], axis=0).astype(BF16), p_s)
                    s["t"] = s["t"] + res[0:C]
                    s["p"] = res[C:]
                else:
                    s["t"] = (s["t"] + _dot(s["t"].astype(BF16), p_s)).astype(BF16)
            fill()
        for s in st:
            s["x1_s"] = stack(_dot(s["t"], s["a_s"])).astype(BF16)
            s["x2_s"] = stack(_dot(s["t"], s["av_s"])).astype(BF16)
        for s in st:
            lhs = jnp.concatenate([s["a_rb"], s["bw_t"]], axis=0)
            o1 = _dot(lhs, s["x1_s"])
            o2 = _dot(lhs, s["x2_s"])
            s["q"] = (s["r32"] + o1[0:C]).astype(BF16)
            s["m"] = (o1[C:] + jnp.where(eye_w, s["w_end"], 0.0)).astype(BF16)
            s["y0"] = o2[0:C] + s["rkv"]
            s["g"] = o2[C:] + s["kwv"]
        return st

    def state_steps(st, h_cur, fillers):
        y_rows = []
        n_chunks = STG // C
        per_step = -(-len(fillers) // n_chunks)
        for c in range(n_chunks):
            y_lanes = []
            for g in range(n_groups):
                s = st[c * n_groups + g]
                res = _dot(jnp.concatenate([s["q"], s["m"]], axis=0), h_cur[g].astype(BF16))
                y_lanes.append(res[0:C] + s["y0"])
                h_cur[g] = res[C:] + s["g"]
            y_rows.append(jnp.concatenate(y_lanes, axis=1))
            run_all(fillers[c * per_step:(c + 1) * per_step])
        return jnp.concatenate(y_rows, axis=0)

    def finish(rows, y, p):
        inv_hd = 1.0 / HD
        mu = head_sum(y) * inv_hd
        d = y - mu
        var = head_sum(d * d) * inv_hd
        yn = d * lax.rsqrt(var + RWKV_GN_EPS) * vec(_V_LNW) + vec(_V_LNB)
        out_ref[rows, :] = ((yn + p["bonus"]) * p["gate"]).astype(BF16)

    assert n_stages == 2
    h_cur = [h_ref[g] for g in range(n_groups)]
    prep_s0 = {name: prep0_ref.at[i] for i, name in enumerate(_PREP_NAMES)}
    prep_s0["bonus"], prep_s0["gate"] = prep_s0["bonus"][...], prep_s0["gate"][...]
    last_s1 = (proj1_ref[STG - 1:STG, 0:N_RKV], proj1_ref[STG - 1:STG, N_RKV + N_RET:N_PROJ])
    last_step_of_row = pl.program_id(1) == pl.num_programs(1) - 1

    def chains_with_fillers(p, thunks):
        for t in thunks:
            slot_queue.extend([[t], []])
        st = chains_of(p)
        while slot_queue:
            fill()
        return st

    def interleave(a, b):
        return [t for pair in zip(a, b) for t in pair]

    front_n0, ret_n0 = project(xnext_ref.at[stage_rows[0], :], 0)
    chains_s0 = chains_with_fillers(prep_s0, front_n0)

    prev_s1 = lambda: (prev_ref[SUBLANES - 1:SUBLANES, 0:N_RKV],
                       prev_ref[SUBLANES - 1:SUBLANES, N_RKV:N_RKV + N_LORA])
    prep_s1, thunks = prepare(1, prev_s1, store=False)
    y_s0 = state_steps(chains_s0, h_cur, interleave(thunks, ret_n0))

    front_n1, ret_n1 = project(xnext_ref.at[stage_rows[1], :], 1)
    chains_s1 = chains_with_fillers(prep_s1, front_n1)

    finish(stage_rows[0], y_s0, prep_s0)
    prev_n0 = lambda: tuple(jnp.where(last_step_of_row, 0.0, z) for z in last_s1)
    y_s1 = state_steps(chains_s1, h_cur, interleave(prepare(0, prev_n0, store=True)[1], ret_n1))
    finish(stage_rows[1], y_s1, prep_s1)
    while ret_units or ret_pending:
        fill()
    for g in range(n_groups):
        h_ref[g] = h_cur[g]
    for h in range(RET_HEADS):
        rstate_ref[h] = ret_state[h]


def _mixer(x2, g_mix, w_in, w_low, vecs, w2cat, seg, tril, freq, gn_w, bsz, seq):
    tt = MIX_TILE
    nt = seq // tt
    w = RWKV_GROUP_HEADS * RWKV_HEAD_DIM
    n_groups = RWKV_WIDTH // w
    cos, sin, dmask = _ret_tables(freq, seq)
    row = lambda b, t: (b * nt + t, 0)
    next_tile = lambda b, t: (jnp.minimum(b * nt + t + 1, bsz * nt - 1), 0)
    pos = lambda b, t: (t, 0)
    const = lambda b, t: (0, 0)
    return pl.pallas_call(
        _mixer_kernel,
        grid=(bsz, nt),
        in_specs=[
            pl.BlockSpec((tt, D_MODEL), const),
            pl.BlockSpec((tt, D_MODEL), next_tile),
            pl.BlockSpec((1, D_MODEL), const),
            pl.BlockSpec(memory_space=pl.ANY),
            pl.BlockSpec((D_MODEL, N_LORA), const, pipeline_mode=pl.Buffered(1)),
            pl.BlockSpec((_N_VEC_ROWS, RWKV_WIDTH), const),
            pl.BlockSpec((LORA_WIDTH, 3 * RWKV_WIDTH), const),
            pl.BlockSpec((LANES, LANES), const),
            pl.BlockSpec((MIX_STAGE, MIX_STAGE), const),
            pl.BlockSpec((tt, RET_HEAD_DIM), pos),
            pl.BlockSpec((tt, RET_HEAD_DIM), pos),
            pl.BlockSpec((RET_HEADS, RET_CHUNK, RET_CHUNK), lambda b, t: (0, 0, 0)),
            pl.BlockSpec((1, RET_WIDTH), const),
        ],
        out_specs=[pl.BlockSpec((tt, RWKV_WIDTH), row), pl.BlockSpec((tt, RET_WIDTH), row)],
        out_shape=[jax.ShapeDtypeStruct((bsz * seq, RWKV_WIDTH), BF16),
                   jax.ShapeDtypeStruct((bsz * seq, RET_WIDTH), BF16)],
        scratch_shapes=[pltpu.VMEM((n_groups, w, w), F32),
                        pltpu.VMEM((RET_HEADS, RET_HEAD_DIM, RET_HEAD_DIM), F32),
                        pltpu.VMEM((SUBLANES, N_RKV + N_LORA), F32),
                        pltpu.VMEM((len(_PREP_NAMES), MIX_STAGE, RWKV_WIDTH), F32),
                        pltpu.VMEM((MIX_STAGE, N_RET), F32),
                        pltpu.VMEM((MIX_STAGE, N_PROJ), F32),
                        pltpu.VMEM((MIX_STAGE, N_RKV + N_LORA), F32),
                        pltpu.VMEM((D_MODEL, N_RKV + N_RET), BF16),
                        pltpu.VMEM((2, D_MODEL, PROJ_PIECE), F32),
                        pltpu.SemaphoreType.DMA((2,))],
        compiler_params=pltpu.CompilerParams(
            dimension_semantics=("arbitrary", "arbitrary"), vmem_limit_bytes=_vmem_limit(56 * 1024 * 1024)),
        name="mixer",
    )(x2, x2, g_mix, w_in, w_low, vecs, w2cat, seg, tril, cos, sin, dmask, gn_w)


def _cast_rows_to_bf16(jobs, sem):
    copies = [pltpu.make_async_copy(src, stg, sem.at[i % 2]) for i, (src, stg, _) in enumerate(jobs)]
    copies[0].start()
    for i, (_, stg, dst) in enumerate(jobs):
        if i + 1 < len(jobs):
            copies[i + 1].start()
        copies[i].wait()
        dst[...] = stg[...].astype(BF16)


def _ffn_weight_jobs(wo_hbm, wg_hbm, wu_hbm, wd_hbm, wo_ref, wg_ref, wu_ref, wd_ref, wide_ref, tall_ref):
    jobs = []
    for hbm, vm in ((wg_hbm, wg_ref), (wu_hbm, wu_ref)):
        for r in range(0, D_MODEL, FFN_WIDE_ROWS):
            rows = pl.ds(r, FFN_WIDE_ROWS)
            jobs.append((hbm.at[rows, :], wide_ref.at[len(jobs) % 2], vm.at[rows, :]))
    for r in range(0, D_FF, FFN_TALL_ROWS):
        rows = pl.ds(r, FFN_TALL_ROWS)
        jobs.append((wd_hbm.at[rows, :], tall_ref.at[len(jobs) % 2], wd_ref.at[rows, :]))
    half = D_MODEL // 2
    for r in range(0, D_MODEL, half):
        rows = pl.ds(r, half)
        jobs.append((wo_hbm.at[rows, :], tall_ref.at[len(jobs) % 2, pl.ds(0, half), :], wo_ref.at[rows, :]))
    return jobs


def _ffn_kernel(yr_ref, yt_ref, x_ref, wo_hbm, gf_ref, wg_hbm, wu_hbm, cw_ref, cb_ref, wd_hbm, gl_ref,
                out_ref, carry_ref, wo_ref, wg_ref, wu_ref, wd_ref, wide_ref, tall_ref, sem):
    sm = FFN_TILE // FFN_STAGES
    stages = [slice(i * sm, (i + 1) * sm) for i in range(FFN_STAGES)]

    @pl.when((pl.program_id(0) == 0) & (pl.program_id(1) == 0))
    def _():
        _cast_rows_to_bf16(_ffn_weight_jobs(wo_hbm, wg_hbm, wu_hbm, wd_hbm, wo_ref, wg_ref, wu_ref, wd_ref,
                                            wide_ref, tall_ref), sem)

    @pl.when(pl.program_id(1) == 0)
    def _():
        carry_ref[...] = jnp.zeros_like(carry_ref)

    mix = [_dot(yr_ref[s, :], wo_ref[0:RWKV_WIDTH, :]) + _dot(yt_ref[s, :], wo_ref[RWKV_WIDTH:, :])
           for s in stages]
    acc = [x_ref[s, :] + m for s, m in zip(stages, mix)]
    hb = [_rms_norm(x1, gf_ref[...]).astype(BF16) for x1 in acc]

    cols = D_FF // FFN_COL_SPLIT
    for j in range(FFN_COL_SPLIT):
        cs = slice(j * cols, (j + 1) * cols)
        gate_up = [(_dot(h, wg_ref[:, cs]), _dot(h, wu_ref[:, cs])) for h in hb]
        prev = carry_ref[:, cs]
        p2, p1 = prev[SUBLANES - 2:SUBLANES - 1], prev[SUBLANES - 1:SUBLANES]
        hidden = []
        for gate, up in gate_up:
            row = lax.broadcasted_iota(jnp.int32, gate.shape, 0)
            g1 = jnp.where(row == 0, p1, pltpu.roll(gate, 1, 0))
            g2 = jnp.where(row == 0, p2, jnp.where(row == 1, p1, pltpu.roll(gate, 2, 0)))
            p2, p1 = gate[sm - 2:sm - 1], gate[sm - 1:sm]
            conv = cw_ref[0:1, cs] * g2 + cw_ref[1:2, cs] * g1 + cw_ref[2:3, cs] * gate + cb_ref[:, cs]
            hidden.append((conv * jax.nn.sigmoid(conv) * up).astype(BF16))
        carry_ref[:, cs] = gate_up[-1][0][sm - SUBLANES:sm]
        acc = [a + _dot(h, wd_ref[cs, :]) for a, h in zip(acc, hidden)]
    for s, a in zip(stages, acc):
        out_ref[s, :] = _rms_norm(a, gl_ref[...])


def _ffn(y_rwkv, y_ret, x2, w_out, g_ffn, w_gate, w_up, conv_w, conv_b, w_down, g_final, bsz, seq):
    tm = FFN_TILE
    nt = seq // tm
    row = lambda b, t: (b * nt + t, 0)
    const = lambda b, t: (0, 0)
    in_hbm = pl.BlockSpec(memory_space=pl.ANY)
    weights = (D_MODEL * D_MODEL + 3 * D_MODEL * D_FF) * 2
    staging = 2 * (FFN_WIDE_ROWS * D_FF + FFN_TALL_ROWS * D_MODEL) * 4
    vmem = weights + staging + 4 * tm * D_MODEL * 4 + 8 * tm * D_FF * 4 + 8 * 1024 * 1024
    return pl.pallas_call(
        _ffn_kernel,
        grid=(bsz, nt),
        in_specs=[
            pl.BlockSpec((tm, RWKV_WIDTH), row),
            pl.BlockSpec((tm, RET_WIDTH), row),
            pl.BlockSpec((tm, D_MODEL), row),
            in_hbm,
            pl.BlockSpec((1, D_MODEL), const),
            in_hbm,
            in_hbm,
            pl.BlockSpec((SUBLANES, D_FF), const),
            pl.BlockSpec((1, D_FF), const),
            in_hbm,
            pl.BlockSpec((1, D_MODEL), const),
        ],
        out_specs=pl.BlockSpec((tm, D_MODEL), row),
        out_shape=jax.ShapeDtypeStruct((bsz * seq, D_MODEL), F32),
        scratch_shapes=[pltpu.VMEM((SUBLANES, D_FF), F32),
                        pltpu.VMEM((D_MODEL, D_MODEL), BF16),
                        pltpu.VMEM((D_MODEL, D_FF), BF16),
                        pltpu.VMEM((D_MODEL, D_FF), BF16),
                        pltpu.VMEM((D_FF, D_MODEL), BF16),
                        pltpu.VMEM((2, FFN_WIDE_ROWS, D_FF), F32),
                        pltpu.VMEM((2, FFN_TALL_ROWS, D_MODEL), F32),
                        pltpu.SemaphoreType.DMA((2,))],
        compiler_params=pltpu.CompilerParams(
            dimension_semantics=("arbitrary", "arbitrary"), vmem_limit_bytes=_vmem_limit(vmem)),
        name="ffn",
    )(y_rwkv, y_ret, x2, w_out, g_ffn, w_gate, w_up, conv_w, conv_b, w_down, g_final)


def _block_ones(n, block):
    i = np.arange(n)
    return (i[:, None] // block) == (i[None, :] // block)


def kernel(x, norm_mix_g, w_in, rwkv_mu_r, rwkv_mu_k, rwkv_mu_v, rwkv_mu_w, rwkv_mu_a, rwkv_mu_g, rwkv_w0, rwkv_w1, rwkv_w2, rwkv_a0, rwkv_a1, rwkv_a2, rwkv_g1, rwkv_g2, rwkv_k_k, rwkv_k_a, rwkv_r_k, rwkv_lnx_w, rwkv_lnx_b, ret_gn_w, w_out, norm_ffn_g, ffn_w_gate, ffn_w_up, ffn_conv_w, ffn_conv_b, ffn_w_down, norm_final_g):
    bsz, seq, d = x.shape
    assert d == D_MODEL and seq % MIX_TILE == 0 and seq % FFN_TILE == 0
    assert norm_mix_g.shape[0] == 1, "one layer"
    x2 = x.reshape(bsz * seq, d)
    row = lambda p: p.reshape(1, -1)

    w_low = _fold_lora(rwkv_w1[0], rwkv_a1[0], rwkv_g1[0], rwkv_mu_w[0], rwkv_mu_a[0], rwkv_mu_g[0])

    vec_rows = [rwkv_mu_r[0], rwkv_mu_k[0], rwkv_mu_v[0], rwkv_w0[0], rwkv_a0[0], rwkv_k_k[0], rwkv_k_a[0],
                rwkv_r_k[0].reshape(-1), rwkv_lnx_w[0], rwkv_lnx_b[0]]
    vecs = jnp.zeros((_N_VEC_ROWS, RWKV_WIDTH), F32).at[:len(vec_rows)].set(jnp.stack(vec_rows))
    w2cat = jnp.zeros((LORA_WIDTH, 3 * RWKV_WIDTH), BF16)
    w2cat = w2cat.at[0:DECAY_LORA, 0:RWKV_WIDTH].set(rwkv_w2[0].astype(BF16))
    w2cat = w2cat.at[DECAY_LORA:DECAY_LORA + AAA_LORA, RWKV_WIDTH:2 * RWKV_WIDTH].set(rwkv_a2[0].astype(BF16))
    w2cat = w2cat.at[DECAY_LORA + AAA_LORA:, 2 * RWKV_WIDTH:].set(rwkv_g2[0].astype(BF16))
    seg = jnp.asarray(_block_ones(LANES, RWKV_HEAD_DIM), BF16)
    ti = np.arange(MIX_STAGE)
    tril = jnp.asarray(_block_ones(MIX_STAGE, RWKV_CHUNK) & (ti[:, None] >= ti[None, :]), BF16)
    half = RET_HEAD_DIM // 2
    inv_freq = ROPE_BASE ** (-jnp.arange(half, dtype=F32) / half)
    freq = jnp.concatenate([inv_freq, inv_freq]).reshape(1, RET_HEAD_DIM)
    y_rwkv, y_ret = _mixer(x2, row(norm_mix_g[0]), w_in[0], w_low, vecs, w2cat, seg, tril, freq,
                           row(ret_gn_w[0]),
                           bsz, seq)

    conv_w = jnp.zeros((SUBLANES, D_FF), F32).at[0:3].set(ffn_conv_w[0][:, 0, :])
    out = _ffn(y_rwkv, y_ret, x2, w_out[0], row(norm_ffn_g[0]), ffn_w_gate[0], ffn_w_up[0], conv_w,
               row(ffn_conv_b[0]), ffn_w_down[0], row(norm_final_g), bsz, seq)
    return out.reshape(bsz, seq, d)
```

```python
import math

import jax
import jax.numpy as jnp
import numpy as np
from jax import lax
from jax.experimental import pallas as pl
from jax.experimental.pallas import tpu as pltpu

F32 = jnp.float32
BF16 = jnp.bfloat16

D_MODEL = 1024
RWKV_HEADS = 8
RWKV_HEAD_DIM = 64
RWKV_WIDTH = 512
RET_HEADS = 4
RET_HEAD_DIM = 128
RET_WIDTH = 512
DECAY_LORA = 64
AAA_LORA = 64
GATE_LORA = 128
LORA_WIDTH = DECAY_LORA + AAA_LORA + GATE_LORA
RET_CHUNK = 128
ROPE_BASE = 10000.0
D_FF = 2816
NORM_EPS = 1e-6
RWKV_GN_EPS = 64e-5
RET_GN_EPS = 1e-5

V7X_VMEM_BYTES = 64 * 1024 * 1024
SUBLANES = 8
LANES = 128

RWKV_CHUNK = 64
INV_BASE = 8
RWKV_GROUP_HEADS = 2
MIX_TILE = 512
MIX_STAGE = 256
PROJ_PIECE = 512
FFN_TILE = 512
FFN_STAGES = 2
FFN_COL_SPLIT = 2
FFN_WIDE_ROWS = 256
FFN_TALL_ROWS = 704

N_RKV = 3 * RWKV_WIDTH
N_RET = 4 * RET_WIDTH
N_LORA = 2 * LORA_WIDTH
N_PROJ = N_RKV + N_RET + N_LORA

_LOG_GAMMA = [math.log(1.0 - 2.0 ** (-5.0 - h)) for h in range(RET_HEADS)]
_GAMMA_C = [math.exp(RET_CHUNK * lg) for lg in _LOG_GAMMA]


def _vmem_limit(nbytes):
    return int(min(nbytes, V7X_VMEM_BYTES - 4 * 1024 * 1024))


def _dot(a, b):
    return jnp.dot(a, b, preferred_element_type=F32)


def _dot_nt(a, b):
    return lax.dot_general(a, b, (((1,), (1,)), ((), ())), preferred_element_type=F32)


def _dot_tn(a, b):
    return lax.dot_general(a, b, (((0,), (0,)), ((), ())), preferred_element_type=F32)


def _split_dot(mat_bf16, x):
    hi = x.astype(BF16)
    lo = (x - hi.astype(F32)).astype(BF16)
    return _dot(mat_bf16, hi) + _dot(mat_bf16, lo)


def _shift_rows(x, prev_row):
    rolled = pltpu.roll(x, 1, 0)
    row = lax.broadcasted_iota(jnp.int32, x.shape, 0)
    return jnp.where(row == 0, prev_row, rolled)


def _rms_norm(x, g):
    ms = jnp.mean(x * x, axis=-1, keepdims=True)
    return x * lax.rsqrt(ms + NORM_EPS) * g


def _fold_lora_kernel(w1_ref, a1_ref, g1_ref, muw_ref, mua_ref, mug_ref, o_ref):
    c0 = 0
    for w_ref, mu_ref in ((w1_ref, muw_ref), (a1_ref, mua_ref), (g1_ref, mug_ref)):
        w, mu = w_ref[...], mu_ref[...]
        n = w.shape[1]
        o_ref[:, c0:c0 + n] = (w * (1.0 - mu)).astype(BF16)
        o_ref[:, LORA_WIDTH + c0:LORA_WIDTH + c0 + n] = (w * mu).astype(BF16)
        c0 += n


def _fold_lora(w1, a1, g1, mu_w, mu_a, mu_g):
    col = lambda v: v.reshape(-1, 1)
    return pl.pallas_call(
        _fold_lora_kernel,
        out_shape=jax.ShapeDtypeStruct((D_MODEL, 2 * LORA_WIDTH), BF16),
        name="fold_lora",
    )(w1, a1, g1, col(mu_w), col(mu_a), col(mu_g))


def _ret_tables_kernel(freq_ref, cos_ref, sin_ref, dmask_ref):
    C, HD = RET_CHUNK, RET_HEAD_DIM
    n = cos_ref.shape[0]
    ang = lax.broadcasted_iota(jnp.int32, (n, HD), 0).astype(F32) * freq_ref[...]
    lane = lax.broadcasted_iota(jnp.int32, (n, HD), 1)
    sin = jnp.sin(ang)
    cos_ref[...] = jnp.cos(ang)
    sin_ref[...] = jnp.where(lane < HD // 2, -sin, sin)
    ii = lax.broadcasted_iota(jnp.int32, (C, C), 0)
    jj = lax.broadcasted_iota(jnp.int32, (C, C), 1)
    diff = jnp.maximum((ii - jj).astype(F32), 0.0)
    for h in range(RET_HEADS):
        dmask_ref[h] = jnp.where(ii >= jj, jnp.exp(diff * _LOG_GAMMA[h]), 0.0)


def _ret_tables(freq, seq):
    return pl.pallas_call(
        _ret_tables_kernel,
        out_shape=[jax.ShapeDtypeStruct((seq, RET_HEAD_DIM), F32),
                   jax.ShapeDtypeStruct((seq, RET_HEAD_DIM), F32),
                   jax.ShapeDtypeStruct((RET_HEADS, RET_CHUNK, RET_CHUNK), F32)],
        name="ret_tables",
    )(freq)


(_V_MU_R, _V_MU_K, _V_MU_V, _V_W0, _V_A0, _V_KK, _V_KA, _V_RK, _V_LNW, _V_LNB) = range(10)
_N_VEC_ROWS = 16

_PREP_NAMES = ("a_t", "r_t", "b_t", "k_t", "b_w", "k_w", "w_c", "v", "bonus", "gate")

_PROJ_PIECES = ([("rkv", c) for c in range(0, N_RKV, PROJ_PIECE)]
                + [("ret", c) for c in range(N_RKV, N_RKV + N_RET, PROJ_PIECE)]
                + [("lora", c) for c in range(N_RKV + N_RET, N_PROJ, PROJ_PIECE)])


def _mixer_kernel(xfirst_ref, xnext_ref, gmix_ref, win_hbm, wlo_ref, vec_ref, w2_ref, seg_ref, tril_ref, cos_ref,
                  sin_ref, dmask_ref, gnw_ref, out_ref, yret_ref,
                  h_ref, rstate_ref, prev_ref, prep0_ref, pret0_ref, proj1_ref, prj0_ref,
                  win_ref, wstage_ref, wsem):
    C, G, TT, STG = RWKV_CHUNK, RWKV_GROUP_HEADS, MIX_TILE, MIX_STAGE
    HD = RWKV_HEAD_DIM
    W = G * HD
    GC = G * C
    n_groups = RWKV_WIDTH // W
    n_stages = TT // STG
    stage_rows = [slice(i * STG, (i + 1) * STG) for i in range(n_stages)]

    @pl.when(pl.program_id(1) == 0)
    def _():
        h_ref[...] = jnp.zeros_like(h_ref)
        rstate_ref[...] = jnp.zeros_like(rstate_ref)

    vec = lambda i: vec_ref[i:i + 1, :]

    seg = seg_ref[...]

    def head_sum(x):
        n_blk = x.shape[1] // LANES
        n_rows = x.shape[0]
        rows = jnp.concatenate([x[:, j * LANES:(j + 1) * LANES] for j in range(n_blk)], axis=0)
        s = _dot(rows.astype(BF16), seg)
        return jnp.concatenate([s[j * n_rows:(j + 1) * n_rows] for j in range(n_blk)], axis=1)

    def project(x_rows_ref, stage):
        got = {}

        def piece(i):
            def run():
                if "xb" not in got:
                    got["xb"] = _rms_norm(x_rows_ref[...], gmix_ref[...]).astype(BF16)
                name, c0 = _PROJ_PIECES[i]
                n_in = N_RKV + N_RET
                w = win_ref[:, c0:c0 + PROJ_PIECE] if c0 < n_in else wlo_ref[:, c0 - n_in:c0 - n_in + PROJ_PIECE]
                res = _dot(got["xb"], w)
                if stage == 1:
                    proj1_ref[:, c0:c0 + PROJ_PIECE] = res
                elif name == "ret":
                    pret0_ref[:, c0 - N_RKV:c0 - N_RKV + PROJ_PIECE] = res
                else:
                    d0 = c0 if name == "rkv" else c0 - N_RET
                    prj0_ref[:, d0:d0 + PROJ_PIECE] = res
            return run
        front = [piece(i) for i, (name, _) in enumerate(_PROJ_PIECES) if name != "ret"]
        ret = [piece(i) for i, (name, _) in enumerate(_PROJ_PIECES) if name == "ret"]
        return front, ret

    def prepare(stage, prev_rows, store):
        pp = {}

        def put(name, val):
            if store:
                prep0_ref[_PREP_NAMES.index(name)] = val
            else:
                pp[name] = val

        def prep_a():
            if stage == 1:
                p_rkv, p_lora = proj1_ref[:, 0:N_RKV], proj1_ref[:, N_RKV + N_RET:N_PROJ]
            else:
                p_rkv, p_lora = prj0_ref[:, 0:N_RKV], prj0_ref[:, N_RKV:N_RKV + N_LORA]
            prev_rkv, prev_lora = prev_rows()
            s_rkv = _shift_rows(p_rkv, prev_rkv)
            lerp = lambda j, mu: (p_rkv[:, j * RWKV_WIDTH:(j + 1) * RWKV_WIDTH]
                                  + (s_rkv[:, j * RWKV_WIDTH:(j + 1) * RWKV_WIDTH]
                                     - p_rkv[:, j * RWKV_WIDTH:(j + 1) * RWKV_WIDTH]) * mu)
            if store:
                prev_ref[:, 0:N_RKV] = p_rkv[STG - SUBLANES:]
                prev_ref[:, N_RKV:N_RKV + N_LORA] = p_lora[STG - SUBLANES:]
            pp["r"] = lerp(0, vec(_V_MU_R))
            pp["k"] = lerp(1, vec(_V_MU_K))
            pp["v"] = lerp(2, vec(_V_MU_V))
            put("v", pp["v"])
            low = p_lora[:, 0:LORA_WIDTH] + _shift_rows(p_lora, prev_lora)[:, LORA_WIDTH:2 * LORA_WIDTH]
            lane = lax.broadcasted_iota(jnp.int32, low.shape, 1)
            act = jnp.where(lane < DECAY_LORA, jnp.tanh(low),
                            jnp.where(lane < DECAY_LORA + AAA_LORA, low, jax.nn.sigmoid(low)))
            pp["second"] = _dot(act.astype(BF16), w2_ref[...])

        def prep_b():
            second = pp["second"]
            pp["ld"] = -math.exp(-0.5) * jax.nn.sigmoid(second[:, 0:RWKV_WIDTH] + vec(_V_W0))
            pp["a"] = jax.nn.sigmoid(second[:, RWKV_WIDTH:2 * RWKV_WIDTH] + vec(_V_A0))
            put("gate", second[:, 2 * RWKV_WIDTH:3 * RWKV_WIDTH])
            pp["kk"] = pp["k"] * vec(_V_KK)
            pp["kk_ss"] = head_sum(pp["kk"] * pp["kk"])
            pp["cum"] = _split_dot(tril_ref[...], pp["ld"])

        def prep_c():
            pp["kk"] = pp["kk"] * lax.rsqrt(jnp.maximum(pp["kk_ss"], 1e-24))
            pp["k2"] = pp["k"] * (1.0 + (pp["a"] - 1.0) * vec(_V_KA))
            put("bonus", head_sum(pp["r"] * pp["k2"] * vec(_V_RK)) * pp["v"])

        def prep_d():
            cum, ld, kk, k2 = pp["cum"], pp["ld"], pp["kk"], pp["k2"]
            b = kk * pp["a"]
            cum_end = jnp.concatenate(
                [jnp.broadcast_to(cum[(c + 1) * C - 1:(c + 1) * C, :], (C, RWKV_WIDTH))
                 for c in range(STG // C)], axis=0)
            e_neg = jnp.exp(-cum)
            e_end = jnp.exp(cum_end - cum)
            put("a_t", -kk * jnp.exp(cum - ld))
            put("r_t", pp["r"] * jnp.exp(cum))
            put("b_t", b * e_neg)
            put("k_t", k2 * e_neg)
            put("b_w", b * e_end)
            put("k_w", k2 * e_end)
            put("w_c", jnp.exp(cum_end))

        return pp, [prep_a, prep_b, prep_c, prep_d]

    def run_all(thunks):
        for thunk in thunks:
            thunk()

    zero_prev = lambda: (jnp.zeros((1, N_RKV), F32), jnp.zeros((1, N_LORA), F32))

    @pl.when((pl.program_id(0) == 0) & (pl.program_id(1) == 0))
    def _():
        pieces = []
        for s in range(n_stages):
            front, ret = project(xfirst_ref.at[stage_rows[s], :], s)
            n_rkv = N_RKV // PROJ_PIECE
            pieces.append(front[:n_rkv] + ret + front[n_rkv:])
        copies = [pltpu.make_async_copy(win_hbm.at[:, pl.ds(c0, PROJ_PIECE)], wstage_ref.at[i % 2], wsem.at[i % 2])
                  for i, c0 in enumerate(range(0, N_RKV + N_RET, PROJ_PIECE))]
        copies[0].start()
        for i in range(len(_PROJ_PIECES)):
            if i < len(copies):
                if i + 1 < len(copies):
                    copies[i + 1].start()
                copies[i].wait()
                win_ref[:, i * PROJ_PIECE:(i + 1) * PROJ_PIECE] = wstage_ref[i % 2].astype(BF16)
            for s in range(n_stages):
                pieces[s][i]()
        run_all(prepare(0, zero_prev, store=True)[1])

    ret_state = [rstate_ref[h] for h in range(RET_HEADS)]

    def ret_unit(c, h):
        RC, RD = RET_CHUNK, RET_HEAD_DIM
        s, c_loc = divmod(c * RC, STG)
        rows = slice(c * RC, (c + 1) * RC)
        loc = slice(c_loc, c_loc + RC)
        row_i = lax.broadcasted_iota(jnp.int32, (RC, 1), 0).astype(F32)
        lg = _LOG_GAMMA[h]
        cos, sin = cos_ref[rows, :], sin_ref[rows, :]
        rot = lambda xh: xh * cos + pltpu.roll(xh, RD // 2, 1) * sin

        def part(j):
            c0 = j * RET_WIDTH + h * RD
            if s == 0:
                return pret0_ref[loc, c0:c0 + RD]
            return proj1_ref[loc, N_RKV + c0:N_RKV + c0 + RD]

        q = rot(part(0))
        kr = rot(part(1)) * (RD ** -0.5)
        vr = part(2)
        gt = part(3)
        qb, kb, vb = q.astype(BF16), kr.astype(BF16), vr.astype(BF16)
        scores = (_dot_nt(qb, kb) * dmask_ref[h]).astype(BF16)
        kv = _dot_tn(kb, (vr * jnp.exp((RC - 1.0 - row_i) * lg)).astype(BF16))
        inter = _dot((q * jnp.exp((row_i + 1.0) * lg)).astype(BF16), ret_state[h].astype(BF16))
        ret_state[h] = ret_state[h] * _GAMMA_C[h] + kv

        def second_half():
            y = _dot(scores, vb) + inter
            mu = jnp.mean(y, axis=-1, keepdims=True)
            d = y - mu
            var = jnp.mean(d * d, axis=-1, keepdims=True)
            yn = d * lax.rsqrt(var + RET_GN_EPS) * gnw_ref[:, h * RD:(h + 1) * RD]
            yret_ref[rows, h * RD:(h + 1) * RD] = (gt * jax.nn.sigmoid(gt) * yn).astype(BF16)
        return second_half

    ret_units = [(c, h) for c in range(TT // RET_CHUNK) for h in range(RET_HEADS)]
    ret_pending = []
    slot_queue = []

    def fill():
        if ret_pending:
            ret_pending.pop(0)()
        if ret_units:
            ret_pending.append(ret_unit(*ret_units.pop(0)))
        if slot_queue:
            for thunk in slot_queue.pop(0):
                thunk()

    c_bits = C.bit_length() - 1
    hd_bits = HD.bit_length() - 1
    assert C == 1 << c_bits and HD == 1 << hd_bits and W == GC
    sr = lax.broadcasted_iota(jnp.int32, (GC, W), 0)
    sl_ = lax.broadcasted_iota(jnp.int32, (GC, W), 1)
    stack_mask = (sr >> c_bits) == (sl_ >> hd_bits)
    ti = lax.broadcasted_iota(jnp.int32, (C, GC), 0)
    si = lax.broadcasted_iota(jnp.int32, (C, GC), 1) & (C - 1)
    strict = ti > si
    incl = ti >= si
    eye_c = (ti == si).astype(F32)
    same_block = {}
    size = INV_BASE
    while size <= C:
        bits = size.bit_length() - 1
        same_block[size] = (ti >> bits) == (si >> bits)
        size *= 2
    wi = lax.broadcasted_iota(jnp.int32, (W, W), 0)
    wj = lax.broadcasted_iota(jnp.int32, (W, W), 1)
    eye_w = wi == wj

    def stack(x):
        return jnp.where(stack_mask, jnp.concatenate([x] * G, axis=0), 0.0)

    blk = lambda z, c, g: z[c * C:(c + 1) * C, g * W:(g + 1) * W]

    def chains_of(p):
        st = []
        for c in range(STG // C):
            for g in range(n_groups):
                a_c, r_c = blk(p["a_t"], c, g), blk(p["r_t"], c, g)
                st.append(dict(
                    ar=jnp.concatenate([a_c, r_c], axis=0).astype(BF16), r32=r_c,
                    a_s=stack(a_c).astype(BF16),
                    bk_s=jnp.concatenate([stack(blk(p["b_t"], c, g)), stack(blk(p["k_t"], c, g))],
                                         axis=0).astype(BF16),
                    bw_t=stack(blk(p["b_w"], c, g)).T.astype(BF16),
                    kw_t=stack(blk(p["k_w"], c, g)).T.astype(BF16),
                    v_s=stack(blk(p["v"], c, g)).astype(BF16), w_end=blk(p["w_c"], c, g)[0:1, :]))
        for s in st:
            prod = _dot_nt(s["ar"], s["bk_s"])
            s["n"] = jnp.where(strict, prod[0:C, 0:GC], 0.0)
            s["a_ak"] = jnp.where(strict, prod[0:C, GC:], 0.0).astype(BF16)
            s["a_rb"] = jnp.where(incl, prod[C:, 0:GC], 0.0).astype(BF16)
            s["a_rk"] = jnp.where(incl, prod[C:, GC:], 0.0).astype(BF16)
        fill()
        for s in st:
            res = _dot(jnp.concatenate([s["a_ak"], s["a_rk"], s["kw_t"]], axis=0), s["v_s"])
            s["av_s"] = stack(res[0:C]).astype(BF16)
            s["rkv"] = res[C:2 * C]
            s["kwv"] = res[2 * C:]
        fill()
        for s in st:
            nd = jnp.where(same_block[INV_BASE], s["n"], 0.0)
            s["t"] = eye_c + nd
            s["p"] = _dot(nd.astype(BF16), stack(nd).astype(BF16))
        fill()
        for s in st:
            res = _dot(jnp.concatenate([s["t"], s["p"]], axis=0).astype(BF16), stack(s["p"]).astype(BF16))
            s["t"] = s["t"] + res[0:C]
            s["p"] = res[C:]
        for s in st:
            s["t"] = s["t"] + _dot(s["t"].astype(BF16), stack(s["p"]).astype(BF16))
        fill()
        size = INV_BASE
        while size < C:
            for s in st:
                b_off = jnp.where(same_block[2 * size] & ~same_block[size], s["n"], 0.0)
                s["p"] = _dot(b_off.astype(BF16), stack(s["t"]).astype(BF16))
            fill()
            for s in st:
                s["t"] = s["t"] + _dot(s["t"].astype(BF16), stack(s["p"]).astype(BF16))
            fill()
            size *= 2
        for s in st:
            s["t"] = s["t"].astype(BF16)
        for s in st:
            s["x1_s"] = stack(_dot(s["t"], s["a_s"])).astype(BF16)
            s["x2_s"] = stack(_dot(s["t"], s["av_s"])).astype(BF16)
        for s in st:
            lhs = jnp.concatenate([s["a_rb"], s["bw_t"]], axis=0)
            o1 = _dot(lhs, s["x1_s"])
            o2 = _dot(lhs, s["x2_s"])
            s["q"] = (s["r32"] + o1[0:C]).astype(BF16)
            s["m"] = (o1[C:] + jnp.where(eye_w, s["w_end"], 0.0)).astype(BF16)
            s["y0"] = o2[0:C] + s["rkv"]
            s["g"] = o2[C:] + s["kwv"]
        return st

    def state_steps(st, h_cur, fillers):
        y_rows = []
        n_chunks = STG // C
        per_step = -(-len(fillers) // n_chunks)
        for c in range(n_chunks):
            y_lanes = []
            for g in range(n_groups):
                s = st[c * n_groups + g]
                res = _dot(jnp.concatenate([s["q"], s["m"]], axis=0), h_cur[g].astype(BF16))
                y_lanes.append(res[0:C] + s["y0"])
                h_cur[g] = res[C:] + s["g"]
            y_rows.append(jnp.concatenate(y_lanes, axis=1))
            run_all(fillers[c * per_step:(c + 1) * per_step])
        return jnp.concatenate(y_rows, axis=0)

    def finish(rows, y, p):
        inv_hd = 1.0 / HD
        mu = head_sum(y) * inv_hd
        d = y - mu
        var = head_sum(d * d) * inv_hd
        yn = d * lax.rsqrt(var + RWKV_GN_EPS) * vec(_V_LNW) + vec(_V_LNB)
        out_ref[rows, :] = ((yn + p["bonus"]) * p["gate"]).astype(BF16)

    assert n_stages == 2
    h_cur = [h_ref[g] for g in range(n_groups)]
    prep_s0 = {name: prep0_ref.at[i] for i, name in enumerate(_PREP_NAMES)}
    prep_s0["bonus"], prep_s0["gate"] = prep_s0["bonus"][...], prep_s0["gate"][...]
    last_s1 = (proj1_ref[STG - 1:STG, 0:N_RKV], proj1_ref[STG - 1:STG, N_RKV + N_RET:N_PROJ])
    last_step_of_row = pl.program_id(1) == pl.num_programs(1) - 1

    def chains_with_fillers(p, thunks):
        for t in thunks:
            slot_queue.extend([[t], []])
        st = chains_of(p)
        while slot_queue:
            fill()
        return st

    def interleave(a, b):
        return [t for pair in zip(a, b) for t in pair]

    front_n0, ret_n0 = project(xnext_ref.at[stage_rows[0], :], 0)
    chains_s0 = chains_with_fillers(prep_s0, front_n0)

    prev_s1 = lambda: (prev_ref[SUBLANES - 1:SUBLANES, 0:N_RKV],
                       prev_ref[SUBLANES - 1:SUBLANES, N_RKV:N_RKV + N_LORA])
    prep_s1, thunks = prepare(1, prev_s1, store=False)
    y_s0 = state_steps(chains_s0, h_cur, interleave(thunks, ret_n0))

    front_n1, ret_n1 = project(xnext_ref.at[stage_rows[1], :], 1)
    chains_s1 = chains_with_fillers(prep_s1, front_n1)

    finish(stage_rows[0], y_s0, prep_s0)
    prev_n0 = lambda: tuple(jnp.where(last_step_of_row, 0.0, z) for z in last_s1)
    y_s1 = state_steps(chains_s1, h_cur, interleave(prepare(0, prev_n0, store=True)[1], ret_n1))
    finish(stage_rows[1], y_s1, prep_s1)
    while ret_units or ret_pending:
        fill()
    for g in range(n_groups):
        h_ref[g] = h_cur[g]
    for h in range(RET_HEADS):
        rstate_ref[h] = ret_state[h]


def _mixer(x2, g_mix, w_in, w_low, vecs, w2cat, seg, tril, freq, gn_w, bsz, seq):
    tt = MIX_TILE
    nt = seq // tt
    w = RWKV_GROUP_HEADS * RWKV_HEAD_DIM
    n_groups = RWKV_WIDTH // w
    cos, sin, dmask = _ret_tables(freq, seq)
    row = lambda b, t: (b * nt + t, 0)
    next_tile = lambda b, t: (jnp.minimum(b * nt + t + 1, bsz * nt - 1), 0)
    pos = lambda b, t: (t, 0)
    const = lambda b, t: (0, 0)
    return pl.pallas_call(
        _mixer_kernel,
        grid=(bsz, nt),
        in_specs=[
            pl.BlockSpec((tt, D_MODEL), const),
            pl.BlockSpec((tt, D_MODEL), next_tile),
            pl.BlockSpec((1, D_MODEL), const),
            pl.BlockSpec(memory_space=pl.ANY),
            pl.BlockSpec((D_MODEL, N_LORA), const, pipeline_mode=pl.Buffered(1)),
            pl.BlockSpec((_N_VEC_ROWS, RWKV_WIDTH), const),
            pl.BlockSpec((LORA_WIDTH, 3 * RWKV_WIDTH), const),
            pl.BlockSpec((LANES, LANES), const),
            pl.BlockSpec((MIX_STAGE, MIX_STAGE), const),
            pl.BlockSpec((tt, RET_HEAD_DIM), pos),
            pl.BlockSpec((tt, RET_HEAD_DIM), pos),
            pl.BlockSpec((RET_HEADS, RET_CHUNK, RET_CHUNK), lambda b, t: (0, 0, 0)),
            pl.BlockSpec((1, RET_WIDTH), const),
        ],
        out_specs=[pl.BlockSpec((tt, RWKV_WIDTH), row), pl.BlockSpec((tt, RET_WIDTH), row)],
        out_shape=[jax.ShapeDtypeStruct((bsz * seq, RWKV_WIDTH), BF16),
                   jax.ShapeDtypeStruct((bsz * seq, RET_WIDTH), BF16)],
        scratch_shapes=[pltpu.VMEM((n_groups, w, w), F32),
                        pltpu.VMEM((RET_HEADS, RET_HEAD_DIM, RET_HEAD_DIM), F32),
                        pltpu.VMEM((SUBLANES, N_RKV + N_LORA), F32),
                        pltpu.VMEM((len(_PREP_NAMES), MIX_STAGE, RWKV_WIDTH), F32),
                        pltpu.VMEM((MIX_STAGE, N_RET), F32),
                        pltpu.VMEM((MIX_STAGE, N_PROJ), F32),
                        pltpu.VMEM((MIX_STAGE, N_RKV + N_LORA), F32),
                        pltpu.VMEM((D_MODEL, N_RKV + N_RET), BF16),
                        pltpu.VMEM((2, D_MODEL, PROJ_PIECE), F32),
                        pltpu.SemaphoreType.DMA((2,))],
        compiler_params=pltpu.CompilerParams(
            dimension_semantics=("arbitrary", "arbitrary"), vmem_limit_bytes=_vmem_limit(56 * 1024 * 1024)),
        name="mixer",
    )(x2, x2, g_mix, w_in, w_low, vecs, w2cat, seg, tril, cos, sin, dmask, gn_w)


def _cast_rows_to_bf16(jobs, sem):
    copies = [pltpu.make_async_copy(src, stg, sem.at[i % 2]) for i, (src, stg, _) in enumerate(jobs)]
    copies[0].start()
    for i, (_, stg, dst) in enumerate(jobs):
        if i + 1 < len(jobs):
            copies[i + 1].start()
        copies[i].wait()
        dst[...] = stg[...].astype(BF16)


def _ffn_weight_jobs(wo_hbm, wg_hbm, wu_hbm, wd_hbm, wo_ref, wg_ref, wu_ref, wd_ref, wide_ref, tall_ref):
    jobs = []
    for hbm, vm in ((wg_hbm, wg_ref), (wu_hbm, wu_ref)):
        for r in range(0, D_MODEL, FFN_WIDE_ROWS):
            rows = pl.ds(r, FFN_WIDE_ROWS)
            jobs.append((hbm.at[rows, :], wide_ref.at[len(jobs) % 2], vm.at[rows, :]))
    for r in range(0, D_FF, FFN_TALL_ROWS):
        rows = pl.ds(r, FFN_TALL_ROWS)
        jobs.append((wd_hbm.at[rows, :], tall_ref.at[len(jobs) % 2], wd_ref.at[rows, :]))
    half = D_MODEL // 2
    for r in range(0, D_MODEL, half):
        rows = pl.ds(r, half)
        jobs.append((wo_hbm.at[rows, :], tall_ref.at[len(jobs) % 2, pl.ds(0, half), :], wo_ref.at[rows, :]))
    return jobs


def _ffn_kernel(yr_ref, yt_ref, x_ref, wo_hbm, gf_ref, wg_hbm, wu_hbm, cw_ref, cb_ref, wd_hbm, gl_ref,
                out_ref, carry_ref, wo_ref, wg_ref, wu_ref, wd_ref, wide_ref, tall_ref, sem):
    sm = FFN_TILE // FFN_STAGES
    stages = [slice(i * sm, (i + 1) * sm) for i in range(FFN_STAGES)]

    @pl.when((pl.program_id(0) == 0) & (pl.program_id(1) == 0))
    def _():
        _cast_rows_to_bf16(_ffn_weight_jobs(wo_hbm, wg_hbm, wu_hbm, wd_hbm, wo_ref, wg_ref, wu_ref, wd_ref,
                                            wide_ref, tall_ref), sem)

    @pl.when(pl.program_id(1) == 0)
    def _():
        carry_ref[...] = jnp.zeros_like(carry_ref)

    mix = [_dot(yr_ref[s, :], wo_ref[0:RWKV_WIDTH, :]) + _dot(yt_ref[s, :], wo_ref[RWKV_WIDTH:, :])
           for s in stages]
    acc = [x_ref[s, :] + m for s, m in zip(stages, mix)]
    hb = [_rms_norm(x1, gf_ref[...]).astype(BF16) for x1 in acc]

    cols = D_FF // FFN_COL_SPLIT
    for j in range(FFN_COL_SPLIT):
        cs = slice(j * cols, (j + 1) * cols)
        gate_up = [(_dot(h, wg_ref[:, cs]), _dot(h, wu_ref[:, cs])) for h in hb]
        prev = carry_ref[:, cs]
        p2, p1 = prev[SUBLANES - 2:SUBLANES - 1], prev[SUBLANES - 1:SUBLANES]
        hidden = []
        for gate, up in gate_up:
            row = lax.broadcasted_iota(jnp.int32, gate.shape, 0)
            g1 = jnp.where(row == 0, p1, pltpu.roll(gate, 1, 0))
            g2 = jnp.where(row == 0, p2, jnp.where(row == 1, p1, pltpu.roll(gate, 2, 0)))
            p2, p1 = gate[sm - 2:sm - 1], gate[sm - 1:sm]
            conv = cw_ref[0:1, cs] * g2 + cw_ref[1:2, cs] * g1 + cw_ref[2:3, cs] * gate + cb_ref[:, cs]
            hidden.append((conv * jax.nn.sigmoid(conv) * up).astype(BF16))
        carry_ref[:, cs] = gate_up[-1][0][sm - SUBLANES:sm]
        acc = [a + _dot(h, wd_ref[cs, :]) for a, h in zip(acc, hidden)]
    for s, a in zip(stages, acc):
        out_ref[s, :] = _rms_norm(a, gl_ref[...])


def _ffn(y_rwkv, y_ret, x2, w_out, g_ffn, w_gate, w_up, conv_w, conv_b, w_down, g_final, bsz, seq):
    tm = FFN_TILE
    nt = seq // tm
    row = lambda b, t: (b * nt + t, 0)
    const = lambda b, t: (0, 0)
    in_hbm = pl.BlockSpec(memory_space=pl.ANY)
    weights = (D_MODEL * D_MODEL + 3 * D_MODEL * D_FF) * 2
    staging = 2 * (FFN_WIDE_ROWS * D_FF + FFN_TALL_ROWS * D_MODEL) * 4
    vmem = weights + staging + 4 * tm * D_MODEL * 4 + 8 * tm * D_FF * 4 + 8 * 1024 * 1024
    return pl.pallas_call(
        _ffn_kernel,
        grid=(bsz, nt),
        in_specs=[
            pl.BlockSpec((tm, RWKV_WIDTH), row),
            pl.BlockSpec((tm, RET_WIDTH), row),
            pl.BlockSpec((tm, D_MODEL), row),
            in_hbm,
            pl.BlockSpec((1, D_MODEL), const),
            in_hbm,
            in_hbm,
            pl.BlockSpec((SUBLANES, D_FF), const),
            pl.BlockSpec((1, D_FF), const),
            in_hbm,
            pl.BlockSpec((1, D_MODEL), const),
        ],
        out_specs=pl.BlockSpec((tm, D_MODEL), row),
        out_shape=jax.ShapeDtypeStruct((bsz * seq, D_MODEL), F32),
        scratch_shapes=[pltpu.VMEM((SUBLANES, D_FF), F32),
                        pltpu.VMEM((D_MODEL, D_MODEL), BF16),
                        pltpu.VMEM((D_MODEL, D_FF), BF16),
                        pltpu.VMEM((D_MODEL, D_FF), BF16),
                        pltpu.VMEM((D_FF, D_MODEL), BF16),
                        pltpu.VMEM((2, FFN_WIDE_ROWS, D_FF), F32),
                        pltpu.VMEM((2, FFN_TALL_ROWS, D_MODEL), F32),
                        pltpu.SemaphoreType.DMA((2,))],
        compiler_params=pltpu.CompilerParams(
            dimension_semantics=("arbitrary", "arbitrary"), vmem_limit_bytes=_vmem_limit(vmem)),
        name="ffn",
    )(y_rwkv, y_ret, x2, w_out, g_ffn, w_gate, w_up, conv_w, conv_b, w_down, g_final)


def _block_ones(n, block):
    i = np.arange(n)
    return (i[:, None] // block) == (i[None, :] // block)


def kernel(x, norm_mix_g, w_in, rwkv_mu_r, rwkv_mu_k, rwkv_mu_v, rwkv_mu_w, rwkv_mu_a, rwkv_mu_g, rwkv_w0, rwkv_w1, rwkv_w2, rwkv_a0, rwkv_a1, rwkv_a2, rwkv_g1, rwkv_g2, rwkv_k_k, rwkv_k_a, rwkv_r_k, rwkv_lnx_w, rwkv_lnx_b, ret_gn_w, w_out, norm_ffn_g, ffn_w_gate, ffn_w_up, ffn_conv_w, ffn_conv_b, ffn_w_down, norm_final_g):
    bsz, seq, d = x.shape
    assert d == D_MODEL and seq % MIX_TILE == 0 and seq % FFN_TILE == 0
    assert norm_mix_g.shape[0] == 1, "one layer"
    x2 = x.reshape(bsz * seq, d)
    row = lambda p: p.reshape(1, -1)

    w_low = _fold_lora(rwkv_w1[0], rwkv_a1[0], rwkv_g1[0], rwkv_mu_w[0], rwkv_mu_a[0], rwkv_mu_g[0])

    vec_rows = [rwkv_mu_r[0], rwkv_mu_k[0], rwkv_mu_v[0], rwkv_w0[0], rwkv_a0[0], rwkv_k_k[0], rwkv_k_a[0],
                rwkv_r_k[0].reshape(-1), rwkv_lnx_w[0], rwkv_lnx_b[0]]
    vecs = jnp.zeros((_N_VEC_ROWS, RWKV_WIDTH), F32).at[:len(vec_rows)].set(jnp.stack(vec_rows))
    w2cat = jnp.zeros((LORA_WIDTH, 3 * RWKV_WIDTH), BF16)
    w2cat = w2cat.at[0:DECAY_LORA, 0:RWKV_WIDTH].set(rwkv_w2[0].astype(BF16))
    w2cat = w2cat.at[DECAY_LORA:DECAY_LORA + AAA_LORA, RWKV_WIDTH:2 * RWKV_WIDTH].set(rwkv_a2[0].astype(BF16))
    w2cat = w2cat.at[DECAY_LORA + AAA_LORA:, 2 * RWKV_WIDTH:].set(rwkv_g2[0].astype(BF16))
    seg = jnp.asarray(_block_ones(LANES, RWKV_HEAD_DIM), BF16)
    ti = np.arange(MIX_STAGE)
    tril = jnp.asarray(_block_ones(MIX_STAGE, RWKV_CHUNK) & (ti[:, None] >= ti[None, :]), BF16)
    half = RET_HEAD_DIM // 2
    inv_freq = ROPE_BASE ** (-jnp.arange(half, dtype=F32) / half)
    freq = jnp.concatenate([inv_freq, inv_freq]).reshape(1, RET_HEAD_DIM)
    y_rwkv, y_ret = _mixer(x2, row(norm_mix_g[0]), w_in[0], w_low, vecs, w2cat, seg, tril, freq,
                           row(ret_gn_w[0]),
                           bsz, seq)

    conv_w = jnp.zeros((SUBLANES, D_FF), F32).at[0:3].set(ffn_conv_w[0][:, 0, :])
    out = _ffn(y_rwkv, y_ret, x2, w_out[0], row(norm_ffn_g[0]), ffn_w_gate[0], ffn_w_up[0], conv_w,
               row(ffn_conv_b[0]), ffn_w_down[0], row(norm_final_g), bsz, seq)
    return out.reshape(bsz, seq, d)
```

```python
import math

import jax
import jax.numpy as jnp
import numpy as np
from jax import lax
from jax.experimental import pallas as pl
from jax.experimental.pallas import tpu as pltpu

F32 = jnp.float32
BF16 = jnp.bfloat16

D_MODEL = 1024
RWKV_HEADS = 8
RWKV_HEAD_DIM = 64
RWKV_WIDTH = 512
RET_HEADS = 4
RET_HEAD_DIM = 128
RET_WIDTH = 512
DECAY_LORA = 64
AAA_LORA = 64
GATE_LORA = 128
LORA_WIDTH = DECAY_LORA + AAA_LORA + GATE_LORA
RET_CHUNK = 128
ROPE_BASE = 10000.0
D_FF = 2816
NORM_EPS = 1e-6
RWKV_GN_EPS = 64e-5
RET_GN_EPS = 1e-5

V7X_VMEM_BYTES = 64 * 1024 * 1024
SUBLANES = 8
LANES = 128

RWKV_CHUNK = 64
INV_BASE = 8
RWKV_GROUP_HEADS = 2
MIX_TILE = 512
MIX_STAGE = 256
PROJ_PIECE = 512
FFN_TILE = 512
FFN_STAGES = 2
FFN_COL_SPLIT = 2
FFN_WIDE_ROWS = 256
FFN_TALL_ROWS = 704

N_RKV = 3 * RWKV_WIDTH
N_RET = 4 * RET_WIDTH
N_LORA = 2 * LORA_WIDTH
N_PROJ = N_RKV + N_RET + N_LORA

_LOG_GAMMA = [math.log(1.0 - 2.0 ** (-5.0 - h)) for h in range(RET_HEADS)]
_GAMMA_C = [math.exp(RET_CHUNK * lg) for lg in _LOG_GAMMA]


def _vmem_limit(nbytes):
    return int(min(nbytes, V7X_VMEM_BYTES - 4 * 1024 * 1024))


def _dot(a, b):
    return jnp.dot(a, b, preferred_element_type=F32)


def _dot_nt(a, b):
    return lax.dot_general(a, b, (((1,), (1,)), ((), ())), preferred_element_type=F32)


def _dot_tn(a, b):
    return lax.dot_general(a, b, (((0,), (0,)), ((), ())), preferred_element_type=F32)


def _split_dot(mat_bf16, x):
    hi = x.astype(BF16)
    lo = (x - hi.astype(F32)).astype(BF16)
    return _dot(mat_bf16, hi) + _dot(mat_bf16, lo)


def _shift_rows(x, prev_row):
    rolled = pltpu.roll(x, 1, 0)
    row = lax.broadcasted_iota(jnp.int32, x.shape, 0)
    return jnp.where(row == 0, prev_row, rolled)


def _rms_norm(x, g):
    ms = jnp.mean(x * x, axis=-1, keepdims=True)
    return x * lax.rsqrt(ms + NORM_EPS) * g


def _fold_lora_kernel(w1_ref, a1_ref, g1_ref, muw_ref, mua_ref, mug_ref, o_ref):
    c0 = 0
    for w_ref, mu_ref in ((w1_ref, muw_ref), (a1_ref, mua_ref), (g1_ref, mug_ref)):
        w, mu = w_ref[...], mu_ref[...]
        n = w.shape[1]
        o_ref[:, c0:c0 + n] = (w * (1.0 - mu)).astype(BF16)
        o_ref[:, LORA_WIDTH + c0:LORA_WIDTH + c0 + n] = (w * mu).astype(BF16)
        c0 += n


def _fold_lora(w1, a1, g1, mu_w, mu_a, mu_g):
    col = lambda v: v.reshape(-1, 1)
    return pl.pallas_call(
        _fold_lora_kernel,
        out_shape=jax.ShapeDtypeStruct((D_MODEL, 2 * LORA_WIDTH), BF16),
        name="fold_lora",
    )(w1, a1, g1, col(mu_w), col(mu_a), col(mu_g))


def _ret_tables_kernel(freq_ref, cos_ref, sin_ref, dmask_ref):
    C, HD = RET_CHUNK, RET_HEAD_DIM
    n = cos_ref.shape[0]
    ang = lax.broadcasted_iota(jnp.int32, (n, HD), 0).astype(F32) * freq_ref[...]
    lane = lax.broadcasted_iota(jnp.int32, (n, HD), 1)
    sin = jnp.sin(ang)
    cos_ref[...] = jnp.cos(ang)
    sin_ref[...] = jnp.where(lane < HD // 2, -sin, sin)
    ii = lax.broadcasted_iota(jnp.int32, (C, C), 0)
    jj = lax.broadcasted_iota(jnp.int32, (C, C), 1)
    diff = jnp.maximum((ii - jj).astype(F32), 0.0)
    for h in range(RET_HEADS):
        dmask_ref[h] = jnp.where(ii >= jj, jnp.exp(diff * _LOG_GAMMA[h]), 0.0)


def _ret_tables(freq, seq):
    return pl.pallas_call(
        _ret_tables_kernel,
        out_shape=[jax.ShapeDtypeStruct((seq, RET_HEAD_DIM), F32),
                   jax.ShapeDtypeStruct((seq, RET_HEAD_DIM), F32),
                   jax.ShapeDtypeStruct((RET_HEADS, RET_CHUNK, RET_CHUNK), F32)],
        name="ret_tables",
    )(freq)


(_V_MU_R, _V_MU_K, _V_MU_V, _V_W0, _V_A0, _V_KK, _V_KA, _V_RK, _V_LNW, _V_LNB) = range(10)
_N_VEC_ROWS = 16

_PREP_NAMES = ("a_t", "r_t", "b_t", "k_t", "b_w", "k_w", "w_c", "v", "bonus", "gate")

_PROJ_PIECES = ([("rkv", c) for c in range(0, N_RKV, PROJ_PIECE)]
                + [("ret", c) for c in range(N_RKV, N_RKV + N_RET, PROJ_PIECE)]
                + [("lora", c) for c in range(N_RKV + N_RET, N_PROJ, PROJ_PIECE)])


def _mixer_kernel(xfirst_ref, xnext_ref, gmix_ref, win_ref, wlo_ref, vec_ref, w2_ref, seg_ref, tril_ref, cos_ref,
                  sin_ref, dmask_ref, gnw_ref, out_ref, yret_ref,
                  h_ref, rstate_ref, prev_ref, prep0_ref, pret0_ref, proj1_ref, prj0_ref):
    C, G, TT, STG = RWKV_CHUNK, RWKV_GROUP_HEADS, MIX_TILE, MIX_STAGE
    HD = RWKV_HEAD_DIM
    W = G * HD
    GC = G * C
    n_groups = RWKV_WIDTH // W
    n_stages = TT // STG
    stage_rows = [slice(i * STG, (i + 1) * STG) for i in range(n_stages)]

    @pl.when(pl.program_id(1) == 0)
    def _():
        h_ref[...] = jnp.zeros_like(h_ref)
        rstate_ref[...] = jnp.zeros_like(rstate_ref)

    vec = lambda i: vec_ref[i:i + 1, :]

    seg = seg_ref[...]

    def head_sum(x):
        n_blk = x.shape[1] // LANES
        n_rows = x.shape[0]
        rows = jnp.concatenate([x[:, j * LANES:(j + 1) * LANES] for j in range(n_blk)], axis=0)
        s = _dot(rows.astype(BF16), seg)
        return jnp.concatenate([s[j * n_rows:(j + 1) * n_rows] for j in range(n_blk)], axis=1)

    def project(x_rows_ref, stage):
        got = {}

        def piece(i):
            def run():
                if "xb" not in got:
                    got["xb"] = _rms_norm(x_rows_ref[...], gmix_ref[...]).astype(BF16)
                name, c0 = _PROJ_PIECES[i]
                n_in = N_RKV + N_RET
                w = win_ref[:, c0:c0 + PROJ_PIECE] if c0 < n_in else wlo_ref[:, c0 - n_in:c0 - n_in + PROJ_PIECE]
                res = _dot(got["xb"], w)
                if stage == 1:
                    proj1_ref[:, c0:c0 + PROJ_PIECE] = res
                elif name == "ret":
                    pret0_ref[:, c0 - N_RKV:c0 - N_RKV + PROJ_PIECE] = res
                else:
                    d0 = c0 if name == "rkv" else c0 - N_RET
                    prj0_ref[:, d0:d0 + PROJ_PIECE] = res
            return run
        front = [piece(i) for i, (name, _) in enumerate(_PROJ_PIECES) if name != "ret"]
        ret = [piece(i) for i, (name, _) in enumerate(_PROJ_PIECES) if name == "ret"]
        return front, ret

    def prepare(stage, prev_rows, store):
        pp = {}

        def put(name, val):
            if store:
                prep0_ref[_PREP_NAMES.index(name)] = val
            else:
                pp[name] = val

        def prep_a():
            if stage == 1:
                p_rkv, p_lora = proj1_ref[:, 0:N_RKV], proj1_ref[:, N_RKV + N_RET:N_PROJ]
            else:
                p_rkv, p_lora = prj0_ref[:, 0:N_RKV], prj0_ref[:, N_RKV:N_RKV + N_LORA]
            prev_rkv, prev_lora = prev_rows()
            s_rkv = _shift_rows(p_rkv, prev_rkv)
            lerp = lambda j, mu: (p_rkv[:, j * RWKV_WIDTH:(j + 1) * RWKV_WIDTH]
                                  + (s_rkv[:, j * RWKV_WIDTH:(j + 1) * RWKV_WIDTH]
                                     - p_rkv[:, j * RWKV_WIDTH:(j + 1) * RWKV_WIDTH]) * mu)
            if store:
                prev_ref[:, 0:N_RKV] = p_rkv[STG - SUBLANES:]
                prev_ref[:, N_RKV:N_RKV + N_LORA] = p_lora[STG - SUBLANES:]
            pp["r"] = lerp(0, vec(_V_MU_R))
            pp["k"] = lerp(1, vec(_V_MU_K))
            pp["v"] = lerp(2, vec(_V_MU_V))
            put("v", pp["v"])
            low = p_lora[:, 0:LORA_WIDTH] + _shift_rows(p_lora, prev_lora)[:, LORA_WIDTH:2 * LORA_WIDTH]
            lane = lax.broadcasted_iota(jnp.int32, low.shape, 1)
            act = jnp.where(lane < DECAY_LORA, jnp.tanh(low),
                            jnp.where(lane < DECAY_LORA + AAA_LORA, low, jax.nn.sigmoid(low)))
            pp["second"] = _dot(act.astype(BF16), w2_ref[...])

        def prep_b():
            second = pp["second"]
            pp["ld"] = -math.exp(-0.5) * jax.nn.sigmoid(second[:, 0:RWKV_WIDTH] + vec(_V_W0))
            pp["a"] = jax.nn.sigmoid(second[:, RWKV_WIDTH:2 * RWKV_WIDTH] + vec(_V_A0))
            put("gate", second[:, 2 * RWKV_WIDTH:3 * RWKV_WIDTH])
            pp["kk"] = pp["k"] * vec(_V_KK)
            pp["kk_ss"] = head_sum(pp["kk"] * pp["kk"])
            pp["cum"] = _split_dot(tril_ref[...], pp["ld"])

        def prep_c():
            pp["kk"] = pp["kk"] * lax.rsqrt(jnp.maximum(pp["kk_ss"], 1e-24))
            pp["k2"] = pp["k"] * (1.0 + (pp["a"] - 1.0) * vec(_V_KA))
            put("bonus", head_sum(pp["r"] * pp["k2"] * vec(_V_RK)) * pp["v"])

        def prep_d():
            cum, ld, kk, k2 = pp["cum"], pp["ld"], pp["kk"], pp["k2"]
            b = kk * pp["a"]
            cum_end = jnp.concatenate(
                [jnp.broadcast_to(cum[(c + 1) * C - 1:(c + 1) * C, :], (C, RWKV_WIDTH))
                 for c in range(STG // C)], axis=0)
            e_neg = jnp.exp(-cum)
            e_end = jnp.exp(cum_end - cum)
            put("a_t", -kk * jnp.exp(cum - ld))
            put("r_t", pp["r"] * jnp.exp(cum))
            put("b_t", b * e_neg)
            put("k_t", k2 * e_neg)
            put("b_w", b * e_end)
            put("k_w", k2 * e_end)
            put("w_c", jnp.exp(cum_end))

        return pp, [prep_a, prep_b, prep_c, prep_d]

    def run_all(thunks):
        for thunk in thunks:
            thunk()

    zero_prev = lambda: (jnp.zeros((1, N_RKV), F32), jnp.zeros((1, N_LORA), F32))

    @pl.when((pl.program_id(0) == 0) & (pl.program_id(1) == 0))
    def _():
        for s in range(n_stages):
            front, ret = project(xfirst_ref.at[stage_rows[s], :], s)
            run_all(front + ret)
        run_all(prepare(0, zero_prev, store=True)[1])

    ret_state = [rstate_ref[h] for h in range(RET_HEADS)]

    def ret_unit(c, h):
        RC, RD = RET_CHUNK, RET_HEAD_DIM
        s, c_loc = divmod(c * RC, STG)
        rows = slice(c * RC, (c + 1) * RC)
        loc = slice(c_loc, c_loc + RC)
        row_i = lax.broadcasted_iota(jnp.int32, (RC, 1), 0).astype(F32)
        lg = _LOG_GAMMA[h]
        cos, sin = cos_ref[rows, :], sin_ref[rows, :]
        rot = lambda xh: xh * cos + pltpu.roll(xh, RD // 2, 1) * sin

        def part(j):
            c0 = j * RET_WIDTH + h * RD
            if s == 0:
                return pret0_ref[loc, c0:c0 + RD]
            return proj1_ref[loc, N_RKV + c0:N_RKV + c0 + RD]

        q = rot(part(0))
        kr = rot(part(1)) * (RD ** -0.5)
        vr = part(2)
        gt = part(3)
        qb, kb, vb = q.astype(BF16), kr.astype(BF16), vr.astype(BF16)
        scores = (_dot_nt(qb, kb) * dmask_ref[h]).astype(BF16)
        kv = _dot_tn(kb, (vr * jnp.exp((RC - 1.0 - row_i) * lg)).astype(BF16))
        inter = _dot((q * jnp.exp((row_i + 1.0) * lg)).astype(BF16), ret_state[h].astype(BF16))
        ret_state[h] = ret_state[h] * _GAMMA_C[h] + kv

        def second_half():
            y = _dot(scores, vb) + inter
            mu = jnp.mean(y, axis=-1, keepdims=True)
            d = y - mu
            var = jnp.mean(d * d, axis=-1, keepdims=True)
            yn = d * lax.rsqrt(var + RET_GN_EPS) * gnw_ref[:, h * RD:(h + 1) * RD]
            yret_ref[rows, h * RD:(h + 1) * RD] = (gt * jax.nn.sigmoid(gt) * yn).astype(BF16)
        return second_half

    ret_units = [(c, h) for c in range(TT // RET_CHUNK) for h in range(RET_HEADS)]
    ret_pending = []
    slot_queue = []

    def fill():
        if ret_pending:
            ret_pending.pop(0)()
        if ret_units:
            ret_pending.append(ret_unit(*ret_units.pop(0)))
        if slot_queue:
            for thunk in slot_queue.pop(0):
                thunk()

    c_bits = C.bit_length() - 1
    hd_bits = HD.bit_length() - 1
    assert C == 1 << c_bits and HD == 1 << hd_bits and W == GC
    sr = lax.broadcasted_iota(jnp.int32, (GC, W), 0)
    sl_ = lax.broadcasted_iota(jnp.int32, (GC, W), 1)
    stack_mask = (sr >> c_bits) == (sl_ >> hd_bits)
    ti = lax.broadcasted_iota(jnp.int32, (C, GC), 0)
    si = lax.broadcasted_iota(jnp.int32, (C, GC), 1) & (C - 1)
    strict = ti > si
    incl = ti >= si
    eye_c = (ti == si).astype(F32)
    same_block = {}
    size = INV_BASE
    while size <= C:
        bits = size.bit_length() - 1
        same_block[size] = (ti >> bits) == (si >> bits)
        size *= 2
    wi = lax.broadcasted_iota(jnp.int32, (W, W), 0)
    wj = lax.broadcasted_iota(jnp.int32, (W, W), 1)
    eye_w = wi == wj

    def stack(x):
        return jnp.where(stack_mask, jnp.concatenate([x] * G, axis=0), 0.0)

    blk = lambda z, c, g: z[c * C:(c + 1) * C, g * W:(g + 1) * W]

    def chains_of(p):
        st = []
        for c in range(STG // C):
            for g in range(n_groups):
                a_c, r_c = blk(p["a_t"], c, g), blk(p["r_t"], c, g)
                st.append(dict(
                    ar=jnp.concatenate([a_c, r_c], axis=0).astype(BF16), r32=r_c,
                    a_s=stack(a_c).astype(BF16),
                    bk_s=jnp.concatenate([stack(blk(p["b_t"], c, g)), stack(blk(p["k_t"], c, g))],
                                         axis=0).astype(BF16),
                    bw_t=stack(blk(p["b_w"], c, g)).T.astype(BF16),
                    kw_t=stack(blk(p["k_w"], c, g)).T.astype(BF16),
                    v_s=stack(blk(p["v"], c, g)).astype(BF16), w_end=blk(p["w_c"], c, g)[0:1, :]))
        for s in st:
            prod = _dot_nt(s["ar"], s["bk_s"])
            s["n"] = jnp.where(strict, prod[0:C, 0:GC], 0.0)
            s["a_ak"] = jnp.where(strict, prod[0:C, GC:], 0.0).astype(BF16)
            s["a_rb"] = jnp.where(incl, prod[C:, 0:GC], 0.0).astype(BF16)
            s["a_rk"] = jnp.where(incl, prod[C:, GC:], 0.0).astype(BF16)
        fill()
        for s in st:
            res = _dot(jnp.concatenate([s["a_ak"], s["a_rk"], s["kw_t"]], axis=0), s["v_s"])
            s["av_s"] = stack(res[0:C]).astype(BF16)
            s["rkv"] = res[C:2 * C]
            s["kwv"] = res[2 * C:]
        fill()
        assert INV_BASE == 8
        for s in st:
            nd = jnp.where(same_block[INV_BASE], s["n"], 0.0)
            s["t"] = eye_c + nd
            s["p"] = _dot(nd.astype(BF16), stack(nd).astype(BF16))
        fill()
        for s in st:
            res = _dot(jnp.concatenate([s["t"], s["p"]], axis=0).astype(BF16), stack(s["p"]).astype(BF16))
            s["t"] = s["t"] + res[0:C]
            s["p"] = res[C:]
        for s in st:
            s["t"] = s["t"] + _dot(s["t"].astype(BF16), stack(s["p"]).astype(BF16))
        fill()
        size = INV_BASE
        while size < C:
            for s in st:
                b_off = jnp.where(same_block[2 * size] & ~same_block[size], s["n"], 0.0)
                s["p"] = _dot(b_off.astype(BF16), stack(s["t"]).astype(BF16))
            fill()
            for s in st:
                s["t"] = s["t"] + _dot(s["t"].astype(BF16), stack(s["p"]).astype(BF16))
            fill()
            size *= 2
        for s in st:
            s["t"] = s["t"].astype(BF16)
        for s in st:
            s["x1_s"] = stack(_dot(s["t"], s["a_s"])).astype(BF16)
            s["x2_s"] = stack(_dot(s["t"], s["av_s"])).astype(BF16)
        for s in st:
            lhs = jnp.concatenate([s["a_rb"], s["bw_t"]], axis=0)
            o1 = _dot(lhs, s["x1_s"])
            o2 = _dot(lhs, s["x2_s"])
            s["q"] = (s["r32"] + o1[0:C]).astype(BF16)
            s["m"] = (o1[C:] + jnp.where(eye_w, s["w_end"], 0.0)).astype(BF16)
            s["y0"] = o2[0:C] + s["rkv"]
            s["g"] = o2[C:] + s["kwv"]
        return st

    def state_steps(st, h_cur, fillers):
        y_rows = []
        n_chunks = STG // C
        per_step = -(-len(fillers) // n_chunks)
        for c in range(n_chunks):
            y_lanes = []
            for g in range(n_groups):
                s = st[c * n_groups + g]
                res = _dot(jnp.concatenate([s["q"], s["m"]], axis=0), h_cur[g].astype(BF16))
                y_lanes.append(res[0:C] + s["y0"])
                h_cur[g] = res[C:] + s["g"]
            y_rows.append(jnp.concatenate(y_lanes, axis=1))
            run_all(fillers[c * per_step:(c + 1) * per_step])
        return jnp.concatenate(y_rows, axis=0)

    def finish(rows, y, p):
        inv_hd = 1.0 / HD
        mu = head_sum(y) * inv_hd
        d = y - mu
        var = head_sum(d * d) * inv_hd
        yn = d * lax.rsqrt(var + RWKV_GN_EPS) * vec(_V_LNW) + vec(_V_LNB)
        out_ref[rows, :] = ((yn + p["bonus"]) * p["gate"]).astype(BF16)

    assert n_stages == 2
    h_cur = [h_ref[g] for g in range(n_groups)]
    prep_s0 = {name: prep0_ref.at[i] for i, name in enumerate(_PREP_NAMES)}
    prep_s0["bonus"], prep_s0["gate"] = prep_s0["bonus"][...], prep_s0["gate"][...]
    last_s1 = (proj1_ref[STG - 1:STG, 0:N_RKV], proj1_ref[STG - 1:STG, N_RKV + N_RET:N_PROJ])
    last_step_of_row = pl.program_id(1) == pl.num_programs(1) - 1

    def chains_with_fillers(p, thunks):
        for t in thunks:
            slot_queue.extend([[t], []])
        st = chains_of(p)
        while slot_queue:
            fill()
        return st

    def interleave(a, b):
        return [t for pair in zip(a, b) for t in pair]

    front_n0, ret_n0 = project(xnext_ref.at[stage_rows[0], :], 0)
    chains_s0 = chains_with_fillers(prep_s0, front_n0)

    prev_s1 = lambda: (prev_ref[SUBLANES - 1:SUBLANES, 0:N_RKV],
                       prev_ref[SUBLANES - 1:SUBLANES, N_RKV:N_RKV + N_LORA])
    prep_s1, thunks = prepare(1, prev_s1, store=False)
    y_s0 = state_steps(chains_s0, h_cur, interleave(thunks, ret_n0))

    front_n1, ret_n1 = project(xnext_ref.at[stage_rows[1], :], 1)
    chains_s1 = chains_with_fillers(prep_s1, front_n1)

    finish(stage_rows[0], y_s0, prep_s0)
    prev_n0 = lambda: tuple(jnp.where(last_step_of_row, 0.0, z) for z in last_s1)
    y_s1 = state_steps(chains_s1, h_cur, interleave(prepare(0, prev_n0, store=True)[1], ret_n1))
    finish(stage_rows[1], y_s1, prep_s1)
    while ret_units or ret_pending:
        fill()
    for g in range(n_groups):
        h_ref[g] = h_cur[g]
    for h in range(RET_HEADS):
        rstate_ref[h] = ret_state[h]


def _mixer(x2, g_mix, w_in, w_low, vecs, w2cat, seg, tril, freq, gn_w, bsz, seq):
    tt = MIX_TILE
    nt = seq // tt
    w = RWKV_GROUP_HEADS * RWKV_HEAD_DIM
    n_groups = RWKV_WIDTH // w
    cos, sin, dmask = _ret_tables(freq, seq)
    row = lambda b, t: (b * nt + t, 0)
    next_tile = lambda b, t: (jnp.minimum(b * nt + t + 1, bsz * nt - 1), 0)
    pos = lambda b, t: (t, 0)
    const = lambda b, t: (0, 0)
    return pl.pallas_call(
        _mixer_kernel,
        grid=(bsz, nt),
        in_specs=[
            pl.BlockSpec((tt, D_MODEL), const),
            pl.BlockSpec((tt, D_MODEL), next_tile),
            pl.BlockSpec((1, D_MODEL), const),
            pl.BlockSpec((D_MODEL, N_RKV + N_RET), const, pipeline_mode=pl.Buffered(1)),
            pl.BlockSpec((D_MODEL, N_LORA), const, pipeline_mode=pl.Buffered(1)),
            pl.BlockSpec((_N_VEC_ROWS, RWKV_WIDTH), const),
            pl.BlockSpec((LORA_WIDTH, 3 * RWKV_WIDTH), const),
            pl.BlockSpec((LANES, LANES), const),
            pl.BlockSpec((MIX_STAGE, MIX_STAGE), const),
            pl.BlockSpec((tt, RET_HEAD_DIM), pos),
            pl.BlockSpec((tt, RET_HEAD_DIM), pos),
            pl.BlockSpec((RET_HEADS, RET_CHUNK, RET_CHUNK), lambda b, t: (0, 0, 0)),
            pl.BlockSpec((1, RET_WIDTH), const),
        ],
        out_specs=[pl.BlockSpec((tt, RWKV_WIDTH), row), pl.BlockSpec((tt, RET_WIDTH), row)],
        out_shape=[jax.ShapeDtypeStruct((bsz * seq, RWKV_WIDTH), BF16),
                   jax.ShapeDtypeStruct((bsz * seq, RET_WIDTH), BF16)],
        scratch_shapes=[pltpu.VMEM((n_groups, w, w), F32),
                        pltpu.VMEM((RET_HEADS, RET_HEAD_DIM, RET_HEAD_DIM), F32),
                        pltpu.VMEM((SUBLANES, N_RKV + N_LORA), F32),
                        pltpu.VMEM((len(_PREP_NAMES), MIX_STAGE, RWKV_WIDTH), F32),
                        pltpu.VMEM((MIX_STAGE, N_RET), F32),
                        pltpu.VMEM((MIX_STAGE, N_PROJ), F32),
                        pltpu.VMEM((MIX_STAGE, N_RKV + N_LORA), F32)],
        compiler_params=pltpu.CompilerParams(
            dimension_semantics=("arbitrary", "arbitrary"), vmem_limit_bytes=_vmem_limit(56 * 1024 * 1024)),
        name="mixer",
    )(x2, x2, g_mix, w_in, w_low, vecs, w2cat, seg, tril, cos, sin, dmask, gn_w)


def _cast_rows_to_bf16(jobs, sem):
    copies = [pltpu.make_async_copy(src, stg, sem.at[i % 2]) for i, (src, stg, _) in enumerate(jobs)]
    copies[0].start()
    for i, (_, stg, dst) in enumerate(jobs):
        if i + 1 < len(jobs):
            copies[i + 1].start()
        copies[i].wait()
        dst[...] = stg[...].astype(BF16)


def _ffn_weight_jobs(wo_hbm, wg_hbm, wu_hbm, wd_hbm, wo_ref, wg_ref, wu_ref, wd_ref, wide_ref, tall_ref):
    jobs = []
    for hbm, vm in ((wg_hbm, wg_ref), (wu_hbm, wu_ref)):
        for r in range(0, D_MODEL, FFN_WIDE_ROWS):
            rows = pl.ds(r, FFN_WIDE_ROWS)
            jobs.append((hbm.at[rows, :], wide_ref.at[len(jobs) % 2], vm.at[rows, :]))
    for r in range(0, D_FF, FFN_TALL_ROWS):
        rows = pl.ds(r, FFN_TALL_ROWS)
        jobs.append((wd_hbm.at[rows, :], tall_ref.at[len(jobs) % 2], wd_ref.at[rows, :]))
    half = D_MODEL // 2
    for r in range(0, D_MODEL, half):
        rows = pl.ds(r, half)
        jobs.append((wo_hbm.at[rows, :], tall_ref.at[len(jobs) % 2, pl.ds(0, half), :], wo_ref.at[rows, :]))
    return jobs


def _ffn_kernel(yr_ref, yt_ref, x_ref, wo_hbm, gf_ref, wg_hbm, wu_hbm, cw_ref, cb_ref, wd_hbm, gl_ref,
                out_ref, carry_ref, wo_ref, wg_ref, wu_ref, wd_ref, wide_ref, tall_ref, sem):
    sm = FFN_TILE // FFN_STAGES
    stages = [slice(i * sm, (i + 1) * sm) for i in range(FFN_STAGES)]

    @pl.when((pl.program_id(0) == 0) & (pl.program_id(1) == 0))
    def _():
        _cast_rows_to_bf16(_ffn_weight_jobs(wo_hbm, wg_hbm, wu_hbm, wd_hbm, wo_ref, wg_ref, wu_ref, wd_ref,
                                            wide_ref, tall_ref), sem)

    @pl.when(pl.program_id(1) == 0)
    def _():
        carry_ref[...] = jnp.zeros_like(carry_ref)

    mix = [_dot(yr_ref[s, :], wo_ref[0:RWKV_WIDTH, :]) + _dot(yt_ref[s, :], wo_ref[RWKV_WIDTH:, :])
           for s in stages]
    acc = [x_ref[s, :] + m for s, m in zip(stages, mix)]
    hb = [_rms_norm(x1, gf_ref[...]).astype(BF16) for x1 in acc]

    cols = D_FF // FFN_COL_SPLIT
    for j in range(FFN_COL_SPLIT):
        cs = slice(j * cols, (j + 1) * cols)
        gate_up = [(_dot(h, wg_ref[:, cs]), _dot(h, wu_ref[:, cs])) for h in hb]
        prev = carry_ref[:, cs]
        p2, p1 = prev[SUBLANES - 2:SUBLANES - 1], prev[SUBLANES - 1:SUBLANES]
        hidden = []
        for gate, up in gate_up:
            row = lax.broadcasted_iota(jnp.int32, gate.shape, 0)
            g1 = jnp.where(row == 0, p1, pltpu.roll(gate, 1, 0))
            g2 = jnp.where(row == 0, p2, jnp.where(row == 1, p1, pltpu.roll(gate, 2, 0)))
            p2, p1 = gate[sm - 2:sm - 1], gate[sm - 1:sm]
            conv = cw_ref[0:1, cs] * g2 + cw_ref[1:2, cs] * g1 + cw_ref[2:3, cs] * gate + cb_ref[:, cs]
            hidden.append((conv * jax.nn.sigmoid(conv) * up).astype(BF16))
        carry_ref[:, cs] = gate_up[-1][0][sm - SUBLANES:sm]
        acc = [a + _dot(h, wd_ref[cs, :]) for a, h in zip(acc, hidden)]
    for s, a in zip(stages, acc):
        out_ref[s, :] = _rms_norm(a, gl_ref[...])


def _ffn(y_rwkv, y_ret, x2, w_out, g_ffn, w_gate, w_up, conv_w, conv_b, w_down, g_final, bsz, seq):
    tm = FFN_TILE
    nt = seq // tm
    row = lambda b, t: (b * nt + t, 0)
    const = lambda b, t: (0, 0)
    in_hbm = pl.BlockSpec(memory_space=pl.ANY)
    weights = (D_MODEL * D_MODEL + 3 * D_MODEL * D_FF) * 2
    staging = 2 * (FFN_WIDE_ROWS * D_FF + FFN_TALL_ROWS * D_MODEL) * 4
    vmem = weights + staging + 4 * tm * D_MODEL * 4 + 8 * tm * D_FF * 4 + 8 * 1024 * 1024
    return pl.pallas_call(
        _ffn_kernel,
        grid=(bsz, nt),
        in_specs=[
            pl.BlockSpec((tm, RWKV_WIDTH), row),
            pl.BlockSpec((tm, RET_WIDTH), row),
            pl.BlockSpec((tm, D_MODEL), row),
            in_hbm,
            pl.BlockSpec((1, D_MODEL), const),
            in_hbm,
            in_hbm,
            pl.BlockSpec((SUBLANES, D_FF), const),
            pl.BlockSpec((1, D_FF), const),
            in_hbm,
            pl.BlockSpec((1, D_MODEL), const),
        ],
        out_specs=pl.BlockSpec((tm, D_MODEL), row),
        out_shape=jax.ShapeDtypeStruct((bsz * seq, D_MODEL), F32),
        scratch_shapes=[pltpu.VMEM((SUBLANES, D_FF), F32),
                        pltpu.VMEM((D_MODEL, D_MODEL), BF16),
                        pltpu.VMEM((D_MODEL, D_FF), BF16),
                        pltpu.VMEM((D_MODEL, D_FF), BF16),
                        pltpu.VMEM((D_FF, D_MODEL), BF16),
                        pltpu.VMEM((2, FFN_WIDE_ROWS, D_FF), F32),
                        pltpu.VMEM((2, FFN_TALL_ROWS, D_MODEL), F32),
                        pltpu.SemaphoreType.DMA((2,))],
        compiler_params=pltpu.CompilerParams(
            dimension_semantics=("arbitrary", "arbitrary"), vmem_limit_bytes=_vmem_limit(vmem)),
        name="ffn",
    )(y_rwkv, y_ret, x2, w_out, g_ffn, w_gate, w_up, conv_w, conv_b, w_down, g_final)


def _block_ones(n, block):
    i = np.arange(n)
    return (i[:, None] // block) == (i[None, :] // block)


def kernel(x, norm_mix_g, w_in, rwkv_mu_r, rwkv_mu_k, rwkv_mu_v, rwkv_mu_w, rwkv_mu_a, rwkv_mu_g, rwkv_w0, rwkv_w1, rwkv_w2, rwkv_a0, rwkv_a1, rwkv_a2, rwkv_g1, rwkv_g2, rwkv_k_k, rwkv_k_a, rwkv_r_k, rwkv_lnx_w, rwkv_lnx_b, ret_gn_w, w_out, norm_ffn_g, ffn_w_gate, ffn_w_up, ffn_conv_w, ffn_conv_b, ffn_w_down, norm_final_g):
    bsz, seq, d = x.shape
    assert d == D_MODEL and seq % MIX_TILE == 0 and seq % FFN_TILE == 0
    assert norm_mix_g.shape[0] == 1, "one layer"
    x2 = x.reshape(bsz * seq, d)
    row = lambda p: p.reshape(1, -1)

    w_low = _fold_lora(rwkv_w1[0], rwkv_a1[0], rwkv_g1[0], rwkv_mu_w[0], rwkv_mu_a[0], rwkv_mu_g[0])

    vec_rows = [rwkv_mu_r[0], rwkv_mu_k[0], rwkv_mu_v[0], rwkv_w0[0], rwkv_a0[0], rwkv_k_k[0], rwkv_k_a[0],
                rwkv_r_k[0].reshape(-1), rwkv_lnx_w[0], rwkv_lnx_b[0]]
    vecs = jnp.zeros((_N_VEC_ROWS, RWKV_WIDTH), F32).at[:len(vec_rows)].set(jnp.stack(vec_rows))
    w2cat = jnp.zeros((LORA_WIDTH, 3 * RWKV_WIDTH), BF16)
    w2cat = w2cat.at[0:DECAY_LORA, 0:RWKV_WIDTH].set(rwkv_w2[0].astype(BF16))
    w2cat = w2cat.at[DECAY_LORA:DECAY_LORA + AAA_LORA, RWKV_WIDTH:2 * RWKV_WIDTH].set(rwkv_a2[0].astype(BF16))
    w2cat = w2cat.at[DECAY_LORA + AAA_LORA:, 2 * RWKV_WIDTH:].set(rwkv_g2[0].astype(BF16))
    seg = jnp.asarray(_block_ones(LANES, RWKV_HEAD_DIM), BF16)
    ti = np.arange(MIX_STAGE)
    tril = jnp.asarray(_block_ones(MIX_STAGE, RWKV_CHUNK) & (ti[:, None] >= ti[None, :]), BF16)
    half = RET_HEAD_DIM // 2
    inv_freq = ROPE_BASE ** (-jnp.arange(half, dtype=F32) / half)
    freq = jnp.concatenate([inv_freq, inv_freq]).reshape(1, RET_HEAD_DIM)
    y_rwkv, y_ret = _mixer(x2, row(norm_mix_g[0]), w_in[0].astype(BF16), w_low, vecs, w2cat, seg, tril, freq,
                           row(ret_gn_w[0]),
                           bsz, seq)

    conv_w = jnp.zeros((SUBLANES, D_FF), F32).at[0:3].set(ffn_conv_w[0][:, 0, :])
    out = _ffn(y_rwkv, y_ret, x2, w_out[0], row(norm_ffn_g[0]), ffn_w_gate[0], ffn_w_up[0], conv_w,
               row(ffn_conv_b[0]), ffn_w_down[0], row(norm_final_g), bsz, seq)
    return out.reshape(bsz, seq, d)
```

```python
import math

import jax
import jax.numpy as jnp
import numpy as np
from jax import lax
from jax.experimental import pallas as pl
from jax.experimental.pallas import tpu as pltpu

F32 = jnp.float32
BF16 = jnp.bfloat16

D_MODEL = 1024
RWKV_HEADS = 8
RWKV_HEAD_DIM = 64
RWKV_WIDTH = 512
RET_HEADS = 4
RET_HEAD_DIM = 128
RET_WIDTH = 512
DECAY_LORA = 64
AAA_LORA = 64
GATE_LORA = 128
LORA_WIDTH = DECAY_LORA + AAA_LORA + GATE_LORA
RET_CHUNK = 128
ROPE_BASE = 10000.0
D_FF = 2816
NORM_EPS = 1e-6
RWKV_GN_EPS = 64e-5
RET_GN_EPS = 1e-5

V7X_VMEM_BYTES = 64 * 1024 * 1024
SUBLANES = 8
LANES = 128

RWKV_CHUNK = 64
INV_BASE = 8
RWKV_GROUP_HEADS = 2
MIX_TILE = 512
MIX_STAGE = 256
PROJ_PIECE = 512
FFN_TILE = 512
FFN_STAGES = 2
FFN_COL_SPLIT = 2
FFN_WIDE_ROWS = 256
FFN_TALL_ROWS = 704

N_RKV = 3 * RWKV_WIDTH
N_RET = 4 * RET_WIDTH
N_LORA = 2 * LORA_WIDTH
N_PROJ = N_RKV + N_RET + N_LORA

_LOG_GAMMA = [math.log(1.0 - 2.0 ** (-5.0 - h)) for h in range(RET_HEADS)]
_GAMMA_C = [math.exp(RET_CHUNK * lg) for lg in _LOG_GAMMA]


def _vmem_limit(nbytes):
    return int(min(nbytes, V7X_VMEM_BYTES - 4 * 1024 * 1024))


def _dot(a, b):
    return jnp.dot(a, b, preferred_element_type=F32)


def _dot_nt(a, b):
    return lax.dot_general(a, b, (((1,), (1,)), ((), ())), preferred_element_type=F32)


def _dot_tn(a, b):
    return lax.dot_general(a, b, (((0,), (0,)), ((), ())), preferred_element_type=F32)


def _split_dot(mat_bf16, x):
    hi = x.astype(BF16)
    lo = (x - hi.astype(F32)).astype(BF16)
    return _dot(mat_bf16, hi) + _dot(mat_bf16, lo)


def _shift_rows(x, prev_row):
    rolled = pltpu.roll(x, 1, 0)
    row = lax.broadcasted_iota(jnp.int32, x.shape, 0)
    return jnp.where(row == 0, prev_row, rolled)


def _rms_norm(x, g):
    ms = jnp.mean(x * x, axis=-1, keepdims=True)
    return x * lax.rsqrt(ms + NORM_EPS) * g


def _fold_lora_kernel(w1_ref, a1_ref, g1_ref, muw_ref, mua_ref, mug_ref, o_ref):
    c0 = 0
    for w_ref, mu_ref in ((w1_ref, muw_ref), (a1_ref, mua_ref), (g1_ref, mug_ref)):
        w, mu = w_ref[...], mu_ref[...]
        n = w.shape[1]
        o_ref[:, c0:c0 + n] = (w * (1.0 - mu)).astype(BF16)
        o_ref[:, LORA_WIDTH + c0:LORA_WIDTH + c0 + n] = (w * mu).astype(BF16)
        c0 += n


def _fold_lora(w1, a1, g1, mu_w, mu_a, mu_g):
    col = lambda v: v.reshape(-1, 1)
    return pl.pallas_call(
        _fold_lora_kernel,
        out_shape=jax.ShapeDtypeStruct((D_MODEL, 2 * LORA_WIDTH), BF16),
        name="fold_lora",
    )(w1, a1, g1, col(mu_w), col(mu_a), col(mu_g))


def _ret_tables_kernel(freq_ref, cos_ref, sin_ref, dmask_ref):
    C, HD = RET_CHUNK, RET_HEAD_DIM
    n = cos_ref.shape[0]
    ang = lax.broadcasted_iota(jnp.int32, (n, HD), 0).astype(F32) * freq_ref[...]
    lane = lax.broadcasted_iota(jnp.int32, (n, HD), 1)
    sin = jnp.sin(ang)
    cos_ref[...] = jnp.cos(ang)
    sin_ref[...] = jnp.where(lane < HD // 2, -sin, sin)
    ii = lax.broadcasted_iota(jnp.int32, (C, C), 0)
    jj = lax.broadcasted_iota(jnp.int32, (C, C), 1)
    diff = jnp.maximum((ii - jj).astype(F32), 0.0)
    for h in range(RET_HEADS):
        dmask_ref[h] = jnp.where(ii >= jj, jnp.exp(diff * _LOG_GAMMA[h]), 0.0)


def _ret_tables(freq, seq):
    return pl.pallas_call(
        _ret_tables_kernel,
        out_shape=[jax.ShapeDtypeStruct((seq, RET_HEAD_DIM), F32),
                   jax.ShapeDtypeStruct((seq, RET_HEAD_DIM), F32),
                   jax.ShapeDtypeStruct((RET_HEADS, RET_CHUNK, RET_CHUNK), F32)],
        name="ret_tables",
    )(freq)


(_V_MU_R, _V_MU_K, _V_MU_V, _V_W0, _V_A0, _V_KK, _V_KA, _V_RK, _V_LNW, _V_LNB) = range(10)
_N_VEC_ROWS = 16

_PREP_NAMES = ("a_t", "r_t", "b_t", "k_t", "b_w", "k_w", "w_c", "v", "bonus", "gate")

_PROJ_PIECES = ([("rkv", c) for c in range(0, N_RKV, PROJ_PIECE)]
                + [("ret", c) for c in range(N_RKV, N_RKV + N_RET, PROJ_PIECE)]
                + [("lora", c) for c in range(N_RKV + N_RET, N_PROJ, PROJ_PIECE)])


def _mixer_kernel(xfirst_ref, xnext_ref, gmix_ref, win_hbm, wlo_ref, vec_ref, w2_ref, seg_ref, tril_ref, cos_ref,
                  sin_ref, dmask_ref, gnw_ref, out_ref, yret_ref,
                  h_ref, rstate_ref, prev_ref, prep0_ref, pret0_ref, proj1_ref, prj0_ref,
                  win_ref, wstage_ref, wsem):
    C, G, TT, STG = RWKV_CHUNK, RWKV_GROUP_HEADS, MIX_TILE, MIX_STAGE
    HD = RWKV_HEAD_DIM
    W = G * HD
    GC = G * C
    n_groups = RWKV_WIDTH // W
    n_stages = TT // STG
    stage_rows = [slice(i * STG, (i + 1) * STG) for i in range(n_stages)]

    @pl.when(pl.program_id(1) == 0)
    def _():
        h_ref[...] = jnp.zeros_like(h_ref)
        rstate_ref[...] = jnp.zeros_like(rstate_ref)

    vec = lambda i: vec_ref[i:i + 1, :]

    seg = seg_ref[...]

    def head_sum(x):
        n_blk = x.shape[1] // LANES
        n_rows = x.shape[0]
        rows = jnp.concatenate([x[:, j * LANES:(j + 1) * LANES] for j in range(n_blk)], axis=0)
        s = _dot(rows.astype(BF16), seg)
        return jnp.concatenate([s[j * n_rows:(j + 1) * n_rows] for j in range(n_blk)], axis=1)

    def project(x_rows_ref, stage):
        got = {}

        def piece(i):
            def run():
                if "xb" not in got:
                    got["xb"] = _rms_norm(x_rows_ref[...], gmix_ref[...]).astype(BF16)
                name, c0 = _PROJ_PIECES[i]
                n_in = N_RKV + N_RET
                w = win_ref[:, c0:c0 + PROJ_PIECE] if c0 < n_in else wlo_ref[:, c0 - n_in:c0 - n_in + PROJ_PIECE]
                res = _dot(got["xb"], w)
                if stage == 1:
                    proj1_ref[:, c0:c0 + PROJ_PIECE] = res
                elif name == "ret":
                    pret0_ref[:, c0 - N_RKV:c0 - N_RKV + PROJ_PIECE] = res
                else:
                    d0 = c0 if name == "rkv" else c0 - N_RET
                    prj0_ref[:, d0:d0 + PROJ_PIECE] = res
            return run
        front = [piece(i) for i, (name, _) in enumerate(_PROJ_PIECES) if name != "ret"]
        ret = [piece(i) for i, (name, _) in enumerate(_PROJ_PIECES) if name == "ret"]
        return front, ret

    def prepare(stage, prev_rows, store):
        pp = {}

        def put(name, val):
            if store:
                prep0_ref[_PREP_NAMES.index(name)] = val
            else:
                pp[name] = val

        def prep_a():
            if stage == 1:
                p_rkv, p_lora = proj1_ref[:, 0:N_RKV], proj1_ref[:, N_RKV + N_RET:N_PROJ]
            else:
                p_rkv, p_lora = prj0_ref[:, 0:N_RKV], prj0_ref[:, N_RKV:N_RKV + N_LORA]
            prev_rkv, prev_lora = prev_rows()
            s_rkv = _shift_rows(p_rkv, prev_rkv)
            lerp = lambda j, mu: (p_rkv[:, j * RWKV_WIDTH:(j + 1) * RWKV_WIDTH]
                                  + (s_rkv[:, j * RWKV_WIDTH:(j + 1) * RWKV_WIDTH]
                                     - p_rkv[:, j * RWKV_WIDTH:(j + 1) * RWKV_WIDTH]) * mu)
            if store:
                prev_ref[:, 0:N_RKV] = p_rkv[STG - SUBLANES:]
                prev_ref[:, N_RKV:N_RKV + N_LORA] = p_lora[STG - SUBLANES:]
            pp["r"] = lerp(0, vec(_V_MU_R))
            pp["k"] = lerp(1, vec(_V_MU_K))
            pp["v"] = lerp(2, vec(_V_MU_V))
            put("v", pp["v"])
            low = p_lora[:, 0:LORA_WIDTH] + _shift_rows(p_lora, prev_lora)[:, LORA_WIDTH:2 * LORA_WIDTH]
            lane = lax.broadcasted_iota(jnp.int32, low.shape, 1)
            act = jnp.where(lane < DECAY_LORA, jnp.tanh(low),
                            jnp.where(lane < DECAY_LORA + AAA_LORA, low, jax.nn.sigmoid(low)))
            pp["second"] = _dot(act.astype(BF16), w2_ref[...])

        def prep_b():
            second = pp["second"]
            pp["ld"] = -math.exp(-0.5) * jax.nn.sigmoid(second[:, 0:RWKV_WIDTH] + vec(_V_W0))
            pp["a"] = jax.nn.sigmoid(second[:, RWKV_WIDTH:2 * RWKV_WIDTH] + vec(_V_A0))
            put("gate", second[:, 2 * RWKV_WIDTH:3 * RWKV_WIDTH])
            pp["kk"] = pp["k"] * vec(_V_KK)
            pp["kk_ss"] = head_sum(pp["kk"] * pp["kk"])
            pp["cum"] = _split_dot(tril_ref[...], pp["ld"])

        def prep_c():
            pp["kk"] = pp["kk"] * lax.rsqrt(jnp.maximum(pp["kk_ss"], 1e-24))
            pp["k2"] = pp["k"] * (1.0 + (pp["a"] - 1.0) * vec(_V_KA))
            put("bonus", head_sum(pp["r"] * pp["k2"] * vec(_V_RK)) * pp["v"])

        def prep_d():
            cum, ld, kk, k2 = pp["cum"], pp["ld"], pp["kk"], pp["k2"]
            b = kk * pp["a"]
            cum_end = jnp.concatenate(
                [jnp.broadcast_to(cum[(c + 1) * C - 1:(c + 1) * C, :], (C, RWKV_WIDTH))
                 for c in range(STG // C)], axis=0)
            e_neg = jnp.exp(-cum)
            e_end = jnp.exp(cum_end - cum)
            put("a_t", -kk * jnp.exp(cum - ld))
            put("r_t", pp["r"] * jnp.exp(cum))
            put("b_t", b * e_neg)
            put("k_t", k2 * e_neg)
            put("b_w", b * e_end)
            put("k_w", k2 * e_end)
            put("w_c", jnp.exp(cum_end))

        return pp, [prep_a, prep_b, prep_c, prep_d]

    def run_all(thunks):
        for thunk in thunks:
            thunk()

    zero_prev = lambda: (jnp.zeros((1, N_RKV), F32), jnp.zeros((1, N_LORA), F32))

    @pl.when((pl.program_id(0) == 0) & (pl.program_id(1) == 0))
    def _():
        pieces = []
        for s in range(n_stages):
            front, ret = project(xfirst_ref.at[stage_rows[s], :], s)
            n_rkv = N_RKV // PROJ_PIECE
            pieces.append(front[:n_rkv] + ret + front[n_rkv:])
        copies = [pltpu.make_async_copy(win_hbm.at[:, pl.ds(c0, PROJ_PIECE)], wstage_ref.at[i % 2], wsem.at[i % 2])
                  for i, c0 in enumerate(range(0, N_RKV + N_RET, PROJ_PIECE))]
        copies[0].start()
        for i in range(len(_PROJ_PIECES)):
            if i < len(copies):
                if i + 1 < len(copies):
                    copies[i + 1].start()
                copies[i].wait()
                win_ref[:, i * PROJ_PIECE:(i + 1) * PROJ_PIECE] = wstage_ref[i % 2].astype(BF16)
            for s in range(n_stages):
                pieces[s][i]()
        run_all(prepare(0, zero_prev, store=True)[1])

    ret_state = [rstate_ref[h] for h in range(RET_HEADS)]

    def ret_unit(c, h):
        RC, RD = RET_CHUNK, RET_HEAD_DIM
        s, c_loc = divmod(c * RC, STG)
        rows = slice(c * RC, (c + 1) * RC)
        loc = slice(c_loc, c_loc + RC)
        row_i = lax.broadcasted_iota(jnp.int32, (RC, 1), 0).astype(F32)
        lg = _LOG_GAMMA[h]
        cos, sin = cos_ref[rows, :], sin_ref[rows, :]
        rot = lambda xh: xh * cos + pltpu.roll(xh, RD // 2, 1) * sin

        def part(j):
            c0 = j * RET_WIDTH + h * RD
            if s == 0:
                return pret0_ref[loc, c0:c0 + RD]
            return proj1_ref[loc, N_RKV + c0:N_RKV + c0 + RD]

        q = rot(part(0))
        kr = rot(part(1)) * (RD ** -0.5)
        vr = part(2)
        gt = part(3)
        qb, kb, vb = q.astype(BF16), kr.astype(BF16), vr.astype(BF16)
        scores = (_dot_nt(qb, kb) * dmask_ref[h]).astype(BF16)
        kv = _dot_tn(kb, (vr * jnp.exp((RC - 1.0 - row_i) * lg)).astype(BF16))
        inter = _dot((q * jnp.exp((row_i + 1.0) * lg)).astype(BF16), ret_state[h].astype(BF16))
        ret_state[h] = ret_state[h] * _GAMMA_C[h] + kv

        def second_half():
            y = _dot(scores, vb) + inter
            mu = jnp.mean(y, axis=-1, keepdims=True)
            d = y - mu
            var = jnp.mean(d * d, axis=-1, keepdims=True)
            yn = d * lax.rsqrt(var + RET_GN_EPS) * gnw_ref[:, h * RD:(h + 1) * RD]
            yret_ref[rows, h * RD:(h + 1) * RD] = (gt * jax.nn.sigmoid(gt) * yn).astype(BF16)
        return second_half

    ret_units = [(c, h) for c in range(TT // RET_CHUNK) for h in range(RET_HEADS)]
    ret_pending = []
    slot_queue = []

    def fill():
        if ret_pending:
            ret_pending.pop(0)()
        if ret_units:
            ret_pending.append(ret_unit(*ret_units.pop(0)))
        if slot_queue:
            for thunk in slot_queue.pop(0):
                thunk()

    c_bits = C.bit_length() - 1
    hd_bits = HD.bit_length() - 1
    assert C == 1 << c_bits and HD == 1 << hd_bits and W == GC
    sr = lax.broadcasted_iota(jnp.int32, (GC, W), 0)
    sl_ = lax.broadcasted_iota(jnp.int32, (GC, W), 1)
    stack_mask = (sr >> c_bits) == (sl_ >> hd_bits)
    ti = lax.broadcasted_iota(jnp.int32, (C, GC), 0)
    si = lax.broadcasted_iota(jnp.int32, (C, GC), 1) & (C - 1)
    strict = ti > si
    incl = ti >= si
    eye_c = (ti == si).astype(F32)
    same_block = {}
    size = INV_BASE
    while size <= C:
        bits = size.bit_length() - 1
        same_block[size] = (ti >> bits) == (si >> bits)
        size *= 2
    wi = lax.broadcasted_iota(jnp.int32, (W, W), 0)
    wj = lax.broadcasted_iota(jnp.int32, (W, W), 1)
    eye_w = wi == wj

    def stack(x):
        return jnp.where(stack_mask, jnp.concatenate([x] * G, axis=0), 0.0)

    blk = lambda z, c, g: z[c * C:(c + 1) * C, g * W:(g + 1) * W]

    def chains_of(p):
        st = []
        for c in range(STG // C):
            for g in range(n_groups):
                a_c, r_c = blk(p["a_t"], c, g), blk(p["r_t"], c, g)
                st.append(dict(
                    ar=jnp.concatenate([a_c, r_c], axis=0).astype(BF16), r32=r_c,
                    a_s=stack(a_c).astype(BF16),
                    bk_s=jnp.concatenate([stack(blk(p["b_t"], c, g)), stack(blk(p["k_t"], c, g))],
                                         axis=0).astype(BF16),
                    bw_t=stack(blk(p["b_w"], c, g)).T.astype(BF16),
                    kw_t=stack(blk(p["k_w"], c, g)).T.astype(BF16),
                    v_s=stack(blk(p["v"], c, g)).astype(BF16), w_end=blk(p["w_c"], c, g)[0:1, :]))
        for s in st:
            prod = _dot_nt(s["ar"], s["bk_s"])
            s["n"] = jnp.where(strict, prod[0:C, 0:GC], 0.0)
            s["a_ak"] = jnp.where(strict, prod[0:C, GC:], 0.0).astype(BF16)
            s["a_rb"] = jnp.where(incl, prod[C:, 0:GC], 0.0).astype(BF16)
            s["a_rk"] = jnp.where(incl, prod[C:, GC:], 0.0).astype(BF16)
        fill()
        for s in st:
            res = _dot(jnp.concatenate([s["a_ak"], s["a_rk"], s["kw_t"]], axis=0), s["v_s"])
            s["av_s"] = stack(res[0:C]).astype(BF16)
            s["rkv"] = res[C:2 * C]
            s["kwv"] = res[2 * C:]
        fill()
        assert INV_BASE == 8 and C == 64
        for s in st:
            nd = jnp.where(same_block[8], s["n"], 0.0)
            s["t"] = eye_c + nd
            s["p"] = _dot(nd.astype(BF16), stack(nd).astype(BF16))
        fill()
        for s in st:
            res = _dot(jnp.concatenate([s["t"], s["p"]], axis=0).astype(BF16), stack(s["p"]).astype(BF16))
            s["t"] = s["t"] + res[0:C]
            s["p"] = res[C:]
        for s in st:
            s["t"] = s["t"] + _dot(s["t"].astype(BF16), stack(s["p"]).astype(BF16))
        fill()
        for s in st:
            b_off = [jnp.where(same_block[2 * size] & ~same_block[size], s["n"], 0.0) for size in (8, 16, 32)]
            res = _dot(jnp.concatenate(b_off, axis=0).astype(BF16), stack(s["t"]).astype(BF16))
            s["x"], s["u16"], s["u32"] = res[0:C], res[C:2 * C], res[2 * C:]
        fill()
        for s in st:
            res = _dot(jnp.concatenate([s["t"], s["u16"], s["u32"]], axis=0).astype(BF16),
                       stack(s["x"]).astype(BF16))
            s["t"], s["x"], s["u32"] = s["t"] + res[0:C], s["u16"] + res[C:2 * C], s["u32"] + res[2 * C:]
        fill()
        for s in st:
            res = _dot(jnp.concatenate([s["t"], s["u32"]], axis=0).astype(BF16), stack(s["x"]).astype(BF16))
            s["t"], s["x"] = s["t"] + res[0:C], s["u32"] + res[C:]
        fill()
        for s in st:
            s["t"] = (s["t"] + _dot(s["t"].astype(BF16), stack(s["x"]).astype(BF16))).astype(BF16)
        fill()
        for s in st:
            s["x1_s"] = stack(_dot(s["t"], s["a_s"])).astype(BF16)
            s["x2_s"] = stack(_dot(s["t"], s["av_s"])).astype(BF16)
        for s in st:
            lhs = jnp.concatenate([s["a_rb"], s["bw_t"]], axis=0)
            o1 = _dot(lhs, s["x1_s"])
            o2 = _dot(lhs, s["x2_s"])
            s["q"] = (s["r32"] + o1[0:C]).astype(BF16)
            s["m"] = (o1[C:] + jnp.where(eye_w, s["w_end"], 0.0)).astype(BF16)
            s["y0"] = o2[0:C] + s["rkv"]
            s["g"] = o2[C:] + s["kwv"]
        return st

    def state_steps(st, h_cur, fillers):
        y_rows = []
        n_chunks = STG // C
        per_step = -(-len(fillers) // n_chunks)
        for c in range(n_chunks):
            y_lanes = []
            for g in range(n_groups):
                s = st[c * n_groups + g]
                res = _dot(jnp.concatenate([s["q"], s["m"]], axis=0), h_cur[g].astype(BF16))
                y_lanes.append(res[0:C] + s["y0"])
                h_cur[g] = res[C:] + s["g"]
            y_rows.append(jnp.concatenate(y_lanes, axis=1))
            run_all(fillers[c * per_step:(c + 1) * per_step])
        return jnp.concatenate(y_rows, axis=0)

    def finish(rows, y, p):
        inv_hd = 1.0 / HD
        mu = head_sum(y) * inv_hd
        d = y - mu
        var = head_sum(d * d) * inv_hd
        yn = d * lax.rsqrt(var + RWKV_GN_EPS) * vec(_V_LNW) + vec(_V_LNB)
        out_ref[rows, :] = ((yn + p["bonus"]) * p["gate"]).astype(BF16)

    assert n_stages == 2
    h_cur = [h_ref[g] for g in range(n_groups)]
    prep_s0 = {name: prep0_ref.at[i] for i, name in enumerate(_PREP_NAMES)}
    prep_s0["bonus"], prep_s0["gate"] = prep_s0["bonus"][...], prep_s0["gate"][...]
    last_s1 = (proj1_ref[STG - 1:STG, 0:N_RKV], proj1_ref[STG - 1:STG, N_RKV + N_RET:N_PROJ])
    last_step_of_row = pl.program_id(1) == pl.num_programs(1) - 1

    def chains_with_fillers(p, thunks):
        for t in thunks:
            slot_queue.extend([[t], []])
        st = chains_of(p)
        while slot_queue:
            fill()
        return st

    def interleave(a, b):
        return [t for pair in zip(a, b) for t in pair]

    front_n0, ret_n0 = project(xnext_ref.at[stage_rows[0], :], 0)
    chains_s0 = chains_with_fillers(prep_s0, front_n0)

    prev_s1 = lambda: (prev_ref[SUBLANES - 1:SUBLANES, 0:N_RKV],
                       prev_ref[SUBLANES - 1:SUBLANES, N_RKV:N_RKV + N_LORA])
    prep_s1, thunks = prepare(1, prev_s1, store=False)
    y_s0 = state_steps(chains_s0, h_cur, interleave(thunks, ret_n0))

    front_n1, ret_n1 = project(xnext_ref.at[stage_rows[1], :], 1)
    chains_s1 = chains_with_fillers(prep_s1, front_n1)

    finish(stage_rows[0], y_s0, prep_s0)
    prev_n0 = lambda: tuple(jnp.where(last_step_of_row, 0.0, z) for z in last_s1)
    y_s1 = state_steps(chains_s1, h_cur, interleave(prepare(0, prev_n0, store=True)[1], ret_n1))
    finish(stage_rows[1], y_s1, prep_s1)
    while ret_units or ret_pending:
        fill()
    for g in range(n_groups):
        h_ref[g] = h_cur[g]
    for h in range(RET_HEADS):
        rstate_ref[h] = ret_state[h]


def _mixer(x2, g_mix, w_in, w_low, vecs, w2cat, seg, tril, freq, gn_w, bsz, seq):
    tt = MIX_TILE
    nt = seq // tt
    w = RWKV_GROUP_HEADS * RWKV_HEAD_DIM
    n_groups = RWKV_WIDTH // w
    cos, sin, dmask = _ret_tables(freq, seq)
    row = lambda b, t: (b * nt + t, 0)
    next_tile = lambda b, t: (jnp.minimum(b * nt + t + 1, bsz * nt - 1), 0)
    pos = lambda b, t: (t, 0)
    const = lambda b, t: (0, 0)
    return pl.pallas_call(
        _mixer_kernel,
        grid=(bsz, nt),
        in_specs=[
            pl.BlockSpec((tt, D_MODEL), const),
            pl.BlockSpec((tt, D_MODEL), next_tile),
            pl.BlockSpec((1, D_MODEL), const),
            pl.BlockSpec(memory_space=pl.ANY),
            pl.BlockSpec((D_MODEL, N_LORA), const, pipeline_mode=pl.Buffered(1)),
            pl.BlockSpec((_N_VEC_ROWS, RWKV_WIDTH), const),
            pl.BlockSpec((LORA_WIDTH, 3 * RWKV_WIDTH), const),
            pl.BlockSpec((LANES, LANES), const),
            pl.BlockSpec((MIX_STAGE, MIX_STAGE), const),
            pl.BlockSpec((tt, RET_HEAD_DIM), pos),
            pl.BlockSpec((tt, RET_HEAD_DIM), pos),
            pl.BlockSpec((RET_HEADS, RET_CHUNK, RET_CHUNK), lambda b, t: (0, 0, 0)),
            pl.BlockSpec((1, RET_WIDTH), const),
        ],
        out_specs=[pl.BlockSpec((tt, RWKV_WIDTH), row), pl.BlockSpec((tt, RET_WIDTH), row)],
        out_shape=[jax.ShapeDtypeStruct((bsz * seq, RWKV_WIDTH), BF16),
                   jax.ShapeDtypeStruct((bsz * seq, RET_WIDTH), BF16)],
        scratch_shapes=[pltpu.VMEM((n_groups, w, w), F32),
                        pltpu.VMEM((RET_HEADS, RET_HEAD_DIM, RET_HEAD_DIM), F32),
                        pltpu.VMEM((SUBLANES, N_RKV + N_LORA), F32),
                        pltpu.VMEM((len(_PREP_NAMES), MIX_STAGE, RWKV_WIDTH), F32),
                        pltpu.VMEM((MIX_STAGE, N_RET), F32),
                        pltpu.VMEM((MIX_STAGE, N_PROJ), F32),
                        pltpu.VMEM((MIX_STAGE, N_RKV + N_LORA), F32),
                        pltpu.VMEM((D_MODEL, N_RKV + N_RET), BF16),
                        pltpu.VMEM((2, D_MODEL, PROJ_PIECE), F32),
                        pltpu.SemaphoreType.DMA((2,))],
        compiler_params=pltpu.CompilerParams(
            dimension_semantics=("arbitrary", "arbitrary"), vmem_limit_bytes=_vmem_limit(56 * 1024 * 1024)),
        name="mixer",
    )(x2, x2, g_mix, w_in, w_low, vecs, w2cat, seg, tril, cos, sin, dmask, gn_w)


def _cast_rows_to_bf16(jobs, sem):
    copies = [pltpu.make_async_copy(src, stg, sem.at[i % 2]) for i, (src, stg, _) in enumerate(jobs)]
    copies[0].start()
    for i, (_, stg, dst) in enumerate(jobs):
        if i + 1 < len(jobs):
            copies[i + 1].start()
        copies[i].wait()
        dst[...] = stg[...].astype(BF16)


def _ffn_weight_jobs(wo_hbm, wg_hbm, wu_hbm, wd_hbm, wo_ref, wg_ref, wu_ref, wd_ref, wide_ref, tall_ref):
    jobs = []
    for hbm, vm in ((wg_hbm, wg_ref), (wu_hbm, wu_ref)):
        for r in range(0, D_MODEL, FFN_WIDE_ROWS):
            rows = pl.ds(r, FFN_WIDE_ROWS)
            jobs.append((hbm.at[rows, :], wide_ref.at[len(jobs) % 2], vm.at[rows, :]))
    for r in range(0, D_FF, FFN_TALL_ROWS):
        rows = pl.ds(r, FFN_TALL_ROWS)
        jobs.append((wd_hbm.at[rows, :], tall_ref.at[len(jobs) % 2], wd_ref.at[rows, :]))
    half = D_MODEL // 2
    for r in range(0, D_MODEL, half):
        rows = pl.ds(r, half)
        jobs.append((wo_hbm.at[rows, :], tall_ref.at[len(jobs) % 2, pl.ds(0, half), :], wo_ref.at[rows, :]))
    return jobs


def _ffn_kernel(yr_ref, yt_ref, x_ref, wo_hbm, gf_ref, wg_hbm, wu_hbm, cw_ref, cb_ref, wd_hbm, gl_ref,
                out_ref, carry_ref, wo_ref, wg_ref, wu_ref, wd_ref, wide_ref, tall_ref, sem):
    sm = FFN_TILE // FFN_STAGES
    stages = [slice(i * sm, (i + 1) * sm) for i in range(FFN_STAGES)]

    @pl.when((pl.program_id(0) == 0) & (pl.program_id(1) == 0))
    def _():
        _cast_rows_to_bf16(_ffn_weight_jobs(wo_hbm, wg_hbm, wu_hbm, wd_hbm, wo_ref, wg_ref, wu_ref, wd_ref,
                                            wide_ref, tall_ref), sem)

    @pl.when(pl.program_id(1) == 0)
    def _():
        carry_ref[...] = jnp.zeros_like(carry_ref)

    mix = [_dot(yr_ref[s, :], wo_ref[0:RWKV_WIDTH, :]) + _dot(yt_ref[s, :], wo_ref[RWKV_WIDTH:, :])
           for s in stages]
    acc = [x_ref[s, :] + m for s, m in zip(stages, mix)]
    hb = [_rms_norm(x1, gf_ref[...]).astype(BF16) for x1 in acc]

    cols = D_FF // FFN_COL_SPLIT
    for j in range(FFN_COL_SPLIT):
        cs = slice(j * cols, (j + 1) * cols)
        gate_up = [(_dot(h, wg_ref[:, cs]), _dot(h, wu_ref[:, cs])) for h in hb]
        prev = carry_ref[:, cs]
        p2, p1 = prev[SUBLANES - 2:SUBLANES - 1], prev[SUBLANES - 1:SUBLANES]
        hidden = []
        for gate, up in gate_up:
            row = lax.broadcasted_iota(jnp.int32, gate.shape, 0)
            g1 = jnp.where(row == 0, p1, pltpu.roll(gate, 1, 0))
            g2 = jnp.where(row == 0, p2, jnp.where(row == 1, p1, pltpu.roll(gate, 2, 0)))
            p2, p1 = gate[sm - 2:sm - 1], gate[sm - 1:sm]
            conv = cw_ref[0:1, cs] * g2 + cw_ref[1:2, cs] * g1 + cw_ref[2:3, cs] * gate + cb_ref[:, cs]
            hidden.append((conv * jax.nn.sigmoid(conv) * up).astype(BF16))
        carry_ref[:, cs] = gate_up[-1][0][sm - SUBLANES:sm]
        acc = [a + _dot(h, wd_ref[cs, :]) for a, h in zip(acc, hidden)]
    for s, a in zip(stages, acc):
        out_ref[s, :] = _rms_norm(a, gl_ref[...])


def _ffn(y_rwkv, y_ret, x2, w_out, g_ffn, w_gate, w_up, conv_w, conv_b, w_down, g_final, bsz, seq):
    tm = FFN_TILE
    nt = seq // tm
    row = lambda b, t: (b * nt + t, 0)
    const = lambda b, t: (0, 0)
    in_hbm = pl.BlockSpec(memory_space=pl.ANY)
    weights = (D_MODEL * D_MODEL + 3 * D_MODEL * D_FF) * 2
    staging = 2 * (FFN_WIDE_ROWS * D_FF + FFN_TALL_ROWS * D_MODEL) * 4
    vmem = weights + staging + 4 * tm * D_MODEL * 4 + 8 * tm * D_FF * 4 + 8 * 1024 * 1024
    return pl.pallas_call(
        _ffn_kernel,
        grid=(bsz, nt),
        in_specs=[
            pl.BlockSpec((tm, RWKV_WIDTH), row),
            pl.BlockSpec((tm, RET_WIDTH), row),
            pl.BlockSpec((tm, D_MODEL), row),
            in_hbm,
            pl.BlockSpec((1, D_MODEL), const),
            in_hbm,
            in_hbm,
            pl.BlockSpec((SUBLANES, D_FF), const),
            pl.BlockSpec((1, D_FF), const),
            in_hbm,
            pl.BlockSpec((1, D_MODEL), const),
        ],
        out_specs=pl.BlockSpec((tm, D_MODEL), row),
        out_shape=jax.ShapeDtypeStruct((bsz * seq, D_MODEL), F32),
        scratch_shapes=[pltpu.VMEM((SUBLANES, D_FF), F32),
                        pltpu.VMEM((D_MODEL, D_MODEL), BF16),
                        pltpu.VMEM((D_MODEL, D_FF), BF16),
                        pltpu.VMEM((D_MODEL, D_FF), BF16),
                        pltpu.VMEM((D_FF, D_MODEL), BF16),
                        pltpu.VMEM((2, FFN_WIDE_ROWS, D_FF), F32),
                        pltpu.VMEM((2, FFN_TALL_ROWS, D_MODEL), F32),
                        pltpu.SemaphoreType.DMA((2,))],
        compiler_params=pltpu.CompilerParams(
            dimension_semantics=("arbitrary", "arbitrary"), vmem_limit_bytes=_vmem_limit(vmem)),
        name="ffn",
    )(y_rwkv, y_ret, x2, w_out, g_ffn, w_gate, w_up, conv_w, conv_b, w_down, g_final)


def _block_ones(n, block):
    i = np.arange(n)
    return (i[:, None] // block) == (i[None, :] // block)


def kernel(x, norm_mix_g, w_in, rwkv_mu_r, rwkv_mu_k, rwkv_mu_v, rwkv_mu_w, rwkv_mu_a, rwkv_mu_g, rwkv_w0, rwkv_w1, rwkv_w2, rwkv_a0, rwkv_a1, rwkv_a2, rwkv_g1, rwkv_g2, rwkv_k_k, rwkv_k_a, rwkv_r_k, rwkv_lnx_w, rwkv_lnx_b, ret_gn_w, w_out, norm_ffn_g, ffn_w_gate, ffn_w_up, ffn_conv_w, ffn_conv_b, ffn_w_down, norm_final_g):
    bsz, seq, d = x.shape
    assert d == D_MODEL and seq % MIX_TILE == 0 and seq % FFN_TILE == 0
    assert norm_mix_g.shape[0] == 1, "one layer"
    x2 = x.reshape(bsz * seq, d)
    row = lambda p: p.reshape(1, -1)

    w_low = _fold_lora(rwkv_w1[0], rwkv_a1[0], rwkv_g1[0], rwkv_mu_w[0], rwkv_mu_a[0], rwkv_mu_g[0])

    vec_rows = [rwkv_mu_r[0], rwkv_mu_k[0], rwkv_mu_v[0], rwkv_w0[0], rwkv_a0[0], rwkv_k_k[0], rwkv_k_a[0],
                rwkv_r_k[0].reshape(-1), rwkv_lnx_w[0], rwkv_lnx_b[0]]
    vecs = jnp.zeros((_N_VEC_ROWS, RWKV_WIDTH), F32).at[:len(vec_rows)].set(jnp.stack(vec_rows))
    w2cat = jnp.zeros((LORA_WIDTH, 3 * RWKV_WIDTH), BF16)
    w2cat = w2cat.at[0:DECAY_LORA, 0:RWKV_WIDTH].set(rwkv_w2[0].astype(BF16))
    w2cat = w2cat.at[DECAY_LORA:DECAY_LORA + AAA_LORA, RWKV_WIDTH:2 * RWKV_WIDTH].set(rwkv_a2[0].astype(BF16))
    w2cat = w2cat.at[DECAY_LORA + AAA_LORA:, 2 * RWKV_WIDTH:].set(rwkv_g2[0].astype(BF16))
    seg = jnp.asarray(_block_ones(LANES, RWKV_HEAD_DIM), BF16)
    ti = np.arange(MIX_STAGE)
    tril = jnp.asarray(_block_ones(MIX_STAGE, RWKV_CHUNK) & (ti[:, None] >= ti[None, :]), BF16)
    half = RET_HEAD_DIM // 2
    inv_freq = ROPE_BASE ** (-jnp.arange(half, dtype=F32) / half)
    freq = jnp.concatenate([inv_freq, inv_freq]).reshape(1, RET_HEAD_DIM)
    y_rwkv, y_ret = _mixer(x2, row(norm_mix_g[0]), w_in[0], w_low, vecs, w2cat, seg, tril, freq,
                           row(ret_gn_w[0]),
                           bsz, seq)

    conv_w = jnp.zeros((SUBLANES, D_FF), F32).at[0:3].set(ffn_conv_w[0][:, 0, :])
    out = _ffn(y_rwkv, y_ret, x2, w_out[0], row(norm_ffn_g[0]), ffn_w_gate[0], ffn_w_up[0], conv_w,
               row(ffn_conv_b[0]), ffn_w_down[0], row(norm_final_g), bsz, seq)
    return out.reshape(bsz, seq, d)
```

```python
import math

import jax
import jax.numpy as jnp
import numpy as np
from jax import lax
from jax.experimental import pallas as pl
from jax.experimental.pallas import tpu as pltpu

F32 = jnp.float32
BF16 = jnp.bfloat16

D_MODEL = 1024
RWKV_HEADS = 8
RWKV_HEAD_DIM = 64
RWKV_WIDTH = 512
RET_HEADS = 4
RET_HEAD_DIM = 128
RET_WIDTH = 512
DECAY_LORA = 64
AAA_LORA = 64
GATE_LORA = 128
LORA_WIDTH = DECAY_LORA + AAA_LORA + GATE_LORA
RET_CHUNK = 128
ROPE_BASE = 10000.0
D_FF = 2816
NORM_EPS = 1e-6
RWKV_GN_EPS = 64e-5
RET_GN_EPS = 1e-5

V7X_VMEM_BYTES = 64 * 1024 * 1024
SUBLANES = 8
LANES = 128

RWKV_CHUNK = 64
INV_BASE = 8
RWKV_GROUP_HEADS = 2
MIX_TILE = 512
MIX_STAGE = 256
PROJ_PIECE = 512
FFN_TILE = 512
FFN_STAGES = 2
FFN_COL_SPLIT = 2

N_RKV = 3 * RWKV_WIDTH
N_RET = 4 * RET_WIDTH
N_LORA = 2 * LORA_WIDTH
N_PROJ = N_RKV + N_RET + N_LORA

_LOG_GAMMA = [math.log(1.0 - 2.0 ** (-5.0 - h)) for h in range(RET_HEADS)]
_GAMMA_C = [math.exp(RET_CHUNK * lg) for lg in _LOG_GAMMA]


def _vmem_limit(nbytes):
    return int(min(nbytes, V7X_VMEM_BYTES - 4 * 1024 * 1024))


def _dot(a, b):
    return jnp.dot(a, b, preferred_element_type=F32)


def _dot_nt(a, b):
    return lax.dot_general(a, b, (((1,), (1,)), ((), ())), preferred_element_type=F32)


def _dot_tn(a, b):
    return lax.dot_general(a, b, (((0,), (0,)), ((), ())), preferred_element_type=F32)


def _split_dot(mat_bf16, x):
    hi = x.astype(BF16)
    lo = (x - hi.astype(F32)).astype(BF16)
    return _dot(mat_bf16, hi) + _dot(mat_bf16, lo)


def _shift_rows(x, prev_row):
    rolled = pltpu.roll(x, 1, 0)
    row = lax.broadcasted_iota(jnp.int32, x.shape, 0)
    return jnp.where(row == 0, prev_row, rolled)


def _rms_norm(x, g):
    ms = jnp.mean(x * x, axis=-1, keepdims=True)
    return x * lax.rsqrt(ms + NORM_EPS) * g


def _fold_lora_kernel(w1_ref, a1_ref, g1_ref, muw_ref, mua_ref, mug_ref, o_ref):
    c0 = 0
    for w_ref, mu_ref in ((w1_ref, muw_ref), (a1_ref, mua_ref), (g1_ref, mug_ref)):
        w, mu = w_ref[...], mu_ref[...]
        n = w.shape[1]
        o_ref[:, c0:c0 + n] = (w * (1.0 - mu)).astype(BF16)
        o_ref[:, LORA_WIDTH + c0:LORA_WIDTH + c0 + n] = (w * mu).astype(BF16)
        c0 += n


def _fold_lora(w1, a1, g1, mu_w, mu_a, mu_g):
    col = lambda v: v.reshape(-1, 1)
    return pl.pallas_call(
        _fold_lora_kernel,
        out_shape=jax.ShapeDtypeStruct((D_MODEL, 2 * LORA_WIDTH), BF16),
        name="fold_lora",
    )(w1, a1, g1, col(mu_w), col(mu_a), col(mu_g))


def _ret_tables_kernel(freq_ref, cos_ref, sin_ref, dmask_ref):
    C, HD = RET_CHUNK, RET_HEAD_DIM
    n = cos_ref.shape[0]
    ang = lax.broadcasted_iota(jnp.int32, (n, HD), 0).astype(F32) * freq_ref[...]
    lane = lax.broadcasted_iota(jnp.int32, (n, HD), 1)
    sin = jnp.sin(ang)
    cos_ref[...] = jnp.cos(ang)
    sin_ref[...] = jnp.where(lane < HD // 2, -sin, sin)
    ii = lax.broadcasted_iota(jnp.int32, (C, C), 0)
    jj = lax.broadcasted_iota(jnp.int32, (C, C), 1)
    diff = jnp.maximum((ii - jj).astype(F32), 0.0)
    for h in range(RET_HEADS):
        dmask_ref[h] = jnp.where(ii >= jj, jnp.exp(diff * _LOG_GAMMA[h]), 0.0)


def _ret_tables(freq, seq):
    return pl.pallas_call(
        _ret_tables_kernel,
        out_shape=[jax.ShapeDtypeStruct((seq, RET_HEAD_DIM), F32),
                   jax.ShapeDtypeStruct((seq, RET_HEAD_DIM), F32),
                   jax.ShapeDtypeStruct((RET_HEADS, RET_CHUNK, RET_CHUNK), F32)],
        name="ret_tables",
    )(freq)


(_V_MU_R, _V_MU_K, _V_MU_V, _V_W0, _V_A0, _V_KK, _V_KA, _V_RK, _V_LNW, _V_LNB) = range(10)
_N_VEC_ROWS = 16

_PREP_NAMES = ("a_t", "r_t", "b_t", "k_t", "b_w", "k_w", "w_c", "v", "bonus", "gate")

_PROJ_PIECES = ([("rkv", c) for c in range(0, N_RKV, PROJ_PIECE)]
                + [("ret", c) for c in range(N_RKV, N_RKV + N_RET, PROJ_PIECE)]
                + [("lora", c) for c in range(N_RKV + N_RET, N_PROJ, PROJ_PIECE)])


def _mixer_kernel(xfirst_ref, xnext_ref, gmix_ref, win_hbm, wlo_ref, vec_ref, w2_ref, seg_ref, tril_ref, cos_ref,
                  sin_ref, dmask_ref, gnw_ref, fw0_ref, fw1_ref, fw2_ref, fw3_ref,
                  out_ref, yret_ref, fb0_ref, fb1_ref, fb2_ref, fb3_ref,
                  h_ref, rstate_ref, prev_ref, prep0_ref, pret0_ref, proj1_ref, prj0_ref,
                  win_ref, wstage_ref, wsem):
    C, G, TT, STG = RWKV_CHUNK, RWKV_GROUP_HEADS, MIX_TILE, MIX_STAGE
    for f32_ref, bf16_ref in ((fw0_ref, fb0_ref), (fw1_ref, fb1_ref), (fw2_ref, fb2_ref), (fw3_ref, fb3_ref)):
        bf16_ref[...] = f32_ref[...].astype(BF16)
    HD = RWKV_HEAD_DIM
    W = G * HD
    GC = G * C
    n_groups = RWKV_WIDTH // W
    n_stages = TT // STG
    stage_rows = [slice(i * STG, (i + 1) * STG) for i in range(n_stages)]

    @pl.when(pl.program_id(1) == 0)
    def _():
        h_ref[...] = jnp.zeros_like(h_ref)
        rstate_ref[...] = jnp.zeros_like(rstate_ref)

    vec = lambda i: vec_ref[i:i + 1, :]

    seg = seg_ref[...]

    def head_sum(x):
        n_blk = x.shape[1] // LANES
        n_rows = x.shape[0]
        rows = jnp.concatenate([x[:, j * LANES:(j + 1) * LANES] for j in range(n_blk)], axis=0)
        s = _dot(rows.astype(BF16), seg)
        return jnp.concatenate([s[j * n_rows:(j + 1) * n_rows] for j in range(n_blk)], axis=1)

    def project(x_rows_ref, stage):
        got = {}

        def piece(i):
            def run():
                if "xb" not in got:
                    got["xb"] = _rms_norm(x_rows_ref[...], gmix_ref[...]).astype(BF16)
                name, c0 = _PROJ_PIECES[i]
                n_in = N_RKV + N_RET
                w = win_ref[:, c0:c0 + PROJ_PIECE] if c0 < n_in else wlo_ref[:, c0 - n_in:c0 - n_in + PROJ_PIECE]
                res = _dot(got["xb"], w)
                if stage == 1:
                    proj1_ref[:, c0:c0 + PROJ_PIECE] = res
                elif name == "ret":
                    pret0_ref[:, c0 - N_RKV:c0 - N_RKV + PROJ_PIECE] = res
                else:
                    d0 = c0 if name == "rkv" else c0 - N_RET
                    prj0_ref[:, d0:d0 + PROJ_PIECE] = res
            return run
        front = [piece(i) for i, (name, _) in enumerate(_PROJ_PIECES) if name != "ret"]
        ret = [piece(i) for i, (name, _) in enumerate(_PROJ_PIECES) if name == "ret"]
        return front, ret

    def prepare(stage, prev_rows, store):
        pp = {}

        def put(name, val):
            if store:
                prep0_ref[_PREP_NAMES.index(name)] = val
            else:
                pp[name] = val

        def prep_a():
            if stage == 1:
                p_rkv, p_lora = proj1_ref[:, 0:N_RKV], proj1_ref[:, N_RKV + N_RET:N_PROJ]
            else:
                p_rkv, p_lora = prj0_ref[:, 0:N_RKV], prj0_ref[:, N_RKV:N_RKV + N_LORA]
            prev_rkv, prev_lora = prev_rows()
            s_rkv = _shift_rows(p_rkv, prev_rkv)
            lerp = lambda j, mu: (p_rkv[:, j * RWKV_WIDTH:(j + 1) * RWKV_WIDTH]
                                  + (s_rkv[:, j * RWKV_WIDTH:(j + 1) * RWKV_WIDTH]
                                     - p_rkv[:, j * RWKV_WIDTH:(j + 1) * RWKV_WIDTH]) * mu)
            if store:
                prev_ref[:, 0:N_RKV] = p_rkv[STG - SUBLANES:]
                prev_ref[:, N_RKV:N_RKV + N_LORA] = p_lora[STG - SUBLANES:]
            pp["r"] = lerp(0, vec(_V_MU_R))
            pp["k"] = lerp(1, vec(_V_MU_K))
            pp["v"] = lerp(2, vec(_V_MU_V))
            put("v", pp["v"])
            low = p_lora[:, 0:LORA_WIDTH] + _shift_rows(p_lora, prev_lora)[:, LORA_WIDTH:2 * LORA_WIDTH]
            lane = lax.broadcasted_iota(jnp.int32, low.shape, 1)
            act = jnp.where(lane < DECAY_LORA, jnp.tanh(low),
                            jnp.where(lane < DECAY_LORA + AAA_LORA, low, jax.nn.sigmoid(low)))
            pp["second"] = _dot(act.astype(BF16), w2_ref[...])

        def prep_b():
            second = pp["second"]
            pp["ld"] = -math.exp(-0.5) * jax.nn.sigmoid(second[:, 0:RWKV_WIDTH] + vec(_V_W0))
            pp["a"] = jax.nn.sigmoid(second[:, RWKV_WIDTH:2 * RWKV_WIDTH] + vec(_V_A0))
            put("gate", second[:, 2 * RWKV_WIDTH:3 * RWKV_WIDTH])
            pp["kk"] = pp["k"] * vec(_V_KK)
            pp["kk_ss"] = head_sum(pp["kk"] * pp["kk"])
            pp["cum"] = _split_dot(tril_ref[...], pp["ld"])

        def prep_c():
            pp["kk"] = pp["kk"] * lax.rsqrt(jnp.maximum(pp["kk_ss"], 1e-24))
            pp["k2"] = pp["k"] * (1.0 + (pp["a"] - 1.0) * vec(_V_KA))
            put("bonus", head_sum(pp["r"] * pp["k2"] * vec(_V_RK)) * pp["v"])

        def prep_d():
            cum, ld, kk, k2 = pp["cum"], pp["ld"], pp["kk"], pp["k2"]
            b = kk * pp["a"]
            cum_end = jnp.concatenate(
                [jnp.broadcast_to(cum[(c + 1) * C - 1:(c + 1) * C, :], (C, RWKV_WIDTH))
                 for c in range(STG // C)], axis=0)
            e_neg = jnp.exp(-cum)
            e_end = jnp.exp(cum_end - cum)
            put("a_t", -kk * jnp.exp(cum - ld))
            put("r_t", pp["r"] * jnp.exp(cum))
            put("b_t", b * e_neg)
            put("k_t", k2 * e_neg)
            put("b_w", b * e_end)
            put("k_w", k2 * e_end)
            put("w_c", jnp.exp(cum_end))

        return pp, [prep_a, prep_b, prep_c, prep_d]

    def run_all(thunks):
        for thunk in thunks:
            thunk()

    zero_prev = lambda: (jnp.zeros((1, N_RKV), F32), jnp.zeros((1, N_LORA), F32))

    @pl.when((pl.program_id(0) == 0) & (pl.program_id(1) == 0))
    def _():
        pieces = []
        for s in range(n_stages):
            front, ret = project(xfirst_ref.at[stage_rows[s], :], s)
            n_rkv = N_RKV // PROJ_PIECE
            pieces.append(front[:n_rkv] + ret + front[n_rkv:])
        copies = [pltpu.make_async_copy(win_hbm.at[:, pl.ds(c0, PROJ_PIECE)], wstage_ref.at[i % 2], wsem.at[i % 2])
                  for i, c0 in enumerate(range(0, N_RKV + N_RET, PROJ_PIECE))]
        copies[0].start()
        for i in range(len(_PROJ_PIECES)):
            if i < len(copies):
                if i + 1 < len(copies):
                    copies[i + 1].start()
                copies[i].wait()
                win_ref[:, i * PROJ_PIECE:(i + 1) * PROJ_PIECE] = wstage_ref[i % 2].astype(BF16)
            for s in range(n_stages):
                pieces[s][i]()
        run_all(prepare(0, zero_prev, store=True)[1])

    ret_state = [rstate_ref[h] for h in range(RET_HEADS)]

    def ret_unit(c, h):
        RC, RD = RET_CHUNK, RET_HEAD_DIM
        s, c_loc = divmod(c * RC, STG)
        rows = slice(c * RC, (c + 1) * RC)
        loc = slice(c_loc, c_loc + RC)
        row_i = lax.broadcasted_iota(jnp.int32, (RC, 1), 0).astype(F32)
        lg = _LOG_GAMMA[h]
        cos, sin = cos_ref[rows, :], sin_ref[rows, :]
        rot = lambda xh: xh * cos + pltpu.roll(xh, RD // 2, 1) * sin

        def part(j):
            c0 = j * RET_WIDTH + h * RD
            if s == 0:
                return pret0_ref[loc, c0:c0 + RD]
            return proj1_ref[loc, N_RKV + c0:N_RKV + c0 + RD]

        q = rot(part(0))
        kr = rot(part(1)) * (RD ** -0.5)
        vr = part(2)
        gt = part(3)
        qb, kb, vb = q.astype(BF16), kr.astype(BF16), vr.astype(BF16)
        scores = (_dot_nt(qb, kb) * dmask_ref[h]).astype(BF16)
        kv = _dot_tn(kb, (vr * jnp.exp((RC - 1.0 - row_i) * lg)).astype(BF16))
        inter = _dot((q * jnp.exp((row_i + 1.0) * lg)).astype(BF16), ret_state[h].astype(BF16))
        ret_state[h] = ret_state[h] * _GAMMA_C[h] + kv

        def second_half():
            y = _dot(scores, vb) + inter
            mu = jnp.mean(y, axis=-1, keepdims=True)
            d = y - mu
            var = jnp.mean(d * d, axis=-1, keepdims=True)
            yn = d * lax.rsqrt(var + RET_GN_EPS) * gnw_ref[:, h * RD:(h + 1) * RD]
            yret_ref[rows, h * RD:(h + 1) * RD] = (gt * jax.nn.sigmoid(gt) * yn).astype(BF16)
        return second_half

    ret_units = [(c, h) for c in range(TT // RET_CHUNK) for h in range(RET_HEADS)]
    ret_pending = []
    slot_queue = []

    def fill():
        if ret_pending:
            ret_pending.pop(0)()
        if ret_units:
            ret_pending.append(ret_unit(*ret_units.pop(0)))
        if slot_queue:
            for thunk in slot_queue.pop(0):
                thunk()

    c_bits = C.bit_length() - 1
    hd_bits = HD.bit_length() - 1
    assert C == 1 << c_bits and HD == 1 << hd_bits and W == GC
    sr = lax.broadcasted_iota(jnp.int32, (GC, W), 0)
    sl_ = lax.broadcasted_iota(jnp.int32, (GC, W), 1)
    stack_mask = (sr >> c_bits) == (sl_ >> hd_bits)
    ti = lax.broadcasted_iota(jnp.int32, (C, GC), 0)
    si = lax.broadcasted_iota(jnp.int32, (C, GC), 1) & (C - 1)
    strict = ti > si
    incl = ti >= si
    eye_c = (ti == si).astype(F32)
    same_block = {}
    size = INV_BASE
    while size <= C:
        bits = size.bit_length() - 1
        same_block[size] = (ti >> bits) == (si >> bits)
        size *= 2
    wi = lax.broadcasted_iota(jnp.int32, (W, W), 0)
    wj = lax.broadcasted_iota(jnp.int32, (W, W), 1)
    eye_w = wi == wj

    def stack(x):
        return jnp.where(stack_mask, jnp.concatenate([x] * G, axis=0), 0.0)

    blk = lambda z, c, g: z[c * C:(c + 1) * C, g * W:(g + 1) * W]

    def chains_of(p):
        st = []
        for c in range(STG // C):
            for g in range(n_groups):
                a_c, r_c = blk(p["a_t"], c, g), blk(p["r_t"], c, g)
                st.append(dict(
                    ar=jnp.concatenate([a_c, r_c], axis=0).astype(BF16), r32=r_c,
                    a_s=stack(a_c).astype(BF16),
                    bk_s=jnp.concatenate([stack(blk(p["b_t"], c, g)), stack(blk(p["k_t"], c, g))],
                                         axis=0).astype(BF16),
                    bw_t=stack(blk(p["b_w"], c, g)).T.astype(BF16),
                    kw_t=stack(blk(p["k_w"], c, g)).T.astype(BF16),
                    v_s=stack(blk(p["v"], c, g)).astype(BF16), w_end=blk(p["w_c"], c, g)[0:1, :]))
        for s in st:
            prod = _dot_nt(s["ar"], s["bk_s"])
            s["n"] = jnp.where(strict, prod[0:C, 0:GC], 0.0)
            s["a_ak"] = jnp.where(strict, prod[0:C, GC:], 0.0).astype(BF16)
            s["a_rb"] = jnp.where(incl, prod[C:, 0:GC], 0.0).astype(BF16)
            s["a_rk"] = jnp.where(incl, prod[C:, GC:], 0.0).astype(BF16)
        fill()
        for s in st:
            res = _dot(jnp.concatenate([s["a_ak"], s["a_rk"], s["kw_t"]], axis=0), s["v_s"])
            s["av_s"] = stack(res[0:C]).astype(BF16)
            s["rkv"] = res[C:2 * C]
            s["kwv"] = res[2 * C:]
        fill()
        assert INV_BASE == 8 and C == 64
        for s in st:
            nd = jnp.where(same_block[8], s["n"], 0.0)
            s["t"] = eye_c + nd
            s["p"] = _dot(nd.astype(BF16), stack(nd).astype(BF16))
        fill()
        for s in st:
            res = _dot(jnp.concatenate([s["t"], s["p"]], axis=0).astype(BF16), stack(s["p"]).astype(BF16))
            s["t"] = s["t"] + res[0:C]
            s["p"] = res[C:]
        for s in st:
            s["t"] = s["t"] + _dot(s["t"].astype(BF16), stack(s["p"]).astype(BF16))
        fill()
        for s in st:
            b_off = [jnp.where(same_block[2 * size] & ~same_block[size], s["n"], 0.0) for size in (8, 16, 32)]
            res = _dot(jnp.concatenate(b_off, axis=0).astype(BF16), stack(s["t"]).astype(BF16))
            s["x"], s["u16"], s["u32"] = res[0:C], res[C:2 * C], res[2 * C:]
        fill()
        for s in st:
            res = _dot(jnp.concatenate([s["t"], s["u16"], s["u32"]], axis=0).astype(BF16),
                       stack(s["x"]).astype(BF16))
            s["t"], s["x"], s["u32"] = s["t"] + res[0:C], s["u16"] + res[C:2 * C], s["u32"] + res[2 * C:]
        fill()
        for s in st:
            res = _dot(jnp.concatenate([s["t"], s["u32"]], axis=0).astype(BF16), stack(s["x"]).astype(BF16))
            s["t"], s["x"] = s["t"] + res[0:C], s["u32"] + res[C:]
        fill()
        for s in st:
            s["t"] = (s["t"] + _dot(s["t"].astype(BF16), stack(s["x"]).astype(BF16))).astype(BF16)
        fill()
        for s in st:
            s["x1_s"] = stack(_dot(s["t"], s["a_s"])).astype(BF16)
            s["x2_s"] = stack(_dot(s["t"], s["av_s"])).astype(BF16)
        for s in st:
            lhs = jnp.concatenate([s["a_rb"], s["bw_t"]], axis=0)
            o1 = _dot(lhs, s["x1_s"])
            o2 = _dot(lhs, s["x2_s"])
            s["q"] = (s["r32"] + o1[0:C]).astype(BF16)
            s["m"] = (o1[C:] + jnp.where(eye_w, s["w_end"], 0.0)).astype(BF16)
            s["y0"] = o2[0:C] + s["rkv"]
            s["g"] = o2[C:] + s["kwv"]
        return st

    def state_steps(st, h_cur, fillers):
        y_rows = []
        n_chunks = STG // C
        per_step = -(-len(fillers) // n_chunks)
        for c in range(n_chunks):
            y_lanes = []
            for g in range(n_groups):
                s = st[c * n_groups + g]
                res = _dot(jnp.concatenate([s["q"], s["m"]], axis=0), h_cur[g].astype(BF16))
                y_lanes.append(res[0:C] + s["y0"])
                h_cur[g] = res[C:] + s["g"]
            y_rows.append(jnp.concatenate(y_lanes, axis=1))
            run_all(fillers[c * per_step:(c + 1) * per_step])
        return jnp.concatenate(y_rows, axis=0)

    def finish(rows, y, p):
        inv_hd = 1.0 / HD
        mu = head_sum(y) * inv_hd
        d = y - mu
        var = head_sum(d * d) * inv_hd
        yn = d * lax.rsqrt(var + RWKV_GN_EPS) * vec(_V_LNW) + vec(_V_LNB)
        out_ref[rows, :] = ((yn + p["bonus"]) * p["gate"]).astype(BF16)

    assert n_stages == 2
    h_cur = [h_ref[g] for g in range(n_groups)]
    prep_s0 = {name: prep0_ref.at[i] for i, name in enumerate(_PREP_NAMES)}
    prep_s0["bonus"], prep_s0["gate"] = prep_s0["bonus"][...], prep_s0["gate"][...]
    last_s1 = (proj1_ref[STG - 1:STG, 0:N_RKV], proj1_ref[STG - 1:STG, N_RKV + N_RET:N_PROJ])
    last_step_of_row = pl.program_id(1) == pl.num_programs(1) - 1

    def chains_with_fillers(p, thunks):
        for t in thunks:
            slot_queue.extend([[t], []])
        st = chains_of(p)
        while slot_queue:
            fill()
        return st

    def interleave(a, b):
        return [t for pair in zip(a, b) for t in pair]

    front_n0, ret_n0 = project(xnext_ref.at[stage_rows[0], :], 0)
    chains_s0 = chains_with_fillers(prep_s0, front_n0)

    prev_s1 = lambda: (prev_ref[SUBLANES - 1:SUBLANES, 0:N_RKV],
                       prev_ref[SUBLANES - 1:SUBLANES, N_RKV:N_RKV + N_LORA])
    prep_s1, thunks = prepare(1, prev_s1, store=False)
    y_s0 = state_steps(chains_s0, h_cur, interleave(thunks, ret_n0))

    front_n1, ret_n1 = project(xnext_ref.at[stage_rows[1], :], 1)
    chains_s1 = chains_with_fillers(prep_s1, front_n1)

    finish(stage_rows[0], y_s0, prep_s0)
    prev_n0 = lambda: tuple(jnp.where(last_step_of_row, 0.0, z) for z in last_s1)
    y_s1 = state_steps(chains_s1, h_cur, interleave(prepare(0, prev_n0, store=True)[1], ret_n1))
    finish(stage_rows[1], y_s1, prep_s1)
    while ret_units or ret_pending:
        fill()
    for g in range(n_groups):
        h_ref[g] = h_cur[g]
    for h in range(RET_HEADS):
        rstate_ref[h] = ret_state[h]


def _mixer(x2, g_mix, w_in, w_low, vecs, w2cat, seg, tril, freq, gn_w, later_weights, bsz, seq):
    tt = MIX_TILE
    nt = seq // tt
    w = RWKV_GROUP_HEADS * RWKV_HEAD_DIM
    n_groups = RWKV_WIDTH // w
    cos, sin, dmask = _ret_tables(freq, seq)
    row = lambda b, t: (b * nt + t, 0)
    cast_specs = []
    for m in later_weights:
        rows = m.shape[0] // (bsz * nt)
        share = 1 if rows % (2 * SUBLANES) == 0 else 2
        assert (rows * share) % (2 * SUBLANES) == 0 and m.shape[0] % (rows * share) == 0
        cast_specs.append(pl.BlockSpec((rows * share, m.shape[1]), lambda b, t, share=share: ((b * nt + t) // share, 0)))
    next_tile = lambda b, t: (jnp.minimum(b * nt + t + 1, bsz * nt - 1), 0)
    pos = lambda b, t: (t, 0)
    const = lambda b, t: (0, 0)
    return pl.pallas_call(
        _mixer_kernel,
        grid=(bsz, nt),
        in_specs=[
            pl.BlockSpec((tt, D_MODEL), const),
            pl.BlockSpec((tt, D_MODEL), next_tile),
            pl.BlockSpec((1, D_MODEL), const),
            pl.BlockSpec(memory_space=pl.ANY),
            pl.BlockSpec((D_MODEL, N_LORA), const, pipeline_mode=pl.Buffered(1)),
            pl.BlockSpec((_N_VEC_ROWS, RWKV_WIDTH), const),
            pl.BlockSpec((LORA_WIDTH, 3 * RWKV_WIDTH), const),
            pl.BlockSpec((LANES, LANES), const),
            pl.BlockSpec((MIX_STAGE, MIX_STAGE), const),
            pl.BlockSpec((tt, RET_HEAD_DIM), pos),
            pl.BlockSpec((tt, RET_HEAD_DIM), pos),
            pl.BlockSpec((RET_HEADS, RET_CHUNK, RET_CHUNK), lambda b, t: (0, 0, 0)),
            pl.BlockSpec((1, RET_WIDTH), const),
        ] + cast_specs,
        out_specs=[pl.BlockSpec((tt, RWKV_WIDTH), row), pl.BlockSpec((tt, RET_WIDTH), row)] + cast_specs,
        out_shape=[jax.ShapeDtypeStruct((bsz * seq, RWKV_WIDTH), BF16),
                   jax.ShapeDtypeStruct((bsz * seq, RET_WIDTH), BF16)]
                  + [jax.ShapeDtypeStruct(m.shape, BF16) for m in later_weights],
        scratch_shapes=[pltpu.VMEM((n_groups, w, w), F32),
                        pltpu.VMEM((RET_HEADS, RET_HEAD_DIM, RET_HEAD_DIM), F32),
                        pltpu.VMEM((SUBLANES, N_RKV + N_LORA), F32),
                        pltpu.VMEM((len(_PREP_NAMES), MIX_STAGE, RWKV_WIDTH), F32),
                        pltpu.VMEM((MIX_STAGE, N_RET), F32),
                        pltpu.VMEM((MIX_STAGE, N_PROJ), F32),
                        pltpu.VMEM((MIX_STAGE, N_RKV + N_LORA), F32),
                        pltpu.VMEM((D_MODEL, N_RKV + N_RET), BF16),
                        pltpu.VMEM((2, D_MODEL, PROJ_PIECE), F32),
                        pltpu.SemaphoreType.DMA((2,))],
        compiler_params=pltpu.CompilerParams(
            dimension_semantics=("arbitrary", "arbitrary"), vmem_limit_bytes=_vmem_limit(56 * 1024 * 1024)),
        name="mixer",
    )(x2, x2, g_mix, w_in, w_low, vecs, w2cat, seg, tril, cos, sin, dmask, gn_w, *later_weights)


def _ffn_kernel(yr_ref, yt_ref, x_ref, wo_ref, gf_ref, wg_ref, wu_ref, cw_ref, cb_ref, wd_ref, gl_ref,
                out_ref, carry_ref):
    sm = FFN_TILE // FFN_STAGES
    stages = [slice(i * sm, (i + 1) * sm) for i in range(FFN_STAGES)]

    @pl.when(pl.program_id(1) == 0)
    def _():
        carry_ref[...] = jnp.zeros_like(carry_ref)

    mix = [_dot(yr_ref[s, :], wo_ref[0:RWKV_WIDTH, :]) + _dot(yt_ref[s, :], wo_ref[RWKV_WIDTH:, :])
           for s in stages]
    acc = [x_ref[s, :] + m for s, m in zip(stages, mix)]
    hb = [_rms_norm(x1, gf_ref[...]).astype(BF16) for x1 in acc]

    cols = D_FF // FFN_COL_SPLIT
    for j in range(FFN_COL_SPLIT):
        cs = slice(j * cols, (j + 1) * cols)
        gate_up = [(_dot(h, wg_ref[:, cs]), _dot(h, wu_ref[:, cs])) for h in hb]
        prev = carry_ref[:, cs]
        p2, p1 = prev[SUBLANES - 2:SUBLANES - 1], prev[SUBLANES - 1:SUBLANES]
        hidden = []
        for gate, up in gate_up:
            row = lax.broadcasted_iota(jnp.int32, gate.shape, 0)
            g1 = jnp.where(row == 0, p1, pltpu.roll(gate, 1, 0))
            g2 = jnp.where(row == 0, p2, jnp.where(row == 1, p1, pltpu.roll(gate, 2, 0)))
            p2, p1 = gate[sm - 2:sm - 1], gate[sm - 1:sm]
            conv = cw_ref[0:1, cs] * g2 + cw_ref[1:2, cs] * g1 + cw_ref[2:3, cs] * gate + cb_ref[:, cs]
            hidden.append((conv * jax.nn.sigmoid(conv) * up).astype(BF16))
        carry_ref[:, cs] = gate_up[-1][0][sm - SUBLANES:sm]
        acc = [a + _dot(h, wd_ref[cs, :]) for a, h in zip(acc, hidden)]
    for s, a in zip(stages, acc):
        out_ref[s, :] = _rms_norm(a, gl_ref[...])


def _ffn(y_rwkv, y_ret, x2, w_out, g_ffn, w_gate, w_up, conv_w, conv_b, w_down, g_final, bsz, seq):
    tm = FFN_TILE
    nt = seq // tm
    row = lambda b, t: (b * nt + t, 0)
    const = lambda b, t: (0, 0)
    single = dict(pipeline_mode=pl.Buffered(1))
    weights = (D_MODEL * D_MODEL + 3 * D_MODEL * D_FF) * 2
    vmem = weights + 4 * tm * D_MODEL * 4 + 8 * tm * D_FF * 4 + 8 * 1024 * 1024
    return pl.pallas_call(
        _ffn_kernel,
        grid=(bsz, nt),
        in_specs=[
            pl.BlockSpec((tm, RWKV_WIDTH), row),
            pl.BlockSpec((tm, RET_WIDTH), row),
            pl.BlockSpec((tm, D_MODEL), row),
            pl.BlockSpec((D_MODEL, D_MODEL), const, **single),
            pl.BlockSpec((1, D_MODEL), const),
            pl.BlockSpec((D_MODEL, D_FF), const, **single),
            pl.BlockSpec((D_MODEL, D_FF), const, **single),
            pl.BlockSpec((SUBLANES, D_FF), const),
            pl.BlockSpec((1, D_FF), const),
            pl.BlockSpec((D_FF, D_MODEL), const, **single),
            pl.BlockSpec((1, D_MODEL), const),
        ],
        out_specs=pl.BlockSpec((tm, D_MODEL), row),
        out_shape=jax.ShapeDtypeStruct((bsz * seq, D_MODEL), F32),
        scratch_shapes=[pltpu.VMEM((SUBLANES, D_FF), F32)],
        compiler_params=pltpu.CompilerParams(
            dimension_semantics=("arbitrary", "arbitrary"), vmem_limit_bytes=_vmem_limit(vmem)),
        name="ffn",
    )(y_rwkv, y_ret, x2, w_out, g_ffn, w_gate, w_up, conv_w, conv_b, w_down, g_final)


def _block_ones(n, block):
    i = np.arange(n)
    return (i[:, None] // block) == (i[None, :] // block)


def kernel(x, norm_mix_g, w_in, rwkv_mu_r, rwkv_mu_k, rwkv_mu_v, rwkv_mu_w, rwkv_mu_a, rwkv_mu_g, rwkv_w0, rwkv_w1, rwkv_w2, rwkv_a0, rwkv_a1, rwkv_a2, rwkv_g1, rwkv_g2, rwkv_k_k, rwkv_k_a, rwkv_r_k, rwkv_lnx_w, rwkv_lnx_b, ret_gn_w, w_out, norm_ffn_g, ffn_w_gate, ffn_w_up, ffn_conv_w, ffn_conv_b, ffn_w_down, norm_final_g):
    bsz, seq, d = x.shape
    assert d == D_MODEL and seq % MIX_TILE == 0 and seq % FFN_TILE == 0
    assert norm_mix_g.shape[0] == 1, "one layer"
    x2 = x.reshape(bsz * seq, d)
    row = lambda p: p.reshape(1, -1)

    w_low = _fold_lora(rwkv_w1[0], rwkv_a1[0], rwkv_g1[0], rwkv_mu_w[0], rwkv_mu_a[0], rwkv_mu_g[0])

    vec_rows = [rwkv_mu_r[0], rwkv_mu_k[0], rwkv_mu_v[0], rwkv_w0[0], rwkv_a0[0], rwkv_k_k[0], rwkv_k_a[0],
                rwkv_r_k[0].reshape(-1), rwkv_lnx_w[0], rwkv_lnx_b[0]]
    vecs = jnp.zeros((_N_VEC_ROWS, RWKV_WIDTH), F32).at[:len(vec_rows)].set(jnp.stack(vec_rows))
    w2cat = jnp.zeros((LORA_WIDTH, 3 * RWKV_WIDTH), BF16)
    w2cat = w2cat.at[0:DECAY_LORA, 0:RWKV_WIDTH].set(rwkv_w2[0].astype(BF16))
    w2cat = w2cat.at[DECAY_LORA:DECAY_LORA + AAA_LORA, RWKV_WIDTH:2 * RWKV_WIDTH].set(rwkv_a2[0].astype(BF16))
    w2cat = w2cat.at[DECAY_LORA + AAA_LORA:, 2 * RWKV_WIDTH:].set(rwkv_g2[0].astype(BF16))
    seg = jnp.asarray(_block_ones(LANES, RWKV_HEAD_DIM), BF16)
    ti = np.arange(MIX_STAGE)
    tril = jnp.asarray(_block_ones(MIX_STAGE, RWKV_CHUNK) & (ti[:, None] >= ti[None, :]), BF16)
    half = RET_HEAD_DIM // 2
    inv_freq = ROPE_BASE ** (-jnp.arange(half, dtype=F32) / half)
    freq = jnp.concatenate([inv_freq, inv_freq]).reshape(1, RET_HEAD_DIM)
    y_rwkv, y_ret, wo_b, wg_b, wu_b, wd_b = _mixer(
        x2, row(norm_mix_g[0]), w_in[0], w_low, vecs, w2cat, seg, tril, freq, row(ret_gn_w[0]),
        (w_out[0], ffn_w_gate[0], ffn_w_up[0], ffn_w_down[0]), bsz, seq)

    conv_w = jnp.zeros((SUBLANES, D_FF), F32).at[0:3].set(ffn_conv_w[0][:, 0, :])
    out = _ffn(y_rwkv, y_ret, x2, wo_b, row(norm_ffn_g[0]), wg_b, wu_b, conv_w,
               row(ffn_conv_b[0]), wd_b, row(norm_final_g), bsz, seq)
    return out.reshape(bsz, seq, d)
```

```python
import math

import jax
import jax.numpy as jnp
import numpy as np
from jax import lax
from jax.experimental import pallas as pl
from jax.experimental.pallas import tpu as pltpu

F32 = jnp.float32
BF16 = jnp.bfloat16

D_MODEL = 1024
RWKV_HEADS = 8
RWKV_HEAD_DIM = 64
RWKV_WIDTH = 512
RET_HEADS = 4
RET_HEAD_DIM = 128
RET_WIDTH = 512
DECAY_LORA = 64
AAA_LORA = 64
GATE_LORA = 128
LORA_WIDTH = DECAY_LORA + AAA_LORA + GATE_LORA
RET_CHUNK = 128
ROPE_BASE = 10000.0
D_FF = 2816
NORM_EPS = 1e-6
RWKV_GN_EPS = 64e-5
RET_GN_EPS = 1e-5

V7X_VMEM_BYTES = 64 * 1024 * 1024
SUBLANES = 8
LANES = 128

RWKV_CHUNK = 64
INV_BASE = 8
RWKV_GROUP_HEADS = 2
MIX_TILE = 512
MIX_STAGE = 256
PROJ_PIECE = 512
FFN_TILE = 1024
FFN_STAGES = 2
FFN_COL_SPLIT = 2

N_RKV = 3 * RWKV_WIDTH
N_RET = 4 * RET_WIDTH
N_LORA = 2 * LORA_WIDTH
N_PROJ = N_RKV + N_RET + N_LORA

_LOG_GAMMA = [math.log(1.0 - 2.0 ** (-5.0 - h)) for h in range(RET_HEADS)]
_GAMMA_C = [math.exp(RET_CHUNK * lg) for lg in _LOG_GAMMA]


def _vmem_limit(nbytes):
    return int(min(nbytes, V7X_VMEM_BYTES - 4 * 1024 * 1024))


def _dot(a, b):
    return jnp.dot(a, b, preferred_element_type=F32)


def _dot_nt(a, b):
    return lax.dot_general(a, b, (((1,), (1,)), ((), ())), preferred_element_type=F32)


def _dot_tn(a, b):
    return lax.dot_general(a, b, (((0,), (0,)), ((), ())), preferred_element_type=F32)


def _split_dot(mat_bf16, x):
    hi = x.astype(BF16)
    lo = (x - hi.astype(F32)).astype(BF16)
    return _dot(mat_bf16, hi) + _dot(mat_bf16, lo)


def _shift_rows(x, prev_row):
    rolled = pltpu.roll(x, 1, 0)
    row = lax.broadcasted_iota(jnp.int32, x.shape, 0)
    return jnp.where(row == 0, prev_row, rolled)


def _rms_norm(x, g):
    ms = jnp.mean(x * x, axis=-1, keepdims=True)
    return x * lax.rsqrt(ms + NORM_EPS) * g


def _fold_lora_kernel(w1_ref, a1_ref, g1_ref, muw_ref, mua_ref, mug_ref, o_ref):
    c0 = 0
    for w_ref, mu_ref in ((w1_ref, muw_ref), (a1_ref, mua_ref), (g1_ref, mug_ref)):
        w, mu = w_ref[...], mu_ref[...]
        n = w.shape[1]
        o_ref[:, c0:c0 + n] = (w * (1.0 - mu)).astype(BF16)
        o_ref[:, LORA_WIDTH + c0:LORA_WIDTH + c0 + n] = (w * mu).astype(BF16)
        c0 += n


def _fold_lora(w1, a1, g1, mu_w, mu_a, mu_g):
    col = lambda v: v.reshape(-1, 1)
    return pl.pallas_call(
        _fold_lora_kernel,
        out_shape=jax.ShapeDtypeStruct((D_MODEL, 2 * LORA_WIDTH), BF16),
        name="fold_lora",
    )(w1, a1, g1, col(mu_w), col(mu_a), col(mu_g))


def _ret_tables_kernel(freq_ref, cos_ref, sin_ref, dmask_ref):
    C, HD = RET_CHUNK, RET_HEAD_DIM
    n = cos_ref.shape[0]
    ang = lax.broadcasted_iota(jnp.int32, (n, HD), 0).astype(F32) * freq_ref[...]
    lane = lax.broadcasted_iota(jnp.int32, (n, HD), 1)
    sin = jnp.sin(ang)
    cos_ref[...] = jnp.cos(ang)
    sin_ref[...] = jnp.where(lane < HD // 2, -sin, sin)
    ii = lax.broadcasted_iota(jnp.int32, (C, C), 0)
    jj = lax.broadcasted_iota(jnp.int32, (C, C), 1)
    diff = jnp.maximum((ii - jj).astype(F32), 0.0)
    for h in range(RET_HEADS):
        dmask_ref[h] = jnp.where(ii >= jj, jnp.exp(diff * _LOG_GAMMA[h]), 0.0)


def _ret_tables(freq, seq):
    return pl.pallas_call(
        _ret_tables_kernel,
        out_shape=[jax.ShapeDtypeStruct((seq, RET_HEAD_DIM), F32),
                   jax.ShapeDtypeStruct((seq, RET_HEAD_DIM), F32),
                   jax.ShapeDtypeStruct((RET_HEADS, RET_CHUNK, RET_CHUNK), F32)],
        name="ret_tables",
    )(freq)


(_V_MU_R, _V_MU_K, _V_MU_V, _V_W0, _V_A0, _V_KK, _V_KA, _V_RK, _V_LNW, _V_LNB) = range(10)
_N_VEC_ROWS = 16

_PREP_NAMES = ("a_t", "r_t", "b_t", "k_t", "b_w", "k_w", "w_c", "v", "bonus", "gate")

_PROJ_PIECES = ([("rkv", c) for c in range(0, N_RKV, PROJ_PIECE)]
                + [("ret", c) for c in range(N_RKV, N_RKV + N_RET, PROJ_PIECE)]
                + [("lora", c) for c in range(N_RKV + N_RET, N_PROJ, PROJ_PIECE)])


def _mixer_kernel(xfirst_ref, xnext_ref, gmix_ref, win_hbm, wlo_ref, vec_ref, w2_ref, seg_ref, tril_ref, cos_ref,
                  sin_ref, dmask_ref, gnw_ref, fw0_ref, fw1_ref, fw2_ref, fw3_ref,
                  out_ref, yret_ref, fb0_ref, fb1_ref, fb2_ref, fb3_ref,
                  h_ref, rstate_ref, prev_ref, prep0_ref, pret0_ref, proj1_ref, prj0_ref,
                  win_ref, wstage_ref, wsem):
    C, G, TT, STG = RWKV_CHUNK, RWKV_GROUP_HEADS, MIX_TILE, MIX_STAGE
    for f32_ref, bf16_ref in ((fw0_ref, fb0_ref), (fw1_ref, fb1_ref), (fw2_ref, fb2_ref), (fw3_ref, fb3_ref)):
        bf16_ref[...] = f32_ref[...].astype(BF16)
    HD = RWKV_HEAD_DIM
    W = G * HD
    GC = G * C
    n_groups = RWKV_WIDTH // W
    n_stages = TT // STG
    stage_rows = [slice(i * STG, (i + 1) * STG) for i in range(n_stages)]

    @pl.when(pl.program_id(1) == 0)
    def _():
        h_ref[...] = jnp.zeros_like(h_ref)
        rstate_ref[...] = jnp.zeros_like(rstate_ref)

    vec = lambda i: vec_ref[i:i + 1, :]

    seg = seg_ref[...]

    def head_sum(x):
        n_blk = x.shape[1] // LANES
        n_rows = x.shape[0]
        rows = jnp.concatenate([x[:, j * LANES:(j + 1) * LANES] for j in range(n_blk)], axis=0)
        s = _dot(rows.astype(BF16), seg)
        return jnp.concatenate([s[j * n_rows:(j + 1) * n_rows] for j in range(n_blk)], axis=1)

    def project(x_rows_ref, stage):
        got = {}

        def piece(i):
            def run():
                if "xb" not in got:
                    got["xb"] = _rms_norm(x_rows_ref[...], gmix_ref[...]).astype(BF16)
                name, c0 = _PROJ_PIECES[i]
                n_in = N_RKV + N_RET
                w = win_ref[:, c0:c0 + PROJ_PIECE] if c0 < n_in else wlo_ref[:, c0 - n_in:c0 - n_in + PROJ_PIECE]
                res = _dot(got["xb"], w)
                if stage == 1:
                    proj1_ref[:, c0:c0 + PROJ_PIECE] = res
                elif name == "ret":
                    pret0_ref[:, c0 - N_RKV:c0 - N_RKV + PROJ_PIECE] = res
                else:
                    d0 = c0 if name == "rkv" else c0 - N_RET
                    prj0_ref[:, d0:d0 + PROJ_PIECE] = res
            return run
        front = [piece(i) for i, (name, _) in enumerate(_PROJ_PIECES) if name != "ret"]
        ret = [piece(i) for i, (name, _) in enumerate(_PROJ_PIECES) if name == "ret"]
        return front, ret

    def prepare(stage, prev_rows, store):
        pp = {}

        def put(name, val):
            if store:
                prep0_ref[_PREP_NAMES.index(name)] = val
            else:
                pp[name] = val

        def prep_a():
            if stage == 1:
                p_rkv, p_lora = proj1_ref[:, 0:N_RKV], proj1_ref[:, N_RKV + N_RET:N_PROJ]
            else:
                p_rkv, p_lora = prj0_ref[:, 0:N_RKV], prj0_ref[:, N_RKV:N_RKV + N_LORA]
            prev_rkv, prev_lora = prev_rows()
            s_rkv = _shift_rows(p_rkv, prev_rkv)
            lerp = lambda j, mu: (p_rkv[:, j * RWKV_WIDTH:(j + 1) * RWKV_WIDTH]
                                  + (s_rkv[:, j * RWKV_WIDTH:(j + 1) * RWKV_WIDTH]
                                     - p_rkv[:, j * RWKV_WIDTH:(j + 1) * RWKV_WIDTH]) * mu)
            if store:
                prev_ref[:, 0:N_RKV] = p_rkv[STG - SUBLANES:]
                prev_ref[:, N_RKV:N_RKV + N_LORA] = p_lora[STG - SUBLANES:]
            pp["r"] = lerp(0, vec(_V_MU_R))
            pp["k"] = lerp(1, vec(_V_MU_K))
            pp["v"] = lerp(2, vec(_V_MU_V))
            put("v", pp["v"])
            low = p_lora[:, 0:LORA_WIDTH] + _shift_rows(p_lora, prev_lora)[:, LORA_WIDTH:2 * LORA_WIDTH]
            lane = lax.broadcasted_iota(jnp.int32, low.shape, 1)
            act = jnp.where(lane < DECAY_LORA, jnp.tanh(low),
                            jnp.where(lane < DECAY_LORA + AAA_LORA, low, jax.nn.sigmoid(low)))
            pp["second"] = _dot(act.astype(BF16), w2_ref[...])

        def prep_b():
            second = pp["second"]
            pp["ld"] = -math.exp(-0.5) * jax.nn.sigmoid(second[:, 0:RWKV_WIDTH] + vec(_V_W0))
            pp["a"] = jax.nn.sigmoid(second[:, RWKV_WIDTH:2 * RWKV_WIDTH] + vec(_V_A0))
            put("gate", second[:, 2 * RWKV_WIDTH:3 * RWKV_WIDTH])
            pp["kk"] = pp["k"] * vec(_V_KK)
            pp["kk_ss"] = head_sum(pp["kk"] * pp["kk"])
            pp["cum"] = _split_dot(tril_ref[...], pp["ld"])

        def prep_c():
            pp["kk"] = pp["kk"] * lax.rsqrt(jnp.maximum(pp["kk_ss"], 1e-24))
            pp["k2"] = pp["k"] * (1.0 + (pp["a"] - 1.0) * vec(_V_KA))
            put("bonus", head_sum(pp["r"] * pp["k2"] * vec(_V_RK)) * pp["v"])

        def prep_d():
            cum, ld, kk, k2 = pp["cum"], pp["ld"], pp["kk"], pp["k2"]
            b = kk * pp["a"]
            cum_end = jnp.concatenate(
                [jnp.broadcast_to(cum[(c + 1) * C - 1:(c + 1) * C, :], (C, RWKV_WIDTH))
                 for c in range(STG // C)], axis=0)
            e_neg = jnp.exp(-cum)
            e_end = jnp.exp(cum_end - cum)
            put("a_t", -kk * jnp.exp(cum - ld))
            put("r_t", pp["r"] * jnp.exp(cum))
            put("b_t", b * e_neg)
            put("k_t", k2 * e_neg)
            put("b_w", b * e_end)
            put("k_w", k2 * e_end)
            put("w_c", jnp.exp(cum_end))

        return pp, [prep_a, prep_b, prep_c, prep_d]

    def run_all(thunks):
        for thunk in thunks:
            thunk()

    zero_prev = lambda: (jnp.zeros((1, N_RKV), F32), jnp.zeros((1, N_LORA), F32))

    @pl.when((pl.program_id(0) == 0) & (pl.program_id(1) == 0))
    def _():
        pieces = []
        for s in range(n_stages):
            front, ret = project(xfirst_ref.at[stage_rows[s], :], s)
            n_rkv = N_RKV // PROJ_PIECE
            pieces.append(front[:n_rkv] + ret + front[n_rkv:])
        copies = [pltpu.make_async_copy(win_hbm.at[:, pl.ds(c0, PROJ_PIECE)], wstage_ref.at[i % 2], wsem.at[i % 2])
                  for i, c0 in enumerate(range(0, N_RKV + N_RET, PROJ_PIECE))]
        copies[0].start()
        for i in range(len(_PROJ_PIECES)):
            if i < len(copies):
                if i + 1 < len(copies):
                    copies[i + 1].start()
                copies[i].wait()
                win_ref[:, i * PROJ_PIECE:(i + 1) * PROJ_PIECE] = wstage_ref[i % 2].astype(BF16)
            for s in range(n_stages):
                pieces[s][i]()
        run_all(prepare(0, zero_prev, store=True)[1])

    ret_state = [rstate_ref[h] for h in range(RET_HEADS)]

    def ret_unit(c, h):
        RC, RD = RET_CHUNK, RET_HEAD_DIM
        s, c_loc = divmod(c * RC, STG)
        rows = slice(c * RC, (c + 1) * RC)
        loc = slice(c_loc, c_loc + RC)
        row_i = lax.broadcasted_iota(jnp.int32, (RC, 1), 0).astype(F32)
        lg = _LOG_GAMMA[h]
        cos, sin = cos_ref[rows, :], sin_ref[rows, :]
        rot = lambda xh: xh * cos + pltpu.roll(xh, RD // 2, 1) * sin

        def part(j):
            c0 = j * RET_WIDTH + h * RD
            if s == 0:
                return pret0_ref[loc, c0:c0 + RD]
            return proj1_ref[loc, N_RKV + c0:N_RKV + c0 + RD]

        q = rot(part(0))
        kr = rot(part(1)) * (RD ** -0.5)
        vr = part(2)
        gt = part(3)
        qb, kb, vb = q.astype(BF16), kr.astype(BF16), vr.astype(BF16)
        scores = (_dot_nt(qb, kb) * dmask_ref[h]).astype(BF16)
        kv = _dot_tn(kb, (vr * jnp.exp((RC - 1.0 - row_i) * lg)).astype(BF16))
        inter = _dot((q * jnp.exp((row_i + 1.0) * lg)).astype(BF16), ret_state[h].astype(BF16))
        ret_state[h] = ret_state[h] * _GAMMA_C[h] + kv

        def second_half():
            y = _dot(scores, vb) + inter
            mu = jnp.mean(y, axis=-1, keepdims=True)
            d = y - mu
            var = jnp.mean(d * d, axis=-1, keepdims=True)
            yn = d * lax.rsqrt(var + RET_GN_EPS) * gnw_ref[:, h * RD:(h + 1) * RD]
            yret_ref[rows, h * RD:(h + 1) * RD] = (gt * jax.nn.sigmoid(gt) * yn).astype(BF16)
        return second_half

    ret_units = [(c, h) for c in range(TT // RET_CHUNK) for h in range(RET_HEADS)]
    ret_pending = []
    slot_queue = []

    def fill():
        if ret_pending:
            ret_pending.pop(0)()
        if ret_units:
            ret_pending.append(ret_unit(*ret_units.pop(0)))
        if slot_queue:
            for thunk in slot_queue.pop(0):
                thunk()

    c_bits = C.bit_length() - 1
    hd_bits = HD.bit_length() - 1
    assert C == 1 << c_bits and HD == 1 << hd_bits and W == GC
    sr = lax.broadcasted_iota(jnp.int32, (GC, W), 0)
    sl_ = lax.broadcasted_iota(jnp.int32, (GC, W), 1)
    stack_mask = (sr >> c_bits) == (sl_ >> hd_bits)
    ti = lax.broadcasted_iota(jnp.int32, (C, GC), 0)
    si = lax.broadcasted_iota(jnp.int32, (C, GC), 1) & (C - 1)
    strict = ti > si
    incl = ti >= si
    eye_c = (ti == si).astype(F32)
    same_block = {}
    size = INV_BASE
    while size <= C:
        bits = size.bit_length() - 1
        same_block[size] = (ti >> bits) == (si >> bits)
        size *= 2
    wi = lax.broadcasted_iota(jnp.int32, (W, W), 0)
    wj = lax.broadcasted_iota(jnp.int32, (W, W), 1)
    eye_w = wi == wj

    def stack(x):
        return jnp.where(stack_mask, jnp.concatenate([x] * G, axis=0), 0.0)

    blk = lambda z, c, g: z[c * C:(c + 1) * C, g * W:(g + 1) * W]

    def chains_of(p):
        st = []
        for c in range(STG // C):
            for g in range(n_groups):
                a_c, r_c = blk(p["a_t"], c, g), blk(p["r_t"], c, g)
                st.append(dict(
                    ar=jnp.concatenate([a_c, r_c], axis=0).astype(BF16), r32=r_c,
                    a_s=stack(a_c).astype(BF16),
                    bk_s=jnp.concatenate([stack(blk(p["b_t"], c, g)), stack(blk(p["k_t"], c, g))],
                                         axis=0).astype(BF16),
                    bw_t=stack(blk(p["b_w"], c, g)).T.astype(BF16),
                    kw_t=stack(blk(p["k_w"], c, g)).T.astype(BF16),
                    v_s=stack(blk(p["v"], c, g)).astype(BF16), w_end=blk(p["w_c"], c, g)[0:1, :]))
        for s in st:
            prod = _dot_nt(s["ar"], s["bk_s"])
            s["n"] = jnp.where(strict, prod[0:C, 0:GC], 0.0)
            s["a_ak"] = jnp.where(strict, prod[0:C, GC:], 0.0).astype(BF16)
            s["a_rb"] = jnp.where(incl, prod[C:, 0:GC], 0.0).astype(BF16)
            s["a_rk"] = jnp.where(incl, prod[C:, GC:], 0.0).astype(BF16)
        fill()
        for s in st:
            res = _dot(jnp.concatenate([s["a_ak"], s["a_rk"], s["kw_t"]], axis=0), s["v_s"])
            s["av_s"] = stack(res[0:C]).astype(BF16)
            s["rkv"] = res[C:2 * C]
            s["kwv"] = res[2 * C:]
        fill()
        assert INV_BASE == 8 and C == 64
        for s in st:
            nd = jnp.where(same_block[8], s["n"], 0.0)
            s["t"] = eye_c + nd
            s["p"] = _dot(nd.astype(BF16), stack(nd).astype(BF16))
        fill()
        for s in st:
            res = _dot(jnp.concatenate([s["t"], s["p"]], axis=0).astype(BF16), stack(s["p"]).astype(BF16))
            s["t"] = s["t"] + res[0:C]
            s["p"] = res[C:]
        for s in st:
            s["t"] = s["t"] + _dot(s["t"].astype(BF16), stack(s["p"]).astype(BF16))
        fill()
        for s in st:
            b_off = [jnp.where(same_block[2 * size] & ~same_block[size], s["n"], 0.0) for size in (8, 16, 32)]
            res = _dot(jnp.concatenate(b_off, axis=0).astype(BF16), stack(s["t"]).astype(BF16))
            s["x"], s["u16"], s["u32"] = res[0:C], res[C:2 * C], res[2 * C:]
        fill()
        for s in st:
            res = _dot(jnp.concatenate([s["t"], s["u16"], s["u32"]], axis=0).astype(BF16),
                       stack(s["x"]).astype(BF16))
            s["t"], s["x"], s["u32"] = s["t"] + res[0:C], s["u16"] + res[C:2 * C], s["u32"] + res[2 * C:]
        fill()
        for s in st:
            res = _dot(jnp.concatenate([s["t"], s["u32"]], axis=0).astype(BF16), stack(s["x"]).astype(BF16))
            s["t"], s["x"] = s["t"] + res[0:C], s["u32"] + res[C:]
        fill()
        for s in st:
            s["t"] = (s["t"] + _dot(s["t"].astype(BF16), stack(s["x"]).astype(BF16))).astype(BF16)
        fill()
        for s in st:
            s["x1_s"] = stack(_dot(s["t"], s["a_s"])).astype(BF16)
            s["x2_s"] = stack(_dot(s["t"], s["av_s"])).astype(BF16)
        for s in st:
            lhs = jnp.concatenate([s["a_rb"], s["bw_t"]], axis=0)
            o1 = _dot(lhs, s["x1_s"])
            o2 = _dot(lhs, s["x2_s"])
            s["q"] = (s["r32"] + o1[0:C]).astype(BF16)
            s["m"] = (o1[C:] + jnp.where(eye_w, s["w_end"], 0.0)).astype(BF16)
            s["y0"] = o2[0:C] + s["rkv"]
            s["g"] = o2[C:] + s["kwv"]
        return st

    def state_steps(st, h_cur, fillers):
        y_rows = []
        n_chunks = STG // C
        per_step = -(-len(fillers) // n_chunks)
        for c in range(n_chunks):
            y_lanes = []
            for g in range(n_groups):
                s = st[c * n_groups + g]
                res = _dot(jnp.concatenate([s["q"], s["m"]], axis=0), h_cur[g].astype(BF16))
                y_lanes.append(res[0:C] + s["y0"])
                h_cur[g] = res[C:] + s["g"]
            y_rows.append(jnp.concatenate(y_lanes, axis=1))
            run_all(fillers[c * per_step:(c + 1) * per_step])
        return jnp.concatenate(y_rows, axis=0)

    def finish(rows, y, p):
        inv_hd = 1.0 / HD
        mu = head_sum(y) * inv_hd
        d = y - mu
        var = head_sum(d * d) * inv_hd
        yn = d * lax.rsqrt(var + RWKV_GN_EPS) * vec(_V_LNW) + vec(_V_LNB)
        out_ref[rows, :] = ((yn + p["bonus"]) * p["gate"]).astype(BF16)

    assert n_stages == 2
    h_cur = [h_ref[g] for g in range(n_groups)]
    prep_s0 = {name: prep0_ref.at[i] for i, name in enumerate(_PREP_NAMES)}
    prep_s0["bonus"], prep_s0["gate"] = prep_s0["bonus"][...], prep_s0["gate"][...]
    last_s1 = (proj1_ref[STG - 1:STG, 0:N_RKV], proj1_ref[STG - 1:STG, N_RKV + N_RET:N_PROJ])
    last_step_of_row = pl.program_id(1) == pl.num_programs(1) - 1

    def chains_with_fillers(p, thunks):
        for t in thunks:
            slot_queue.extend([[t], []])
        st = chains_of(p)
        while slot_queue:
            fill()
        return st

    def interleave(a, b):
        return [t for pair in zip(a, b) for t in pair]

    front_n0, ret_n0 = project(xnext_ref.at[stage_rows[0], :], 0)
    chains_s0 = chains_with_fillers(prep_s0, front_n0)

    prev_s1 = lambda: (prev_ref[SUBLANES - 1:SUBLANES, 0:N_RKV],
                       prev_ref[SUBLANES - 1:SUBLANES, N_RKV:N_RKV + N_LORA])
    prep_s1, thunks = prepare(1, prev_s1, store=False)
    y_s0 = state_steps(chains_s0, h_cur, interleave(thunks, ret_n0))

    front_n1, ret_n1 = project(xnext_ref.at[stage_rows[1], :], 1)
    chains_s1 = chains_with_fillers(prep_s1, front_n1)

    finish(stage_rows[0], y_s0, prep_s0)
    prev_n0 = lambda: tuple(jnp.where(last_step_of_row, 0.0, z) for z in last_s1)
    y_s1 = state_steps(chains_s1, h_cur, interleave(prepare(0, prev_n0, store=True)[1], ret_n1))
    finish(stage_rows[1], y_s1, prep_s1)
    while ret_units or ret_pending:
        fill()
    for g in range(n_groups):
        h_ref[g] = h_cur[g]
    for h in range(RET_HEADS):
        rstate_ref[h] = ret_state[h]


def _mixer(x2, g_mix, w_in, w_low, vecs, w2cat, seg, tril, freq, gn_w, later_weights, bsz, seq):
    tt = MIX_TILE
    nt = seq // tt
    w = RWKV_GROUP_HEADS * RWKV_HEAD_DIM
    n_groups = RWKV_WIDTH // w
    cos, sin, dmask = _ret_tables(freq, seq)
    row = lambda b, t: (b * nt + t, 0)
    cast_specs = []
    for m in later_weights:
        rows = m.shape[0] // (bsz * nt)
        share = 1 if rows % (2 * SUBLANES) == 0 else 2
        assert (rows * share) % (2 * SUBLANES) == 0 and m.shape[0] % (rows * share) == 0
        cast_specs.append(pl.BlockSpec((rows * share, m.shape[1]), lambda b, t, share=share: ((b * nt + t) // share, 0)))
    next_tile = lambda b, t: (jnp.minimum(b * nt + t + 1, bsz * nt - 1), 0)
    pos = lambda b, t: (t, 0)
    const = lambda b, t: (0, 0)
    return pl.pallas_call(
        _mixer_kernel,
        grid=(bsz, nt),
        in_specs=[
            pl.BlockSpec((tt, D_MODEL), const),
            pl.BlockSpec((tt, D_MODEL), next_tile),
            pl.BlockSpec((1, D_MODEL), const),
            pl.BlockSpec(memory_space=pl.ANY),
            pl.BlockSpec((D_MODEL, N_LORA), const, pipeline_mode=pl.Buffered(1)),
            pl.BlockSpec((_N_VEC_ROWS, RWKV_WIDTH), const),
            pl.BlockSpec((LORA_WIDTH, 3 * RWKV_WIDTH), const),
            pl.BlockSpec((LANES, LANES), const),
            pl.BlockSpec((MIX_STAGE, MIX_STAGE), const),
            pl.BlockSpec((tt, RET_HEAD_DIM), pos),
            pl.BlockSpec((tt, RET_HEAD_DIM), pos),
            pl.BlockSpec((RET_HEADS, RET_CHUNK, RET_CHUNK), lambda b, t: (0, 0, 0)),
            pl.BlockSpec((1, RET_WIDTH), const),
        ] + cast_specs,
        out_specs=[pl.BlockSpec((tt, RWKV_WIDTH), row), pl.BlockSpec((tt, RET_WIDTH), row)] + cast_specs,
        out_shape=[jax.ShapeDtypeStruct((bsz * seq, RWKV_WIDTH), BF16),
                   jax.ShapeDtypeStruct((bsz * seq, RET_WIDTH), BF16)]
                  + [jax.ShapeDtypeStruct(m.shape, BF16) for m in later_weights],
        scratch_shapes=[pltpu.VMEM((n_groups, w, w), F32),
                        pltpu.VMEM((RET_HEADS, RET_HEAD_DIM, RET_HEAD_DIM), F32),
                        pltpu.VMEM((SUBLANES, N_RKV + N_LORA), F32),
                        pltpu.VMEM((len(_PREP_NAMES), MIX_STAGE, RWKV_WIDTH), F32),
                        pltpu.VMEM((MIX_STAGE, N_RET), F32),
                        pltpu.VMEM((MIX_STAGE, N_PROJ), F32),
                        pltpu.VMEM((MIX_STAGE, N_RKV + N_LORA), F32),
                        pltpu.VMEM((D_MODEL, N_RKV + N_RET), BF16),
                        pltpu.VMEM((2, D_MODEL, PROJ_PIECE), F32),
                        pltpu.SemaphoreType.DMA((2,))],
        compiler_params=pltpu.CompilerParams(
            dimension_semantics=("arbitrary", "arbitrary"), vmem_limit_bytes=_vmem_limit(56 * 1024 * 1024)),
        name="mixer",
    )(x2, x2, g_mix, w_in, w_low, vecs, w2cat, seg, tril, cos, sin, dmask, gn_w, *later_weights)


def _ffn_kernel(yr_ref, yt_ref, x_ref, wo_ref, gf_ref, wg_ref, wu_ref, cw_ref, cb_ref, wd_ref, gl_ref,
                out_ref, carry_ref):
    sm = FFN_TILE // FFN_STAGES
    stages = [slice(i * sm, (i + 1) * sm) for i in range(FFN_STAGES)]

    @pl.when(pl.program_id(1) == 0)
    def _():
        carry_ref[...] = jnp.zeros_like(carry_ref)

    mix = [_dot(yr_ref[s, :], wo_ref[0:RWKV_WIDTH, :]) + _dot(yt_ref[s, :], wo_ref[RWKV_WIDTH:, :])
           for s in stages]
    acc = [x_ref[s, :] + m for s, m in zip(stages, mix)]
    hb = [_rms_norm(x1, gf_ref[...]).astype(BF16) for x1 in acc]

    cols = D_FF // FFN_COL_SPLIT
    for j in range(FFN_COL_SPLIT):
        cs = slice(j * cols, (j + 1) * cols)
        gate_up = [(_dot(h, wg_ref[:, cs]), _dot(h, wu_ref[:, cs])) for h in hb]
        prev = carry_ref[:, cs]
        p2, p1 = prev[SUBLANES - 2:SUBLANES - 1], prev[SUBLANES - 1:SUBLANES]
        hidden = []
        for gate, up in gate_up:
            row = lax.broadcasted_iota(jnp.int32, gate.shape, 0)
            g1 = jnp.where(row == 0, p1, pltpu.roll(gate, 1, 0))
            g2 = jnp.where(row == 0, p2, jnp.where(row == 1, p1, pltpu.roll(gate, 2, 0)))
            p2, p1 = gate[sm - 2:sm - 1], gate[sm - 1:sm]
            conv = cw_ref[0:1, cs] * g2 + cw_ref[1:2, cs] * g1 + cw_ref[2:3, cs] * gate + cb_ref[:, cs]
            hidden.append((conv * jax.nn.sigmoid(conv) * up).astype(BF16))
        carry_ref[:, cs] = gate_up[-1][0][sm - SUBLANES:sm]
        acc = [a + _dot(h, wd_ref[cs, :]) for a, h in zip(acc, hidden)]
    for s, a in zip(stages, acc):
        out_ref[s, :] = _rms_norm(a, gl_ref[...])


def _ffn(y_rwkv, y_ret, x2, w_out, g_ffn, w_gate, w_up, conv_w, conv_b, w_down, g_final, bsz, seq):
    tm = FFN_TILE
    nt = seq // tm
    row = lambda b, t: (b * nt + t, 0)
    const = lambda b, t: (0, 0)
    single = dict(pipeline_mode=pl.Buffered(1))
    weights = (D_MODEL * D_MODEL + 3 * D_MODEL * D_FF) * 2
    vmem = weights + 4 * tm * D_MODEL * 4 + 8 * tm * D_FF * 4 + 8 * 1024 * 1024
    return pl.pallas_call(
        _ffn_kernel,
        grid=(bsz, nt),
        in_specs=[
            pl.BlockSpec((tm, RWKV_WIDTH), row),
            pl.BlockSpec((tm, RET_WIDTH), row),
            pl.BlockSpec((tm, D_MODEL), row),
            pl.BlockSpec((D_MODEL, D_MODEL), const, **single),
            pl.BlockSpec((1, D_MODEL), const),
            pl.BlockSpec((D_MODEL, D_FF), const, **single),
            pl.BlockSpec((D_MODEL, D_FF), const, **single),
            pl.BlockSpec((SUBLANES, D_FF), const),
            pl.BlockSpec((1, D_FF), const),
            pl.BlockSpec((D_FF, D_MODEL), const, **single),
            pl.BlockSpec((1, D_MODEL), const),
        ],
        out_specs=pl.BlockSpec((tm, D_MODEL), row),
        out_shape=jax.ShapeDtypeStruct((bsz * seq, D_MODEL), F32),
        scratch_shapes=[pltpu.VMEM((SUBLANES, D_FF), F32)],
        compiler_params=pltpu.CompilerParams(
            dimension_semantics=("arbitrary", "arbitrary"), vmem_limit_bytes=_vmem_limit(vmem)),
        name="ffn",
    )(y_rwkv, y_ret, x2, w_out, g_ffn, w_gate, w_up, conv_w, conv_b, w_down, g_final)


def _block_ones(n, block):
    i = np.arange(n)
    return (i[:, None] // block) == (i[None, :] // block)


def kernel(x, norm_mix_g, w_in, rwkv_mu_r, rwkv_mu_k, rwkv_mu_v, rwkv_mu_w, rwkv_mu_a, rwkv_mu_g, rwkv_w0, rwkv_w1, rwkv_w2, rwkv_a0, rwkv_a1, rwkv_a2, rwkv_g1, rwkv_g2, rwkv_k_k, rwkv_k_a, rwkv_r_k, rwkv_lnx_w, rwkv_lnx_b, ret_gn_w, w_out, norm_ffn_g, ffn_w_gate, ffn_w_up, ffn_conv_w, ffn_conv_b, ffn_w_down, norm_final_g):
    bsz, seq, d = x.shape
    assert d == D_MODEL and seq % MIX_TILE == 0 and seq % FFN_TILE == 0
    assert norm_mix_g.shape[0] == 1, "one layer"
    x2 = x.reshape(bsz * seq, d)
    row = lambda p: p.reshape(1, -1)

    w_low = _fold_lora(rwkv_w1[0], rwkv_a1[0], rwkv_g1[0], rwkv_mu_w[0], rwkv_mu_a[0], rwkv_mu_g[0])

    vec_rows = [rwkv_mu_r[0], rwkv_mu_k[0], rwkv_mu_v[0], rwkv_w0[0], rwkv_a0[0], rwkv_k_k[0], rwkv_k_a[0],
                rwkv_r_k[0].reshape(-1), rwkv_lnx_w[0], rwkv_lnx_b[0]]
    vecs = jnp.zeros((_N_VEC_ROWS, RWKV_WIDTH), F32).at[:len(vec_rows)].set(jnp.stack(vec_rows))
    w2cat = jnp.zeros((LORA_WIDTH, 3 * RWKV_WIDTH), BF16)
    w2cat = w2cat.at[0:DECAY_LORA, 0:RWKV_WIDTH].set(rwkv_w2[0].astype(BF16))
    w2cat = w2cat.at[DECAY_LORA:DECAY_LORA + AAA_LORA, RWKV_WIDTH:2 * RWKV_WIDTH].set(rwkv_a2[0].astype(BF16))
    w2cat = w2cat.at[DECAY_LORA + AAA_LORA:, 2 * RWKV_WIDTH:].set(rwkv_g2[0].astype(BF16))
    seg = jnp.asarray(_block_ones(LANES, RWKV_HEAD_DIM), BF16)
    ti = np.arange(MIX_STAGE)
    tril = jnp.asarray(_block_ones(MIX_STAGE, RWKV_CHUNK) & (ti[:, None] >= ti[None, :]), BF16)
    half = RET_HEAD_DIM // 2
    inv_freq = ROPE_BASE ** (-jnp.arange(half, dtype=F32) / half)
    freq = jnp.concatenate([inv_freq, inv_freq]).reshape(1, RET_HEAD_DIM)
    y_rwkv, y_ret, wo_b, wg_b, wu_b, wd_b = _mixer(
        x2, row(norm_mix_g[0]), w_in[0], w_low, vecs, w2cat, seg, tril, freq, row(ret_gn_w[0]),
        (w_out[0], ffn_w_gate[0], ffn_w_up[0], ffn_w_down[0]), bsz, seq)

    conv_w = jnp.zeros((SUBLANES, D_FF), F32).at[0:3].set(ffn_conv_w[0][:, 0, :])
    out = _ffn(y_rwkv, y_ret, x2, wo_b, row(norm_ffn_g[0]), wg_b, wu_b, conv_w,
               row(ffn_conv_b[0]), wd_b, row(norm_final_g), bsz, seq)
    return out.reshape(bsz, seq, d)
```
